```python
import math
import jax, jax.numpy as jnp
from jax import lax
import numpy as np

D_MODEL = 1024
BATCH = 2
SEQ = 8192
DEPTH = 2

GRID_W = 64
CTX_LEN = 256
N_EVEN = (DEPTH + 1) // 2
N_ODD = DEPTH // 2
D_FF = -(-8 * D_MODEL // (3 * 256)) * 256
RMS_EPS = 1e-6
FOURIER_CH = D_MODEL // 2
FOURIER_GROUPS = 4
FOURIER_GDIM = FOURIER_CH // FOURIER_GROUPS
HY_CH = D_MODEL - FOURIER_CH
HY_ORDER = 2
HY_SHORT = 3
HY_BANDS = 16
HY_EMB = 1 + 2 * HY_BANDS
HY_FILTER_HID = 64
HY_FAST_DECAY = 0.3
HY_SLOW_DECAY = 1.5
HY_DECAY_TARGET = 1e-2
AB_IN = FOURIER_CH + (HY_ORDER + 1) * HY_CH
NA_HEADS = 16
NA_HEAD_DIM = D_MODEL // NA_HEADS
NA_KH_MAX = 8
NA_KW = 16
NA_QCOLS = NA_KW
NA_KCOLS = 2 * NA_KW

kernel_name = "hybrid_fnet_hyena_natten_dit"


def rms_norm(x, g):
    x32 = x.astype(jnp.float32)
    y = x32 * lax.rsqrt(jnp.mean(x32 * x32, axis=-1, keepdims=True) + RMS_EPS)
    return (y * g.astype(jnp.float32)).astype(x.dtype)


def ada_mod(cond, w, b):
    m = (jax.nn.silu(cond) @ w + b)[..., None, :]
    return jnp.split(m, 6, axis=-1)


def modulate(h, shift, scale):
    return h * (1.0 + scale) + shift


def swiglu(u, w_gate, w_up, w_down):
    return (jax.nn.silu(u @ w_gate) * (u @ w_up)) @ w_down


def fourier_mix(p):
    B_, L, _ = p.shape
    g = p.astype(jnp.float32).reshape(B_, L, FOURIER_GROUPS, FOURIER_GDIM).transpose(0, 2, 1, 3)
    f = jnp.fft.fft2(g, norm="ortho").real
    return f.transpose(0, 2, 1, 3).reshape(B_, L, FOURIER_CH).astype(p.dtype)


def short_conv(u, w, b):
    K, C = w.shape
    pad = K // 2
    y = lax.conv_general_dilated(u, w[:, None, :].astype(u.dtype), (1,), [(pad, K - 1 - pad)],
                                 dimension_numbers=("NWC", "WIO", "NWC"), feature_group_count=C)
    return y + b.astype(u.dtype)


def hyena_filters(L, w1, b1, freq, w2, b2, w3):
    f32 = jnp.float32
    w1, b1, freq, w2, b2, w3 = (a.astype(f32) for a in (w1, b1, freq, w2, b2, w3))
    t01 = jnp.linspace(0.0, 1.0, L, dtype=f32)[:, None]
    bands = jnp.linspace(1e-4, HY_BANDS - 1, HY_BANDS, dtype=f32)[None, :]
    ang = (2.0 * math.pi / L) * jnp.arange(L, dtype=f32)[:, None] * bands
    z = jnp.concatenate([t01, jnp.cos(ang), -jnp.sin(ang)], axis=-1)
    h = jnp.sin(freq * (z @ w1 + b1))
    h = jnp.sin(freq * (h @ w2 + b2))
    h = (h @ w3).reshape(L, 2, HY_ORDER, HY_CH)
    max_decay = math.log(HY_DECAY_TARGET) / HY_FAST_DECAY
    min_decay = math.log(HY_DECAY_TARGET) / HY_SLOW_DECAY
    deltas = jnp.abs(jnp.linspace(min_decay, max_decay, HY_CH, dtype=f32))
    h = h * jnp.exp(-t01 * deltas)[:, None, None, :]
    h = h / jnp.sum(jnp.abs(h), axis=(0, 1), keepdims=True)
    fwd, bwd = h[:, 0], h[:, 1]
    return jnp.concatenate([fwd, jnp.zeros_like(fwd[:1]), bwd[:0:-1]], axis=0)


def long_conv(u, filt, skip):
    L = u.shape[1]
    uf = jnp.fft.rfft(u, n=2 * L, axis=1)
    ff = jnp.fft.rfft(filt, n=2 * L, axis=0)
    y = jnp.fft.irfft(uf * ff[None], n=2 * L, axis=1)[:, :L]
    return y + u * skip


def hyena_mix(p, conv_w, conv_b, w1, b1, freq, w2, b2, w3, skip):
    L = p.shape[1]
    z = short_conv(p, conv_w, conv_b).astype(jnp.float32)
    x1, x2, v = jnp.split(z, 3, axis=-1)
    filt = hyena_filters(L, w1, b1, freq, w2, b2, w3)
    skip = skip.astype(jnp.float32)
    v = x1 * long_conv(v, filt[:, 0], skip[0])
    v = x2 * long_conv(v, filt[:, 1], skip[1])
    return v.astype(p.dtype)


def mixer_ab(u, w_in, w_out, conv_w, conv_b, w1, b1, freq, w2, b2, w3, skip):
    p = u @ w_in
    y_f = fourier_mix(p[..., :FOURIER_CH])
    y_h = hyena_mix(p[..., FOURIER_CH:], conv_w, conv_b, w1, b1, freq, w2, b2, w3, skip)
    return jnp.concatenate([y_f, y_h], axis=-1) @ w_out


def dense_attention(q, k, v):
    s = jnp.einsum("bqhd,bkhd->bhqk", q, k).astype(jnp.float32) * (q.shape[-1] ** -0.5)
    p = jax.nn.softmax(s, axis=-1).astype(v.dtype)
    return jnp.einsum("bhqk,bkhd->bqhd", p, v)


def neighbourhood_attention(q, k, v, kc, vc, rpb):
    B_, L, H, dh = q.shape
    rows = L // GRID_W
    kh = min(NA_KH_MAX, rows)
    scale = dh ** -0.5
    n_chunk = GRID_W // NA_QCOLS
    qcol = np.arange(GRID_W).reshape(n_chunk, NA_QCOLS)
    kstart = np.clip(qcol[:, 0] - NA_KW // 2, 0, GRID_W - NA_KCOLS)
    kcol = kstart[:, None] + np.arange(NA_KCOLS)
    wstart = np.clip(qcol - NA_KW // 2, 0, GRID_W - NA_KW)
    col_ok = (kcol[:, None, :] >= wstart[:, :, None]) & (kcol[:, None, :] < wstart[:, :, None] + NA_KW)
    dc = np.clip(kcol[:, None, :] - qcol[:, :, None] + NA_KW - 1, 0, 2 * NA_KW - 2)
    qg = q.reshape(B_, rows, GRID_W, H, dh)
    kg = k.reshape(B_, rows, GRID_W, H, dh)
    vg = v.reshape(B_, rows, GRID_W, H, dh)
    n_loc = kh * NA_KCOLS

    def row_block(r):
        rs = jnp.clip(r - kh // 2, 0, rows - kh)
        q_r = lax.dynamic_index_in_dim(qg, r, axis=1, keepdims=False).reshape(B_, n_chunk, NA_QCOLS, H, dh)
        k_b = lax.dynamic_slice_in_dim(kg, rs, kh, axis=1)[:, :, kcol]
        v_b = lax.dynamic_slice_in_dim(vg, rs, kh, axis=1)[:, :, kcol]
        dr = rs + jnp.arange(kh) - r + (NA_KH_MAX - 1)
        bias = rpb[:, dr[None, None, :, None], dc[:, :, None, :]].astype(jnp.float32)
        s_loc = jnp.einsum("bcqhd,bicjhd->bhcqij", q_r, k_b).astype(jnp.float32) * scale + bias[None]
        s_loc = jnp.where(col_ok[:, :, None, :], s_loc, -jnp.inf)
        s_ctx = jnp.einsum("bcqhd,bjhd->bhcqj", q_r, kc).astype(jnp.float32) * scale
        s = jnp.concatenate([s_loc.reshape(B_, H, n_chunk, NA_QCOLS, n_loc), s_ctx], axis=-1)
        p = jax.nn.softmax(s, axis=-1).astype(v.dtype)
        p_loc = p[..., :n_loc].reshape(B_, H, n_chunk, NA_QCOLS, kh, NA_KCOLS)
        o = (jnp.einsum("bhcqij,bicjhd->bcqhd", p_loc, v_b)
             + jnp.einsum("bhcqj,bjhd->bcqhd", p[..., n_loc:], vc))
        return o.reshape(B_, GRID_W, H, dh)

    out = lax.map(row_block, jnp.arange(rows))
    return out.transpose(1, 0, 2, 3, 4).reshape(B_, L, H, dh)


def mixer_na(u_lat, u_ctx, w_qkv, w_out, rpb, with_ctx_out):
    B_, L, _ = u_lat.shape
    heads = lambda t: t.reshape(t.shape[0], t.shape[1], NA_HEADS, NA_HEAD_DIM)
    q, k, v = (heads(t) for t in jnp.split(u_lat @ w_qkv, 3, axis=-1))
    kc, vc = (heads(t) for t in jnp.split(u_ctx @ w_qkv[:, D_MODEL:], 2, axis=-1))
    y_lat = neighbourhood_attention(q, k, v, kc, vc, rpb).reshape(B_, L, D_MODEL) @ w_out
    if not with_ctx_out:
        return y_lat, None
    qc = heads(u_ctx @ w_qkv[:, :D_MODEL])
    y_ctx = dense_attention(qc, kc, vc).reshape(u_ctx.shape) @ w_out
    return y_lat, y_ctx


def setup_inputs(seed: int = 0) -> dict:
    key = jax.random.key(seed)
    ks = jax.random.split(key, 26)
    D = D_MODEL

    def nrm(k, shape, s):
        return jax.random.normal(k, shape, jnp.float32) * s

    return {
        "x": nrm(ks[0], (BATCH, SEQ, D), 1.0),
        "c": nrm(ks[1], (BATCH, D), 1.0),
        "ctx": nrm(ks[2], (BATCH, CTX_LEN, D), 1.0),
        "c_ctx": nrm(ks[3], (D,), 1.0),
        "ada_w": nrm(ks[4], (DEPTH, D, 6 * D), 0.5 * D ** -0.5),
        "ada_b": nrm(ks[5], (DEPTH, 6 * D), 0.02),
        "norm1_g": 1.0 + nrm(ks[6], (DEPTH, D), 0.02),
        "norm2_g": 1.0 + nrm(ks[7], (DEPTH, D), 0.02),
        "ffn_w_gate": nrm(ks[8], (DEPTH, D, D_FF), D ** -0.5),
        "ffn_w_up": nrm(ks[9], (DEPTH, D, D_FF), D ** -0.5),
        "ffn_w_down": nrm(ks[10], (DEPTH, D_FF, D), D_FF ** -0.5),
        "ab_w_in": nrm(ks[11], (N_EVEN, D, AB_IN), D ** -0.5),
        "ab_w_out": nrm(ks[12], (N_EVEN, FOURIER_CH + HY_CH, D), D ** -0.5),
        "hy_conv_w": nrm(ks[13], (N_EVEN, HY_SHORT, (HY_ORDER + 1) * HY_CH), HY_SHORT ** -0.5),
        "hy_conv_b": nrm(ks[14], (N_EVEN, (HY_ORDER + 1) * HY_CH), 0.02),
        "hy_f_w1": nrm(ks[15], (N_EVEN, HY_EMB, HY_FILTER_HID), HY_EMB ** -0.5),
        "hy_f_b1": nrm(ks[16], (N_EVEN, HY_FILTER_HID), 0.02),
        "hy_f_freq": 1.0 + nrm(ks[17], (N_EVEN, HY_FILTER_HID), 0.02),
        "hy_f_w2": nrm(ks[18], (N_EVEN, HY_FILTER_HID, HY_FILTER_HID), HY_FILTER_HID ** -0.5),
        "hy_f_b2": nrm(ks[19], (N_EVEN, HY_FILTER_HID), 0.02),
        "hy_f_w3": nrm(ks[20], (N_EVEN, HY_FILTER_HID, 2 * HY_ORDER * HY_CH), HY_FILTER_HID ** -0.5),
        "hy_bias": nrm(ks[21], (N_EVEN, HY_ORDER, HY_CH), 0.5),
        "na_w_qkv": nrm(ks[22], (N_ODD, D, 3 * D), D ** -0.5),
        "na_w_out": nrm(ks[23], (N_ODD, D, D), D ** -0.5),
        "na_rpb": nrm(ks[24], (N_ODD, NA_HEADS, 2 * NA_KH_MAX - 1, 2 * NA_KW - 1), 0.1),
        "final_g": 1.0 + nrm(ks[25], (D,), 0.02),
    }


def reference(x, c, ctx, c_ctx, ada_w, ada_b, norm1_g, norm2_g, ffn_w_gate, ffn_w_up, ffn_w_down,
              ab_w_in, ab_w_out, hy_conv_w, hy_conv_b, hy_f_w1, hy_f_b1, hy_f_freq, hy_f_w2, hy_f_b2,
              hy_f_w3, hy_bias, na_w_qkv, na_w_out, na_rpb, final_g):
    h_lat, h_ctx = x, ctx
    for i in range(DEPTH):
        last = i == DEPTH - 1
        even = i % 2 == 0
        j = i // 2
        need_ctx = (not last) or (not even)
        sh1, sc1, g1, sh2, sc2, g2 = ada_mod(c, ada_w[i], ada_b[i])
        u_lat = modulate(rms_norm(h_lat, norm1_g[i]), sh1, sc1)
        if need_ctx:
            csh1, csc1, cg1, csh2, csc2, cg2 = ada_mod(c_ctx, ada_w[i], ada_b[i])
            u_ctx = modulate(rms_norm(h_ctx, norm1_g[i]), csh1, csc1)
        if even:
            ab = (ab_w_in[j], ab_w_out[j], hy_conv_w[j], hy_conv_b[j], hy_f_w1[j], hy_f_b1[j],
                  hy_f_freq[j], hy_f_w2[j], hy_f_b2[j], hy_f_w3[j], hy_bias[j])
            y_lat = mixer_ab(u_lat, *ab)
            y_ctx = None if last else mixer_ab(u_ctx, *ab)
        else:
            y_lat, y_ctx = mixer_na(u_lat, u_ctx, na_w_qkv[j], na_w_out[j], na_rpb[j], not last)
        h_lat = h_lat + g1 * y_lat
        h_lat = h_lat + g2 * swiglu(modulate(rms_norm(h_lat, norm2_g[i]), sh2, sc2),
                                    ffn_w_gate[i], ffn_w_up[i], ffn_w_down[i])
        if not last:
            h_ctx = h_ctx + cg1 * y_ctx
            h_ctx = h_ctx + cg2 * swiglu(modulate(rms_norm(h_ctx, norm2_g[i]), csh2, csc2),
                                         ffn_w_gate[i], ffn_w_up[i], ffn_w_down[i])
    return rms_norm(h_lat, final_g)
```

```python
import functools
import math

import numpy as np
import jax
import jax.numpy as jnp
from jax import lax
from jax.experimental import pallas as pl
from jax.experimental.pallas import tpu as pltpu

F32 = jnp.float32
BF16 = jnp.bfloat16

D_MODEL = 1024
FOURIER_CH = 512
FOURIER_GROUPS = 4
GROUP_DIM = 128
HY_CH = 512
HY_BANDS = 16
HY_EMB = 33
HY_HID = 64
HY_FAST_DECAY = 0.3
HY_SLOW_DECAY = 1.5
HY_DECAY_TARGET = 1e-2
D_FF = 2816
NA_HEADS = 16
NA_HEAD_DIM = 64
NA_KH = 8
NA_KW = 16
GRID_W = 64
RMS_EPS = 1e-6
NEG_BIG = -1e30

LANES = 128
VMEM_LIMIT_BYTES = 56 * 1024 * 1024

_NT = (((1,), (1,)), ((), ()))
_TN = (((0,), (0,)), ((), ()))


def _params(*sem):
    return pltpu.CompilerParams(dimension_semantics=sem, vmem_limit_bytes=VMEM_LIMIT_BYTES)


def _dot(a, b):
    return jnp.dot(a, b, preferred_element_type=F32)


def _dot_hi(a, b):
    return jnp.dot(a, b, preferred_element_type=F32, precision=lax.Precision.HIGHEST)


def _resident(shape):
    nd = len(shape)
    return pl.BlockSpec(shape, lambda *_: (0,) * nd, pipeline_mode=pl.Buffered(1))


def _rms(x, g):
    ms = jnp.mean(x * x, axis=-1, keepdims=True)
    return x * lax.rsqrt(ms + RMS_EPS) * g


def _norm_mod(x, g, shift, scale):
    return _rms(x, g) * (1.0 + scale) + shift


def _silu(x):
    return x * jax.nn.sigmoid(x)


def _cs(num, den):
    ang = 2.0 * np.pi * (np.asarray(num, np.float64) % den) / den
    return np.cos(ang), np.sin(ang)


@functools.lru_cache(maxsize=None)
def _const_channel_dft():
    c, s = _cs(np.outer(np.arange(GROUP_DIM), np.arange(GROUP_DIM)), GROUP_DIM)
    return (np.concatenate([c, -s], 0) / math.sqrt(GROUP_DIM)).astype(np.float32)


@functools.lru_cache(maxsize=None)
def _const_fourier(a_len):
    n = a_len * LANES
    c, s = _cs(np.outer(np.arange(a_len), np.arange(a_len)), a_len)
    w1 = np.block([[c, s], [-s, c]])
    tc, ts = _cs(np.outer(np.arange(a_len), np.arange(LANES)), n)
    tw = np.stack([tc, -ts])
    c2, s2 = _cs(np.outer(np.arange(LANES), np.arange(LANES)), LANES)
    w2 = np.concatenate([c2, s2], 0) / math.sqrt(n)
    return w1.astype(np.float32), tw.astype(np.float32), w2.astype(np.float32)


@functools.lru_cache(maxsize=None)
def _const_hyena(a_len):
    k1n = 2 * a_len
    n = k1n * LANES
    c, s = _cs(np.outer(np.arange(k1n), np.arange(a_len)), k1n)
    w1 = np.block([[c, s], [-s, c]])
    w1f = np.concatenate([c, -s], 0)
    tc, ts = _cs(np.outer(np.arange(k1n), np.arange(LANES)), n)
    tw = np.stack([tc, -ts])
    c2, s2 = _cs(np.outer(np.arange(LANES), np.arange(LANES)), LANES)
    w2 = np.block([[c2, -s2], [s2, c2]])
    w2s = np.concatenate([c2, s2], 0)
    w2d = np.concatenate([-s2, c2], 0)
    w2inv = np.block([[c2, s2], [-s2, c2]])
    v1 = np.concatenate([c.T, s.T], 0) / n
    f = lambda m: m.astype(np.float32)
    return f(w1), f(w1f), f(tw), f(w2), f(w2s), f(w2d), f(w2inv), f(v1)


@functools.lru_cache(maxsize=None)
def _const_dense(l_len):
    n = 2 * l_len
    cl, sl = _cs(np.outer(np.arange(l_len), np.arange(l_len)), l_len)
    four_c, four_s = cl / math.sqrt(l_len), sl / math.sqrt(l_len)
    cf, sf = _cs(np.outer(np.arange(l_len), np.arange(n)), n)
    fwd = np.block([[cf, -sf], [sf, cf]])
    inv = np.block([[cf.T, sf.T], [-sf.T, cf.T]]) / n
    f = lambda m: m.astype(np.float32)
    return f(four_c), f(four_s), f(cf), f(-sf), f(fwd), f(inv)


@functools.lru_cache(maxsize=None)
def _const_filter_features(l_len):
    t01 = np.linspace(0.0, 1.0, l_len)
    bands = np.linspace(1e-4, HY_BANDS - 1, HY_BANDS)
    ang = (2.0 * np.pi / l_len) * np.arange(l_len)[:, None] * bands[None, :]
    z = np.concatenate([t01[:, None], np.cos(ang), -np.sin(ang)], -1)
    zt = np.zeros((LANES, l_len), np.float64)
    zt[:HY_EMB] = z.T
    max_decay = math.log(HY_DECAY_TARGET) / HY_FAST_DECAY
    min_decay = math.log(HY_DECAY_TARGET) / HY_SLOW_DECAY
    deltas = np.abs(np.linspace(min_decay, max_decay, HY_CH))
    return zt.astype(np.float32), t01[None, :].astype(np.float32), deltas[:, None].astype(np.float32)


def _chunk_time(m):
    rows, l_len = m.shape
    return np.ascontiguousarray(m.reshape(rows, l_len // LANES, LANES).transpose(1, 0, 2))


def _ada_kernel(c_ref, w_ref, b_ref, o_ref):
    s = _silu(c_ref[...]).astype(BF16)
    o_ref[...] = _dot(s, w_ref[...].astype(BF16)) + b_ref[...]


def _ada(cond8, ada_w, ada_b):
    depth, d, n6 = ada_w.shape
    tn = 1536
    return pl.pallas_call(
        _ada_kernel,
        grid=(depth, n6 // tn),
        in_specs=[pl.BlockSpec((8, d), lambda i, j: (0, 0)),
                  pl.BlockSpec((None, d, tn), lambda i, j: (i, 0, j)),
                  pl.BlockSpec((None, 1, tn), lambda i, j: (i, 0, j))],
        out_specs=pl.BlockSpec((None, 8, tn), lambda i, j: (i, 0, j)),
        out_shape=jax.ShapeDtypeStruct((depth, 8, n6), F32),
        compiler_params=_params("parallel", "parallel"),
        name="ada_mod",
    )(cond8, ada_w, ada_b.reshape(depth, 1, n6))


def _inproj_kernel(x_ref, g_ref, mod_ref, wt_ref, cdft_ref, zf_ref, ph_ref, *, chunked):
    d = D_MODEL
    u = _norm_mod(x_ref[...], g_ref[...], mod_ref[:, 0:d], mod_ref[:, d:2 * d]).astype(BF16)
    tl = u.shape[0]
    cdft = cdft_ref[...].astype(BF16)
    for g in range(FOURIER_GROUPS):
        rows = slice(g * GROUP_DIM, (g + 1) * GROUP_DIM)
        pt = lax.dot_general(wt_ref[rows, :], u, _NT, preferred_element_type=F32)
        z = _dot(cdft, pt.astype(BF16))
        if chunked:
            for j in range(tl // LANES):
                zf_ref[rows, 0, j, :] = z[0:GROUP_DIM, j * LANES:(j + 1) * LANES]
                zf_ref[rows, 1, j, :] = z[GROUP_DIM:, j * LANES:(j + 1) * LANES]
        else:
            zf_ref[0, rows, :] = z[0:GROUP_DIM]
            zf_ref[1, rows, :] = z[GROUP_DIM:]
    rb = 512
    for r0 in range(FOURIER_CH, wt_ref.shape[0], rb):
        pt = lax.dot_general(wt_ref[r0:r0 + rb, :], u, _NT, preferred_element_type=F32)
        if chunked:
            for j in range(tl // LANES):
                ph_ref[r0 - FOURIER_CH:r0 - FOURIER_CH + rb, j, :] = pt[:, j * LANES:(j + 1) * LANES]
        else:
            ph_ref[r0 - FOURIER_CH:r0 - FOURIER_CH + rb, :] = pt


def _inproj_latent(x, g, mod3, w_in_t, tl=1024):
    b, l, d = x.shape
    a_len = l // LANES
    nh = w_in_t.shape[0] - FOURIER_CH
    nch = tl // LANES
    return pl.pallas_call(
        functools.partial(_inproj_kernel, chunked=True),
        grid=(b, l // tl),
        in_specs=[pl.BlockSpec((None, tl, d), lambda i, t: (i, t, 0)),
                  pl.BlockSpec((1, d), lambda i, t: (0, 0)),
                  pl.BlockSpec((None, 1, 6 * d), lambda i, t: (i, 0, 0)),
                  _resident(w_in_t.shape),
                  _resident((2 * GROUP_DIM, GROUP_DIM))],
        out_specs=[pl.BlockSpec((None, FOURIER_CH, 2, nch, LANES), lambda i, t: (i, 0, 0, t, 0)),
                   pl.BlockSpec((None, nh, nch, LANES), lambda i, t: (i, 0, t, 0))],
        out_shape=[jax.ShapeDtypeStruct((b, FOURIER_CH, 2, a_len, LANES), F32),
                   jax.ShapeDtypeStruct((b, nh, a_len, LANES), F32)],
        compiler_params=_params("parallel", "parallel"),
        name="inproj_latent",
    )(x, g, mod3, w_in_t, jnp.asarray(_const_channel_dft()))


def _inproj_ctx(x, g, mod3, w_in_t):
    b, l, d = x.shape
    nh = w_in_t.shape[0] - FOURIER_CH
    return pl.pallas_call(
        functools.partial(_inproj_kernel, chunked=False),
        grid=(b,),
        in_specs=[pl.BlockSpec((None, l, d), lambda i: (i, 0, 0)),
                  pl.BlockSpec((1, d), lambda i: (0, 0)),
                  pl.BlockSpec((None, 1, 6 * d), lambda i: (i, 0, 0)),
                  _resident(w_in_t.shape),
                  _resident((2 * GROUP_DIM, GROUP_DIM))],
        out_specs=[pl.BlockSpec((None, 2, FOURIER_CH, l), lambda i: (i, 0, 0, 0)),
                   pl.BlockSpec((None, nh, l), lambda i: (i, 0, 0))],
        out_shape=[jax.ShapeDtypeStruct((b, 2, FOURIER_CH, l), F32),
                   jax.ShapeDtypeStruct((b, nh, l), F32)],
        compiler_params=_params("parallel"),
        name="inproj_ctx",
    )(x, g, mod3, w_in_t, jnp.asarray(_const_channel_dft()))


def _fourier_kernel(z_ref, w1_ref, tw_ref, w2_ref, o_ref, ys_ref):
    ct, _, a_len, _ = z_ref.shape
    w1 = w1_ref[...].astype(BF16)
    tr, ti = tw_ref[0], tw_ref[1]

    def stage1(c, carry):
        x = z_ref[c].reshape(2 * a_len, LANES).astype(BF16)
        y = _dot(w1, x)
        yr, yi = y[0:a_len], y[a_len:]
        ys_ref[c, :, 0:LANES] = (yr * tr - yi * ti).astype(BF16)
        ys_ref[c, :, LANES:] = (yr * ti + yi * tr).astype(BF16)
        return carry

    lax.fori_loop(0, ct, stage1, 0)
    y = ys_ref[...].reshape(ct * a_len, 2 * LANES)
    zr = _dot(y, w2_ref[...].astype(BF16))
    for c in range(ct):
        o_ref[c] = zr[c * a_len:(c + 1) * a_len].T


def _fourier_latent(zf, ct=32):
    b, ch, _, a_len, _ = zf.shape
    w1, tw, w2 = (jnp.asarray(m) for m in _const_fourier(a_len))
    out = pl.pallas_call(
        _fourier_kernel,
        grid=(b, ch // ct),
        in_specs=[pl.BlockSpec((None, ct, 2, a_len, LANES), lambda i, c: (i, c, 0, 0, 0)),
                  _resident(w1.shape), _resident(tw.shape), _resident(w2.shape)],
        out_specs=pl.BlockSpec((None, ct, LANES, a_len), lambda i, c: (i, c, 0, 0)),
        out_shape=jax.ShapeDtypeStruct((b, ch, LANES, a_len), F32),
        scratch_shapes=[pltpu.VMEM((ct, a_len, 2 * LANES), BF16)],
        compiler_params=_params("parallel", "parallel"),
        name="fourier_latent",
    )(zf, w1, tw, w2)
    return out.reshape(b, ch, LANES * a_len)


def _fourier_ctx_kernel(z_ref, c_ref, s_ref, o_ref):
    zr = z_ref[0].astype(BF16)
    zi = z_ref[1].astype(BF16)
    o_ref[...] = _dot(zr, c_ref[...].astype(BF16)) + _dot(zi, s_ref[...].astype(BF16))


def _fourier_ctx(zf):
    b, _, ch, l = zf.shape
    four_c, four_s = (jnp.asarray(m) for m in _const_dense(l)[:2])
    return pl.pallas_call(
        _fourier_ctx_kernel,
        grid=(b,),
        in_specs=[pl.BlockSpec((None, 2, ch, l), lambda i: (i, 0, 0, 0)),
                  _resident((l, l)), _resident((l, l))],
        out_specs=pl.BlockSpec((None, ch, l), lambda i: (i, 0, 0)),
        out_shape=jax.ShapeDtypeStruct((b, ch, l), F32),
        compiler_params=_params("parallel"),
        name="fourier_ctx",
    )(zf, four_c, four_s)


def _filter_mlp(zt, w1t, b1, freq, w2t, b2):
    h = jnp.sin(freq * (_dot_hi(w1t, zt) + b1))
    return jnp.sin(freq * (_dot_hi(w2t, h) + b2))


def _hfilter_kernel(zt_ref, t_ref, dl_ref, w1t_ref, b1_ref, fr_ref, w2t_ref, b2_ref, w3t_ref,
                    w1f_ref, tw_ref, w2s_ref, w2d_ref, o_ref, h2_ref, hf_ref, xs_ref, ys_ref):
    cf = o_ref.shape[0]
    a_len = xs_ref.shape[1]
    k1n = 2 * a_len

    @pl.when((pl.program_id(0) == 0) & (pl.program_id(1) == 0))
    def _():
        def mlp(a, carry):
            h2_ref[a] = _filter_mlp(zt_ref[a], w1t_ref[...], b1_ref[...], fr_ref[...], w2t_ref[...], b2_ref[...])
            return carry
        lax.fori_loop(0, a_len, mlp, 0)

    w3 = w3t_ref[...].reshape(2 * cf, HY_HID)
    dl = dl_ref[...]
    dl2 = jnp.concatenate([dl, dl], 0)

    def taps(a, acc):
        hv = _dot_hi(w3, h2_ref[a]) * jnp.exp(-(dl2 * t_ref[a]))
        hf_ref[a] = hv
        return acc + jnp.abs(hv)

    acc = lax.fori_loop(0, a_len, taps, jnp.zeros((2 * cf, LANES), F32))
    tot = jnp.sum(acc, axis=1, keepdims=True)
    inv = 1.0 / (tot[0:cf] + tot[cf:])
    lane = lax.broadcasted_iota(jnp.int32, (cf, LANES), 1)
    for a in range(a_len):
        f = hf_ref[a, 0:cf, :] * inv
        bk = hf_ref[a, cf:, :] * inv
        if a == 0:
            bk = jnp.where(lane == 0, 0.0, bk)
        xs_ref[0:cf, a, :] = f + bk
        xs_ref[cf:, a, :] = f - bk

    w1f = w1f_ref[...].astype(BF16)
    tr, ti = tw_ref[0], tw_ref[1]

    def stage1(c, carry):
        y = _dot(w1f, xs_ref[c].astype(BF16))
        yr, yi = y[0:k1n], y[k1n:]
        ys_ref[c, :, 0:LANES] = (yr * tr - yi * ti).astype(BF16)
        ys_ref[c, :, LANES:] = (yr * ti + yi * tr).astype(BF16)
        return carry

    lax.fori_loop(0, 2 * cf, stage1, 0)
    ysum = ys_ref[0:cf].reshape(cf * k1n, 2 * LANES)
    ydif = ys_ref[cf:].reshape(cf * k1n, 2 * LANES)
    o_ref[:, :, 0:LANES] = _dot(ysum, w2s_ref[...].astype(BF16)).reshape(cf, k1n, LANES)
    o_ref[:, :, LANES:] = _dot(ydif, w2d_ref[...].astype(BF16)).reshape(cf, k1n, LANES)


def _hfilter_latent(l, w1t, b1, freq, w2t, b2, w3t, cf=32):
    a_len = l // LANES
    k1n = 2 * a_len
    zt, t01, deltas = _const_filter_features(l)
    zt, t01, deltas = jnp.asarray(_chunk_time(zt)), jnp.asarray(_chunk_time(t01)), jnp.asarray(deltas)
    _, w1f, tw, _, w2s, w2d, _, _ = (jnp.asarray(m) for m in _const_hyena(a_len))
    orders = w3t.shape[1]
    return pl.pallas_call(
        _hfilter_kernel,
        grid=(orders, HY_CH // cf),
        in_specs=[_resident(zt.shape), _resident(t01.shape),
                  pl.BlockSpec((cf, 1), lambda o, c: (c, 0)),
                  _resident(w1t.shape), _resident(b1.shape), _resident(freq.shape),
                  _resident(w2t.shape), _resident(b2.shape),
                  pl.BlockSpec((2, None, cf, HY_HID), lambda o, c: (0, o, c, 0)),
                  _resident(w1f.shape), _resident(tw.shape), _resident(w2s.shape), _resident(w2d.shape)],
        out_specs=pl.BlockSpec((None, cf, k1n, 2 * LANES), lambda o, c: (o, c, 0, 0)),
        out_shape=jax.ShapeDtypeStruct((orders, HY_CH, k1n, 2 * LANES), F32),
        scratch_shapes=[pltpu.VMEM((a_len, HY_HID, LANES), F32),
                        pltpu.VMEM((a_len, 2 * cf, LANES), F32),
                        pltpu.VMEM((2 * cf, a_len, LANES), F32),
                        pltpu.VMEM((2 * cf, k1n, 2 * LANES), BF16)],
        compiler_params=_params("arbitrary", "arbitrary"),
        name="hyena_filter_latent",
    )(zt, t01, deltas, w1t, b1, freq, w2t, b2, w3t, w1f, tw, w2s, w2d)


def _hfilter_ctx_kernel(zt_ref, t_ref, dl_ref, w1t_ref, b1_ref, fr_ref, w2t_ref, b2_ref, w3t_ref,
                        cf_ref, sf_ref, o_ref):
    l = zt_ref.shape[1]
    h2 = _filter_mlp(zt_ref[...], w1t_ref[...], b1_ref[...], fr_ref[...], w2t_ref[...], b2_ref[...])
    ch = w3t_ref.shape[1]
    w3 = w3t_ref[...].reshape(2 * ch, HY_HID)
    dl = dl_ref[...]
    hv = _dot_hi(w3, h2) * jnp.exp(-(jnp.concatenate([dl, dl], 0) * t_ref[...]))
    tot = jnp.sum(jnp.abs(hv), axis=1, keepdims=True)
    inv = 1.0 / (tot[0:ch] + tot[ch:])
    f = hv[0:ch] * inv
    lane = lax.broadcasted_iota(jnp.int32, (ch, l), 1)
    bk = jnp.where(lane == 0, 0.0, hv[ch:] * inv)
    o_ref[:, 0:2 * l] = _dot((f + bk).astype(BF16), cf_ref[...].astype(BF16))
    o_ref[:, 2 * l:] = _dot((f - bk).astype(BF16), sf_ref[...].astype(BF16))


def _hfilter_ctx(l, w1t, b1, freq, w2t, b2, w3t):
    zt, t01, deltas = (jnp.asarray(m) for m in _const_filter_features(l))
    cfm, nsf = (jnp.asarray(m) for m in _const_dense(l)[2:4])
    orders = w3t.shape[1]
    return pl.pallas_call(
        _hfilter_ctx_kernel,
        grid=(orders,),
        in_specs=[_resident(zt.shape), _resident(t01.shape), _resident(deltas.shape),
                  _resident(w1t.shape), _resident(b1.shape), _resident(freq.shape),
                  _resident(w2t.shape), _resident(b2.shape),
                  pl.BlockSpec((2, None, HY_CH, HY_HID), lambda o: (0, o, 0, 0)),
                  _resident(cfm.shape), _resident(nsf.shape)],
        out_specs=pl.BlockSpec((None, HY_CH, 4 * l), lambda o: (o, 0, 0)),
        out_shape=jax.ShapeDtypeStruct((orders, HY_CH, 4 * l), F32),
        compiler_params=_params("parallel"),
        name="hyena_filter_ctx",
    )(zt, t01, deltas, w1t, b1, freq, w2t, b2, w3t, cfm, nsf)


def _short_conv_tile(x, w0, w1, w2, bias):
    a_len = x.shape[0]
    lane = lax.broadcasted_iota(jnp.int32, x.shape, 1)
    row = lax.broadcasted_iota(jnp.int32, x.shape, 0)
    xl = pltpu.roll(x, 1, 1)
    prev = jnp.where(lane == 0, jnp.where(row == 0, 0.0, pltpu.roll(xl, 1, 0)), xl)
    xr = pltpu.roll(x, LANES - 1, 1)
    nxt = jnp.where(lane == LANES - 1, jnp.where(row == a_len - 1, 0.0, pltpu.roll(xr, a_len - 1, 0)), xr)
    return w0 * prev + w1 * x + w2 * nxt + bias


def _hconv_kernel(cw_ref, cb_ref, sk_ref, x1_ref, x2_ref, v_ref, hf_ref,
                  w1_ref, tw_ref, w2_ref, w2i_ref, v1_ref, o_ref, vb_ref, ys_ref, us_ref, *, group):
    nb, ct, a_len, _ = v_ref.shape
    k1n = 2 * a_len
    nch3 = cb_ref.shape[0]
    c0 = pl.program_id(0) * ct
    w1 = w1_ref[...].astype(BF16)
    w2 = w2_ref[...].astype(BF16)
    w2i = w2i_ref[...].astype(BF16)
    v1 = v1_ref[...].astype(BF16)
    tr, ti = tw_ref[0], tw_ref[1]

    def sconv(ref, bi, c, ch):
        return _short_conv_tile(ref[bi, c], cw_ref[ch], cw_ref[nch3 + ch], cw_ref[2 * nch3 + ch], cb_ref[ch])

    def prep(c, carry):
        for bi in range(nb):
            vb_ref[bi, c] = sconv(v_ref, bi, c, 2 * HY_CH + c0 + c)
        return carry

    lax.fori_loop(0, ct, prep, 0)

    for order, gate_ref in enumerate((x1_ref, x2_ref)):
        def stage1(c, carry):
            x = jnp.concatenate([vb_ref[0, c], vb_ref[1, c]], 0).astype(BF16)
            y = _dot(w1, x)
            yr, yi = y[0:k1n], y[k1n:]
            ys_ref[c, :, 0:LANES] = (yr * tr - yi * ti).astype(BF16)
            ys_ref[c, :, LANES:] = (yr * ti + yi * tr).astype(BF16)
            return carry

        lax.fori_loop(0, ct, stage1, 0)

        def middle(g, carry):
            cs = pl.ds(g * group, group)
            z = _dot(ys_ref[cs].reshape(group * k1n, 2 * LANES), w2)
            h = hf_ref[order, cs].reshape(group * k1n, 2 * LANES)
            zr, zi = z[:, 0:LANES], z[:, LANES:]
            hr, hi = h[:, 0:LANES], h[:, LANES:]
            p = jnp.concatenate([zr * hr - zi * hi, zr * hi + zi * hr], 1).astype(BF16)
            u = _dot(p, w2i).reshape(group, k1n, 2 * LANES)
            ur, ui = u[:, :, 0:LANES], u[:, :, LANES:]
            us_ref[cs, :, 0:LANES] = (ur * tr + ui * ti).astype(BF16)
            us_ref[cs, :, LANES:] = (ui * tr - ur * ti).astype(BF16)
            return carry

        lax.fori_loop(0, ct // group, middle, 0)

        def stage1_inv(c, carry):
            q = _dot(v1, us_ref[c])
            conv = (q[0:a_len, 0:LANES] - q[a_len:, LANES:], q[a_len:, 0:LANES] + q[0:a_len, LANES:])
            skip = sk_ref[order * HY_CH + c0 + c]
            for bi in range(nb):
                gate = sconv(gate_ref, bi, c, order * HY_CH + c0 + c)
                res = gate * (conv[bi] + vb_ref[bi, c] * skip)
                if order == 0:
                    vb_ref[bi, c] = res
                else:
                    o_ref[bi, c] = res
            return carry

        lax.fori_loop(0, ct, stage1_inv, 0)


def _hconv_latent(ph, hf, conv_w, conv_b, skip, ct=16, group=4):
    b, nch3, a_len, _ = ph.shape
    assert b == 2, "the two batch entries ride as real/imaginary parts"
    k1n = 2 * a_len
    w1, _, tw, w2, _, _, w2i, v1 = (jnp.asarray(m) for m in _const_hyena(a_len))
    nblk = HY_CH // ct
    smem = pl.BlockSpec(memory_space=pltpu.SMEM)
    blk = lambda off: pl.BlockSpec((b, ct, a_len, LANES), lambda c: (0, c + off * nblk, 0, 0))
    return pl.pallas_call(
        functools.partial(_hconv_kernel, group=group),
        grid=(nblk,),
        in_specs=[smem, smem, smem, blk(0), blk(1), blk(2),
                  pl.BlockSpec((2, ct, k1n, 2 * LANES), lambda c: (0, c, 0, 0)),
                  _resident(w1.shape), _resident(tw.shape), _resident(w2.shape),
                  _resident(w2i.shape), _resident(v1.shape)],
        out_specs=pl.BlockSpec((b, ct, a_len, LANES), lambda c: (0, c, 0, 0)),
        out_shape=jax.ShapeDtypeStruct((b, HY_CH, a_len, LANES), F32),
        scratch_shapes=[pltpu.VMEM((b, ct, a_len, LANES), F32),
                        pltpu.VMEM((ct, k1n, 2 * LANES), BF16),
                        pltpu.VMEM((ct, k1n, 2 * LANES), BF16)],
        compiler_params=_params("parallel"),
        name="hyena_conv_latent",
    )(conv_w.reshape(-1), conv_b, skip.reshape(-1), ph, ph, ph, hf, w1, tw, w2, w2i, v1)


def _hconv_ctx_kernel(cw_ref, cb_ref, sk_ref, p_ref, hf_ref, fwd_ref, inv_ref, o_ref):
    nb, _, l = p_ref.shape
    ch = HY_CH
    lane = lax.broadcasted_iota(jnp.int32, (3 * ch, l), 1)
    cw = cw_ref[...]
    z = []
    for bi in range(nb):
        x = p_ref[bi]
        prev = jnp.where(lane == 0, 0.0, pltpu.roll(x, 1, 1))
        nxt = jnp.where(lane == l - 1, 0.0, pltpu.roll(x, l - 1, 1))
        z.append(cw[:, 0:1] * prev + cw[:, 1:2] * x + cw[:, 2:3] * nxt + cb_ref[...])
    fwd = fwd_ref[...].astype(BF16)
    inv = inv_ref[...].astype(BF16)
    cur = [zz[2 * ch:] for zz in z]
    for order in range(2):
        spec = _dot(jnp.concatenate(cur, 1).astype(BF16), fwd)
        h = hf_ref[order]
        zr, zi, hr, hi = spec[:, 0:2 * l], spec[:, 2 * l:], h[:, 0:2 * l], h[:, 2 * l:]
        prod = jnp.concatenate([zr * hr - zi * hi, zr * hi + zi * hr], 1).astype(BF16)
        y = _dot(prod, inv)
        sk = sk_ref[:, order:order + 1]
        cur = [z[bi][order * ch:(order + 1) * ch] * (y[:, bi * l:(bi + 1) * l] + cur[bi] * sk) for bi in range(nb)]
    for bi in range(nb):
        o_ref[bi] = cur[bi]


def _hconv_ctx(ph, hf, conv_w, conv_b, skip):
    b, nch3, l = ph.shape
    assert b == 2
    fwd, inv = (jnp.asarray(m) for m in _const_dense(l)[4:6])
    cw = jnp.pad(conv_w.T, ((0, 0), (0, 5)))
    return pl.pallas_call(
        _hconv_ctx_kernel,
        in_specs=[pl.BlockSpec(cw.shape, lambda: (0, 0)),
                  pl.BlockSpec((nch3, 1), lambda: (0, 0)),
                  pl.BlockSpec((HY_CH, 2), lambda: (0, 0)),
                  pl.BlockSpec(ph.shape, lambda: (0, 0, 0)),
                  pl.BlockSpec(hf.shape, lambda: (0, 0, 0)),
                  pl.BlockSpec(fwd.shape, lambda: (0, 0)),
                  pl.BlockSpec(inv.shape, lambda: (0, 0))],
        out_specs=pl.BlockSpec((b, HY_CH, l), lambda: (0, 0, 0)),
        out_shape=jax.ShapeDtypeStruct((b, HY_CH, l), F32),
        compiler_params=pltpu.CompilerParams(vmem_limit_bytes=VMEM_LIMIT_BYTES),
        name="hyena_conv_ctx",
    )(cw, conv_b.reshape(-1, 1), skip.T, ph, hf, fwd, inv)


def _mix_ffn_kernel(*refs, channel_major, final, ff_chunk):
    d = D_MODEL
    if channel_major:
        h_ref, yf_ref, yh_ref, wo_ref, mod_ref, g2_ref, wg_ref, wu_ref, wd_ref = refs[:9]
        rest = refs[9:]
        y = (lax.dot_general(yf_ref[...].astype(BF16), wo_ref[0:FOURIER_CH, :], _TN, preferred_element_type=F32)
             + lax.dot_general(yh_ref[...].astype(BF16), wo_ref[FOURIER_CH:, :], _TN, preferred_element_type=F32))
    else:
        h_ref, ya_ref, wo_ref, mod_ref, g2_ref, wg_ref, wu_ref, wd_ref = refs[:8]
        rest = refs[8:]
        y = _dot(ya_ref[...], wo_ref[...])
    if final:
        fg_ref, o_ref = rest
    else:
        (o_ref,) = rest
    h1 = h_ref[...] + mod_ref[:, 2 * d:3 * d] * y
    u = _norm_mod(h1, g2_ref[...], mod_ref[:, 3 * d:4 * d], mod_ref[:, 4 * d:5 * d]).astype(BF16)
    acc = jnp.zeros(h1.shape, F32)
    for j in range(D_FF // ff_chunk):
        cols = slice(j * ff_chunk, (j + 1) * ff_chunk)
        act = (_silu(_dot(u, wg_ref[:, cols])) * _dot(u, wu_ref[:, cols])).astype(BF16)
        acc = acc + _dot(act, wd_ref[cols, :])
    h2 = h1 + mod_ref[:, 5 * d:6 * d] * acc
    if final:
        h2 = _rms(h2, fg_ref[...])
    o_ref[...] = h2


def _mix_ffn(h, ys, wo, mod3, g2, wg, wu, wd, final_g=None, tl=512, ff_chunk=256):
    b, l, d = h.shape
    tl = min(tl, l)
    channel_major = len(ys) == 2
    tok = pl.BlockSpec((None, tl, d), lambda i, t: (i, t, 0))
    if channel_major:
        y_specs = [pl.BlockSpec((None, y.shape[1], tl), lambda i, t: (i, 0, t)) for y in ys]
    else:
        y_specs = [tok]
    in_specs = ([tok] + y_specs + [_resident(wo.shape), pl.BlockSpec((None, 1, 6 * d), lambda i, t: (i, 0, 0)),
                                    pl.BlockSpec((1, d), lambda i, t: (0, 0)),
                                    _resident(wg.shape), _resident(wu.shape), _resident(wd.shape)])
    args = [h, *ys, wo, mod3, g2, wg, wu, wd]
    if final_g is not None:
        in_specs.append(pl.BlockSpec((1, d), lambda i, t: (0, 0)))
        args.append(final_g)
    return pl.pallas_call(
        functools.partial(_mix_ffn_kernel, channel_major=channel_major, final=final_g is not None,
                          ff_chunk=ff_chunk),
        grid=(b, l // tl),
        in_specs=in_specs,
        out_specs=tok,
        out_shape=jax.ShapeDtypeStruct((b, l, d), F32),
        compiler_params=_params("parallel", "parallel"),
        name="mix_ffn_final" if final_g is not None else "mix_ffn",
    )(*args)


def _qkv_kernel(x_ref, g_ref, mod_ref, w_ref, q_ref, k_ref, v_ref):
    d = D_MODEL
    u = _norm_mod(x_ref[...], g_ref[...], mod_ref[:, 0:d], mod_ref[:, d:2 * d]).astype(BF16)
    q_ref[...] = (_dot(u, w_ref[:, 0:d]) * (NA_HEAD_DIM ** -0.5)).astype(BF16)
    k_ref[...] = _dot(u, w_ref[:, d:2 * d]).astype(BF16)
    v_ref[...] = _dot(u, w_ref[:, 2 * d:]).astype(BF16)


def _qkv(h, g, mod3, w, tl=1024):
    b, l, d = h.shape
    tl = min(tl, l)
    tok = pl.BlockSpec((None, tl, d), lambda i, t: (i, t, 0))
    return pl.pallas_call(
        _qkv_kernel,
        grid=(b, l // tl),
        in_specs=[tok, pl.BlockSpec((1, d), lambda i, t: (0, 0)),
                  pl.BlockSpec((None, 1, 6 * d), lambda i, t: (i, 0, 0)), _resident(w.shape)],
        out_specs=[tok, tok, tok],
        out_shape=[jax.ShapeDtypeStruct((b, l, d), BF16)] * 3,
        compiler_params=_params("parallel", "parallel"),
        name="qkv_proj",
    )(h, g, mod3, w)


def _bias_tiles_kernel(rpb_ref, o_ref):
    pair = pl.program_id(0)
    ndr, ndc = 2 * NA_KH - 1, 2 * NA_KW - 1
    q = lax.broadcasted_iota(jnp.int32, (GRID_W, LANES), 0)
    lane = lax.broadcasted_iota(jnp.int32, (GRID_W, LANES), 1)
    kcol = lane % GRID_W
    upper = lane >= GRID_W
    wstart = jnp.clip(q - NA_KW // 2, 0, GRID_W - NA_KW)
    ok = (kcol >= wstart) & (kcol < wstart + NA_KW)
    dc = kcol - q + (NA_KW - 1)
    for e in range(2):
        base = (2 * pair + e) * ndr * ndc
        for dr in range(ndr - 1):
            acc = jnp.zeros((GRID_W, LANES), F32)
            for dd in range(ndc):
                val = jnp.where(upper, rpb_ref[base + (dr + 1) * ndc + dd], rpb_ref[base + dr * ndc + dd])
                acc = jnp.where(dc == dd, val, acc)
            o_ref[e, dr] = jnp.where(ok, acc, NEG_BIG)


def _bias_tiles(rpb):
    heads, ndr, ndc = rpb.shape
    return pl.pallas_call(
        _bias_tiles_kernel,
        grid=(heads // 2,),
        in_specs=[pl.BlockSpec(memory_space=pltpu.SMEM)],
        out_specs=pl.BlockSpec((None, 2, ndr - 1, GRID_W, LANES), lambda p: (p, 0, 0, 0, 0)),
        out_shape=jax.ShapeDtypeStruct((heads // 2, 2, ndr - 1, GRID_W, LANES), F32),
        compiler_params=_params("parallel"),
        name="na_bias_tiles",
    )(rpb.reshape(-1))


def _na_kernel(q_ref, k_ref, v_ref, kc_ref, vc_ref, bt_ref, o_ref, *, rows_per_step, grid_rows):
    win = NA_KH * GRID_W
    lane = lax.broadcasted_iota(jnp.int32, (GRID_W, LANES), 1)
    kc = kc_ref[...]
    vc = vc_ref[...]
    rb = pl.program_id(2)

    def one_row(i, carry):
        r = rb * rows_per_step + i
        rs = jnp.clip(r - NA_KH // 2, 0, grid_rows - NA_KH)
        dl = r - rs
        q = q_ref[pl.ds(pl.multiple_of(i * GRID_W, GRID_W), GRID_W), :]
        koff = pl.multiple_of(rs * GRID_W, GRID_W)
        kw = k_ref[pl.ds(koff, win), :]
        vw = v_ref[pl.ds(koff, win), :]
        outs = []
        for e in range(2):
            head = (lane >= NA_HEAD_DIM) if e else (lane < NA_HEAD_DIM)
            qh = jnp.where(head, q, jnp.zeros_like(q))
            bias = jnp.concatenate([bt_ref[e, NA_KH - 1 - dl + 2 * j] for j in range(NA_KH // 2)], 1)
            sl = lax.dot_general(qh, kw, _NT, preferred_element_type=F32) + bias
            sc = lax.dot_general(qh, kc, _NT, preferred_element_type=F32)
            m = jnp.maximum(jnp.max(sl, axis=1, keepdims=True), jnp.max(sc, axis=1, keepdims=True))
            pl_, pc = jnp.exp(sl - m), jnp.exp(sc - m)
            den = jnp.sum(pl_, axis=1, keepdims=True) + jnp.sum(pc, axis=1, keepdims=True)
            o = _dot(pl_.astype(BF16), vw) + _dot(pc.astype(BF16), vc)
            outs.append(o / den)
        o_ref[pl.ds(pl.multiple_of(i * GRID_W, GRID_W), GRID_W), :] = jnp.where(
            lane < NA_HEAD_DIM, outs[0], outs[1]).astype(o_ref.dtype)
        return carry

    lax.fori_loop(0, rows_per_step, one_row, 0)


def _na(q, k, v, kc, vc, bias_tiles, rows_per_step=8):
    b, l, d = q.shape
    lc = kc.shape[1]
    grid_rows = l // GRID_W
    npairs = d // LANES
    tq = rows_per_step * GRID_W
    ndr = bias_tiles.shape[2]
    return pl.pallas_call(
        functools.partial(_na_kernel, rows_per_step=rows_per_step, grid_rows=grid_rows),
        grid=(npairs, b, grid_rows // rows_per_step),
        in_specs=[pl.BlockSpec((None, tq, LANES), lambda p, i, t: (i, t, p)),
                  pl.BlockSpec((None, l, LANES), lambda p, i, t: (i, 0, p)),
                  pl.BlockSpec((None, l, LANES), lambda p, i, t: (i, 0, p)),
                  pl.BlockSpec((None, lc, LANES), lambda p, i, t: (i, 0, p)),
                  pl.BlockSpec((None, lc, LANES), lambda p, i, t: (i, 0, p)),
                  pl.BlockSpec((None, 2, ndr, GRID_W, LANES), lambda p, i, t: (p, 0, 0, 0, 0))],
        out_specs=pl.BlockSpec((None, tq, LANES), lambda p, i, t: (i, t, p)),
        out_shape=jax.ShapeDtypeStruct((b, l, d), BF16),
        compiler_params=_params("parallel", "parallel", "parallel"),
        name="neighbourhood_attention",
    )(q, k, v, kc, vc, bias_tiles)


def kernel(x, c, ctx, c_ctx, ada_w, ada_b, norm1_g, norm2_g, ffn_w_gate, ffn_w_up, ffn_w_down, ab_w_in, ab_w_out, hy_conv_w, hy_conv_b, hy_f_w1, hy_f_b1, hy_f_freq, hy_f_w2, hy_f_b2, hy_f_w3, hy_bias, na_w_qkv, na_w_out, na_rpb, final_g):
    b, l, d = x.shape
    lc = ctx.shape[1]
    assert b == 2 and d == D_MODEL and l % 1024 == 0 and ada_w.shape[0] == 2

    cond8 = jnp.zeros((8, d), F32).at[0:b].set(c).at[b].set(c_ctx)
    mod = _ada(cond8, ada_w, ada_b)
    mod_lat = [mod[i, 0:b].reshape(b, 1, 6 * d) for i in range(2)]
    mod_ctx = [jnp.broadcast_to(mod[i, b:b + 1], (b, 6 * d)).reshape(b, 1, 6 * d) for i in range(2)]
    n1 = [norm1_g[i].reshape(1, d) for i in range(2)]
    n2 = [norm2_g[i].reshape(1, d) for i in range(2)]
    wg = [ffn_w_gate[i].astype(BF16) for i in range(2)]
    wu = [ffn_w_up[i].astype(BF16) for i in range(2)]
    wd = [ffn_w_down[i].astype(BF16) for i in range(2)]

    w_in_t = ab_w_in[0].T.astype(BF16)
    w_out = ab_w_out[0].astype(BF16)
    w1t = jnp.pad(hy_f_w1[0].T, ((0, 0), (0, LANES - HY_EMB)))
    col = lambda vec: vec.reshape(-1, 1)
    mlp = (w1t, col(hy_f_b1[0]), col(hy_f_freq[0]), hy_f_w2[0].T, col(hy_f_b2[0]),
           hy_f_w3[0].T.reshape(2, 2, HY_CH, HY_HID))

    zf, ph = _inproj_latent(x, n1[0], mod_lat[0], w_in_t)
    y_f = _fourier_latent(zf)
    hf = _hfilter_latent(l, *mlp)
    y_h = _hconv_latent(ph, hf, hy_conv_w[0], hy_conv_b[0], hy_bias[0]).reshape(b, HY_CH, l)
    h_lat = _mix_ffn(x, (y_f, y_h), w_out, mod_lat[0], n2[0], wg[0], wu[0], wd[0])

    zfc, phc = _inproj_ctx(ctx, n1[0], mod_ctx[0], w_in_t)
    y_fc = _fourier_ctx(zfc)
    hfc = _hfilter_ctx(lc, *mlp)
    y_hc = _hconv_ctx(phc, hfc, hy_conv_w[0], hy_conv_b[0], hy_bias[0])
    h_ctx = _mix_ffn(ctx, (y_fc, y_hc), w_out, mod_ctx[0], n2[0], wg[0], wu[0], wd[0])

    w_qkv = na_w_qkv[0].astype(BF16)
    q, k, v = _qkv(h_lat, n1[1], mod_lat[1], w_qkv)
    _, kc, vc = _qkv(h_ctx, n1[1], mod_ctx[1], w_qkv)
    attn = _na(q, k, v, kc, vc, _bias_tiles(na_rpb[0]))
    return _mix_ffn(h_lat, (attn,), na_w_out[0].astype(BF16), mod_lat[1], n2[1], wg[1], wu[1], wd[1],
                    final_g=final_g.reshape(1, d))
```

```python
import functools
import math

import numpy as np
import jax
import jax.numpy as jnp
from jax import lax
from jax.experimental import pallas as pl
from jax.experimental.pallas import tpu as pltpu

F32 = jnp.float32
BF16 = jnp.bfloat16

D_MODEL = 1024
FOURIER_CH = 512
FOURIER_GROUPS = 4
GROUP_DIM = 128
HY_CH = 512
HY_BANDS = 16
HY_EMB = 33
HY_HID = 64
HY_FAST_DECAY = 0.3
HY_SLOW_DECAY = 1.5
HY_DECAY_TARGET = 1e-2
D_FF = 2816
NA_HEADS = 16
NA_HEAD_DIM = 64
NA_KH = 8
NA_KW = 16
GRID_W = 64
RMS_EPS = 1e-6
NEG_BIG = -1e30

LANES = 128
VMEM_LIMIT_BYTES = 56 * 1024 * 1024

_NT = (((1,), (1,)), ((), ()))
_TN = (((0,), (0,)), ((), ()))


def _params(*sem):
    return pltpu.CompilerParams(dimension_semantics=sem, vmem_limit_bytes=VMEM_LIMIT_BYTES)


def _dot(a, b):
    return jnp.dot(a, b, preferred_element_type=F32)


def _dot_hi(a, b):
    return jnp.dot(a, b, preferred_element_type=F32, precision=lax.Precision.HIGHEST)


def _resident(shape):
    nd = len(shape)
    return pl.BlockSpec(shape, lambda *_: (0,) * nd, pipeline_mode=pl.Buffered(1))


def _rms(x, g):
    ms = jnp.mean(x * x, axis=-1, keepdims=True)
    return x * lax.rsqrt(ms + RMS_EPS) * g


def _norm_mod(x, g, shift, scale):
    return _rms(x, g) * (1.0 + scale) + shift


def _silu(x):
    return x * jax.nn.sigmoid(x)


def _cs(num, den):
    ang = 2.0 * np.pi * (np.asarray(num, np.float64) % den) / den
    return np.cos(ang), np.sin(ang)


@functools.lru_cache(maxsize=None)
def _const_channel_dft():
    c, s = _cs(np.outer(np.arange(GROUP_DIM), np.arange(GROUP_DIM)), GROUP_DIM)
    return (np.concatenate([c, -s], 0) / math.sqrt(GROUP_DIM)).astype(np.float32)


@functools.lru_cache(maxsize=None)
def _const_fourier(a_len):
    n = a_len * LANES
    c, s = _cs(np.outer(np.arange(a_len), np.arange(a_len)), a_len)
    w1 = np.block([[c, s], [-s, c]])
    tc, ts = _cs(np.outer(np.arange(a_len), np.arange(LANES)), n)
    tw = np.stack([tc, -ts])
    c2, s2 = _cs(np.outer(np.arange(LANES), np.arange(LANES)), LANES)
    w2 = np.concatenate([c2, s2], 0) / math.sqrt(n)
    return w1.astype(np.float32), tw.astype(np.float32), w2.astype(np.float32)


@functools.lru_cache(maxsize=None)
def _const_hyena(a_len):
    k1n = 2 * a_len
    n = k1n * LANES
    c, s = _cs(np.outer(np.arange(k1n), np.arange(a_len)), k1n)
    w1 = np.block([[c, s], [-s, c]])
    w1f = np.concatenate([c, -s], 0)
    tc, ts = _cs(np.outer(np.arange(k1n), np.arange(LANES)), n)
    tw = np.stack([tc, -ts])
    c2, s2 = _cs(np.outer(np.arange(LANES), np.arange(LANES)), LANES)
    w2 = np.block([[c2, -s2], [s2, c2]])
    w2s = np.concatenate([c2, s2], 0)
    w2d = np.concatenate([-s2, c2], 0)
    w2inv = np.block([[c2, s2], [-s2, c2]])
    v1 = np.concatenate([c.T, s.T], 0) / n
    f = lambda m: m.astype(np.float32)
    return f(w1), f(w1f), f(tw), f(w2), f(w2s), f(w2d), f(w2inv), f(v1)


@functools.lru_cache(maxsize=None)
def _const_dense(l_len):
    n = 2 * l_len
    cl, sl = _cs(np.outer(np.arange(l_len), np.arange(l_len)), l_len)
    four_c, four_s = cl / math.sqrt(l_len), sl / math.sqrt(l_len)
    cf, sf = _cs(np.outer(np.arange(l_len), np.arange(n)), n)
    fwd = np.block([[cf, -sf], [sf, cf]])
    inv = np.block([[cf.T, sf.T], [-sf.T, cf.T]]) / n
    f = lambda m: m.astype(np.float32)
    return f(four_c), f(four_s), f(cf), f(-sf), f(fwd), f(inv)


@functools.lru_cache(maxsize=None)
def _const_filter_features(l_len):
    t01 = np.linspace(0.0, 1.0, l_len)
    bands = np.linspace(1e-4, HY_BANDS - 1, HY_BANDS)
    ang = (2.0 * np.pi / l_len) * np.arange(l_len)[:, None] * bands[None, :]
    z = np.concatenate([t01[:, None], np.cos(ang), -np.sin(ang)], -1)
    zt = np.zeros((LANES, l_len), np.float64)
    zt[:HY_EMB] = z.T
    max_decay = math.log(HY_DECAY_TARGET) / HY_FAST_DECAY
    min_decay = math.log(HY_DECAY_TARGET) / HY_SLOW_DECAY
    deltas = np.abs(np.linspace(min_decay, max_decay, HY_CH))
    return zt.astype(np.float32), t01[None, :].astype(np.float32), deltas[:, None].astype(np.float32)


def _chunk_time(m):
    rows, l_len = m.shape
    return np.ascontiguousarray(m.reshape(rows, l_len // LANES, LANES).transpose(1, 0, 2))


def _ada_kernel(c_ref, w_ref, b_ref, o_ref):
    s = _silu(c_ref[...]).astype(BF16)
    o_ref[...] = _dot(s, w_ref[...].astype(BF16)) + b_ref[...]


def _ada(cond8, ada_w, ada_b):
    depth, d, n6 = ada_w.shape
    tn = 1536
    return pl.pallas_call(
        _ada_kernel,
        grid=(depth, n6 // tn),
        in_specs=[pl.BlockSpec((8, d), lambda i, j: (0, 0)),
                  pl.BlockSpec((None, d, tn), lambda i, j: (i, 0, j)),
                  pl.BlockSpec((None, 1, tn), lambda i, j: (i, 0, j))],
        out_specs=pl.BlockSpec((None, 8, tn), lambda i, j: (i, 0, j)),
        out_shape=jax.ShapeDtypeStruct((depth, 8, n6), F32),
        compiler_params=_params("parallel", "parallel"),
        name="ada_mod",
    )(cond8, ada_w, ada_b.reshape(depth, 1, n6))


def _inproj_kernel(x_ref, g_ref, mod_ref, wt_ref, cdft_ref, zf_ref, ph_ref):
    d = D_MODEL
    u = _norm_mod(x_ref[...], g_ref[...], mod_ref[:, 0:d], mod_ref[:, d:2 * d]).astype(BF16)
    cdft = cdft_ref[...].astype(BF16)
    for g in range(FOURIER_GROUPS):
        rows = slice(g * GROUP_DIM, (g + 1) * GROUP_DIM)
        pt = lax.dot_general(wt_ref[rows, :], u, _NT, preferred_element_type=F32)
        z = _dot(cdft, pt.astype(BF16))
        zf_ref[0, rows, :] = z[0:GROUP_DIM]
        zf_ref[1, rows, :] = z[GROUP_DIM:]
    rb = 512
    for r0 in range(FOURIER_CH, wt_ref.shape[0], rb):
        ph_ref[r0 - FOURIER_CH:r0 - FOURIER_CH + rb, :] = lax.dot_general(
            wt_ref[r0:r0 + rb, :], u, _NT, preferred_element_type=F32)


def _inproj(x, g, mod3, w_in_t, tl=1024):
    b, l, d = x.shape
    tl = min(tl, l)
    nh = w_in_t.shape[0] - FOURIER_CH
    return pl.pallas_call(
        _inproj_kernel,
        grid=(b, l // tl),
        in_specs=[pl.BlockSpec((None, tl, d), lambda i, t: (i, t, 0)),
                  pl.BlockSpec((1, d), lambda i, t: (0, 0)),
                  pl.BlockSpec((None, 1, 6 * d), lambda i, t: (i, 0, 0)),
                  _resident(w_in_t.shape),
                  _resident((2 * GROUP_DIM, GROUP_DIM))],
        out_specs=[pl.BlockSpec((None, 2, FOURIER_CH, tl), lambda i, t: (i, 0, 0, t)),
                   pl.BlockSpec((None, nh, tl), lambda i, t: (i, 0, t))],
        out_shape=[jax.ShapeDtypeStruct((b, 2, FOURIER_CH, l), F32),
                   jax.ShapeDtypeStruct((b, nh, l), F32)],
        compiler_params=_params("parallel", "parallel"),
        name="inproj",
    )(x, g, mod3, w_in_t, jnp.asarray(_const_channel_dft()))


def _twiddle_store(ys_ref, c, yr, yi, tr, ti):
    ys_ref[c, :, 0:LANES] = (yr * tr - yi * ti).astype(ys_ref.dtype)
    ys_ref[c, :, LANES:] = (yr * ti + yi * tr).astype(ys_ref.dtype)


def _fourier_kernel(z_ref, w1_ref, tw_ref, w2_ref, o_ref, ys_ref):
    _, ct, a_len, _ = z_ref.shape
    w1 = w1_ref[...].astype(BF16)
    tr, ti = tw_ref[0], tw_ref[1]

    def stage1(p, carry):
        c = 2 * p
        x = jnp.concatenate([jnp.concatenate([z_ref[0, c + h], z_ref[1, c + h]], 0) for h in range(2)], 1)
        y = _dot(w1, x.astype(BF16))
        for h in range(2):
            yh = y[:, h * LANES:(h + 1) * LANES]
            _twiddle_store(ys_ref, c + h, yh[0:a_len], yh[a_len:], tr, ti)
        return carry

    lax.fori_loop(0, ct // 2, stage1, 0, unroll=2)
    y = ys_ref[...].reshape(ct * a_len, 2 * LANES)
    zr = _dot(y, w2_ref[...].astype(BF16))
    for c in range(ct):
        o_ref[c] = zr[c * a_len:(c + 1) * a_len].T


def _fourier_latent(zf, ct=32):
    b, _, ch, l = zf.shape
    a_len = l // LANES
    zf = zf.reshape(b, 2, ch, a_len, LANES)
    w1, tw, w2 = (jnp.asarray(m) for m in _const_fourier(a_len))
    out = pl.pallas_call(
        _fourier_kernel,
        grid=(b, ch // ct),
        in_specs=[pl.BlockSpec((None, 2, ct, a_len, LANES), lambda i, c: (i, 0, c, 0, 0)),
                  _resident(w1.shape), _resident(tw.shape), _resident(w2.shape)],
        out_specs=pl.BlockSpec((None, ct, LANES, a_len), lambda i, c: (i, c, 0, 0)),
        out_shape=jax.ShapeDtypeStruct((b, ch, LANES, a_len), F32),
        scratch_shapes=[pltpu.VMEM((ct, a_len, 2 * LANES), BF16)],
        compiler_params=_params("parallel", "parallel"),
        name="fourier_latent",
    )(zf, w1, tw, w2)
    return out.reshape(b, ch, LANES * a_len)


def _fourier_ctx_kernel(z_ref, c_ref, s_ref, o_ref):
    zr = z_ref[0].astype(BF16)
    zi = z_ref[1].astype(BF16)
    o_ref[...] = _dot(zr, c_ref[...].astype(BF16)) + _dot(zi, s_ref[...].astype(BF16))


def _fourier_ctx(zf):
    b, _, ch, l = zf.shape
    four_c, four_s = (jnp.asarray(m) for m in _const_dense(l)[:2])
    return pl.pallas_call(
        _fourier_ctx_kernel,
        grid=(b,),
        in_specs=[pl.BlockSpec((None, 2, ch, l), lambda i: (i, 0, 0, 0)),
                  _resident((l, l)), _resident((l, l))],
        out_specs=pl.BlockSpec((None, ch, l), lambda i: (i, 0, 0)),
        out_shape=jax.ShapeDtypeStruct((b, ch, l), F32),
        compiler_params=_params("parallel"),
        name="fourier_ctx",
    )(zf, four_c, four_s)


def _filter_mlp(zt, w1t, b1, freq, w2t, b2):
    h = jnp.sin(freq * (_dot_hi(w1t, zt) + b1))
    return jnp.sin(freq * (_dot_hi(w2t, h) + b2))


def _hfilter_kernel(zt_ref, t_ref, dl_ref, w1t_ref, b1_ref, fr_ref, w2t_ref, b2_ref, w3t_ref,
                    w1f_ref, tw_ref, w2s_ref, w2d_ref, o_ref, h2_ref, hf_ref, xs_ref, ys_ref):
    cf = o_ref.shape[0]
    a_len = xs_ref.shape[1]
    k1n = 2 * a_len

    @pl.when((pl.program_id(0) == 0) & (pl.program_id(1) == 0))
    def _():
        def mlp(a, carry):
            h2_ref[a] = _filter_mlp(zt_ref[a], w1t_ref[...], b1_ref[...], fr_ref[...], w2t_ref[...],
                                    b2_ref[...]).astype(h2_ref.dtype)
            return carry
        lax.fori_loop(0, a_len, mlp, 0, unroll=2)

    w3 = w3t_ref[...].reshape(2 * cf, HY_HID).astype(BF16)
    dl = dl_ref[...]
    dl2 = jnp.concatenate([dl, dl], 0)

    def taps(a, acc):
        hv = _dot(w3, h2_ref[a]) * jnp.exp(-(dl2 * t_ref[a]))
        hf_ref[a] = hv
        return acc + jnp.abs(hv)

    acc = lax.fori_loop(0, a_len, taps, jnp.zeros((2 * cf, LANES), F32), unroll=4)
    tot = jnp.sum(acc, axis=1, keepdims=True)
    inv = 1.0 / (tot[0:cf] + tot[cf:])
    lane = lax.broadcasted_iota(jnp.int32, (cf, LANES), 1)

    def combine(a, carry):
        f = hf_ref[a, 0:cf, :] * inv
        bk = hf_ref[a, cf:, :] * inv
        bk = jnp.where((lane == 0) & (a == 0), 0.0, bk)
        hf_ref[a, 0:cf, :] = f + bk
        hf_ref[a, cf:, :] = f - bk
        return carry

    lax.fori_loop(0, a_len, combine, 0, unroll=4)
    xs_ref[...] = jnp.swapaxes(hf_ref[...], 0, 1).astype(xs_ref.dtype)

    w1f = w1f_ref[...].astype(BF16)
    tr, ti = tw_ref[0], tw_ref[1]

    def stage1(p, carry):
        c = 2 * p
        y = _dot(w1f, jnp.concatenate([xs_ref[c], xs_ref[c + 1]], 1))
        for h in range(2):
            yh = y[:, h * LANES:(h + 1) * LANES]
            _twiddle_store(ys_ref, c + h, yh[0:k1n], yh[k1n:], tr, ti)
        return carry

    lax.fori_loop(0, cf, stage1, 0, unroll=2)
    ysum = ys_ref[0:cf].reshape(cf * k1n, 2 * LANES)
    ydif = ys_ref[cf:].reshape(cf * k1n, 2 * LANES)
    o_ref[:, :, 0:LANES] = _dot(ysum, w2s_ref[...].astype(BF16)).reshape(cf, k1n, LANES)
    o_ref[:, :, LANES:] = _dot(ydif, w2d_ref[...].astype(BF16)).reshape(cf, k1n, LANES)


def _hfilter_latent(l, w1t, b1, freq, w2t, b2, w3t, cf=32):
    a_len = l // LANES
    k1n = 2 * a_len
    zt, t01, deltas = _const_filter_features(l)
    zt, t01, deltas = jnp.asarray(_chunk_time(zt)), jnp.asarray(_chunk_time(t01)), jnp.asarray(deltas)
    _, w1f, tw, _, w2s, w2d, _, _ = (jnp.asarray(m) for m in _const_hyena(a_len))
    orders = w3t.shape[1]
    return pl.pallas_call(
        _hfilter_kernel,
        grid=(orders, HY_CH // cf),
        in_specs=[_resident(zt.shape), _resident(t01.shape),
                  pl.BlockSpec((cf, 1), lambda o, c: (c, 0)),
                  _resident(w1t.shape), _resident(b1.shape), _resident(freq.shape),
                  _resident(w2t.shape), _resident(b2.shape),
                  pl.BlockSpec((2, None, cf, HY_HID), lambda o, c: (0, o, c, 0)),
                  _resident(w1f.shape), _resident(tw.shape), _resident(w2s.shape), _resident(w2d.shape)],
        out_specs=pl.BlockSpec((None, cf, k1n, 2 * LANES), lambda o, c: (o, c, 0, 0)),
        out_shape=jax.ShapeDtypeStruct((orders, HY_CH, k1n, 2 * LANES), F32),
        scratch_shapes=[pltpu.VMEM((a_len, HY_HID, LANES), BF16),
                        pltpu.VMEM((a_len, 2 * cf, LANES), F32),
                        pltpu.VMEM((2 * cf, a_len, LANES), BF16),
                        pltpu.VMEM((2 * cf, k1n, 2 * LANES), BF16)],
        compiler_params=_params("arbitrary", "arbitrary"),
        name="hyena_filter_latent",
    )(zt, t01, deltas, w1t, b1, freq, w2t, b2, w3t, w1f, tw, w2s, w2d)


def _hfilter_ctx_kernel(zt_ref, t_ref, dl_ref, w1t_ref, b1_ref, fr_ref, w2t_ref, b2_ref, w3t_ref,
                        cf_ref, sf_ref, o_ref):
    l = zt_ref.shape[1]
    h2 = _filter_mlp(zt_ref[...], w1t_ref[...], b1_ref[...], fr_ref[...], w2t_ref[...], b2_ref[...])
    ch = w3t_ref.shape[1]
    w3 = w3t_ref[...].reshape(2 * ch, HY_HID)
    dl = dl_ref[...]
    hv = _dot_hi(w3, h2) * jnp.exp(-(jnp.concatenate([dl, dl], 0) * t_ref[...]))
    tot = jnp.sum(jnp.abs(hv), axis=1, keepdims=True)
    inv = 1.0 / (tot[0:ch] + tot[ch:])
    f = hv[0:ch] * inv
    lane = lax.broadcasted_iota(jnp.int32, (ch, l), 1)
    bk = jnp.where(lane == 0, 0.0, hv[ch:] * inv)
    o_ref[:, 0:2 * l] = _dot((f + bk).astype(BF16), cf_ref[...].astype(BF16))
    o_ref[:, 2 * l:] = _dot((f - bk).astype(BF16), sf_ref[...].astype(BF16))


def _hfilter_ctx(l, w1t, b1, freq, w2t, b2, w3t):
    zt, t01, deltas = (jnp.asarray(m) for m in _const_filter_features(l))
    cfm, nsf = (jnp.asarray(m) for m in _const_dense(l)[2:4])
    orders = w3t.shape[1]
    return pl.pallas_call(
        _hfilter_ctx_kernel,
        grid=(orders,),
        in_specs=[_resident(zt.shape), _resident(t01.shape), _resident(deltas.shape),
                  _resident(w1t.shape), _resident(b1.shape), _resident(freq.shape),
                  _resident(w2t.shape), _resident(b2.shape),
                  pl.BlockSpec((2, None, HY_CH, HY_HID), lambda o: (0, o, 0, 0)),
                  _resident(cfm.shape), _resident(nsf.shape)],
        out_specs=pl.BlockSpec((None, HY_CH, 4 * l), lambda o: (o, 0, 0)),
        out_shape=jax.ShapeDtypeStruct((orders, HY_CH, 4 * l), F32),
        compiler_params=_params("parallel"),
        name="hyena_filter_ctx",
    )(zt, t01, deltas, w1t, b1, freq, w2t, b2, w3t, cfm, nsf)


def _short_conv_tile(x, w0, w1, w2, bias):
    a_len = x.shape[0]
    lane = lax.broadcasted_iota(jnp.int32, x.shape, 1)
    row = lax.broadcasted_iota(jnp.int32, x.shape, 0)
    xl = pltpu.roll(x, 1, 1)
    prev = jnp.where(lane == 0, jnp.where(row == 0, 0.0, pltpu.roll(xl, 1, 0)), xl)
    xr = pltpu.roll(x, LANES - 1, 1)
    nxt = jnp.where(lane == LANES - 1, jnp.where(row == a_len - 1, 0.0, pltpu.roll(xr, a_len - 1, 0)), xr)
    return w0 * prev + w1 * x + w2 * nxt + bias


def _hconv_kernel(cw_ref, cb_ref, sk_ref, x1_ref, x2_ref, v_ref, hf_ref,
                  w1_ref, tw_ref, w2_ref, w2i_ref, v1_ref, o_ref, vb_ref, ys_ref, us_ref, *, group):
    nb, ct, a_len, _ = v_ref.shape
    k1n = 2 * a_len
    nch3 = cb_ref.shape[0]
    c0 = pl.program_id(0) * ct
    w1 = w1_ref[...].astype(BF16)
    w2 = w2_ref[...].astype(BF16)
    w2i = w2i_ref[...].astype(BF16)
    v1 = v1_ref[...].astype(BF16)
    tr, ti = tw_ref[0], tw_ref[1]

    def sconv(ref, bi, c, ch):
        return _short_conv_tile(ref[bi, c], cw_ref[ch], cw_ref[nch3 + ch], cw_ref[2 * nch3 + ch], cb_ref[ch])

    def prep(c, carry):
        for bi in range(nb):
            vb_ref[bi, c] = sconv(v_ref, bi, c, 2 * HY_CH + c0 + c)
        return carry

    lax.fori_loop(0, ct, prep, 0, unroll=2)

    for order, gate_ref in enumerate((x1_ref, x2_ref)):
        def stage1(p, carry):
            c = 2 * p
            x = jnp.concatenate([jnp.concatenate([vb_ref[0, c + h], vb_ref[1, c + h]], 0) for h in range(2)], 1)
            y = _dot(w1, x.astype(BF16))
            for h in range(2):
                yh = y[:, h * LANES:(h + 1) * LANES]
                _twiddle_store(ys_ref, c + h, yh[0:k1n], yh[k1n:], tr, ti)
            return carry

        lax.fori_loop(0, ct // 2, stage1, 0, unroll=2)

        def middle(g, carry):
            cs = pl.ds(g * group, group)
            z = _dot(ys_ref[cs].reshape(group * k1n, 2 * LANES), w2)
            h = hf_ref[order, cs].reshape(group * k1n, 2 * LANES)
            zr, zi = z[:, 0:LANES], z[:, LANES:]
            hr, hi = h[:, 0:LANES], h[:, LANES:]
            p = jnp.concatenate([zr * hr - zi * hi, zr * hi + zi * hr], 1).astype(BF16)
            u = _dot(p, w2i).reshape(group, k1n, 2 * LANES)
            ur, ui = u[:, :, 0:LANES], u[:, :, LANES:]
            us_ref[cs, :, 0:LANES] = (ur * tr + ui * ti).astype(BF16)
            us_ref[cs, :, LANES:] = (ui * tr - ur * ti).astype(BF16)
            return carry

        lax.fori_loop(0, ct // group, middle, 0)

        def stage1_inv(c, carry):
            q = _dot(v1, us_ref[c])
            conv = (q[0:a_len, 0:LANES] - q[a_len:, LANES:], q[a_len:, 0:LANES] + q[0:a_len, LANES:])
            skip = sk_ref[order * HY_CH + c0 + c]
            for bi in range(nb):
                gate = sconv(gate_ref, bi, c, order * HY_CH + c0 + c)
                res = gate * (conv[bi] + vb_ref[bi, c] * skip)
                if order == 0:
                    vb_ref[bi, c] = res
                else:
                    o_ref[bi, c] = res
            return carry

        lax.fori_loop(0, ct, stage1_inv, 0, unroll=2)


def _hconv_latent(ph, hf, conv_w, conv_b, skip, ct=16, group=4):
    b, nch3, l = ph.shape
    assert b == 2, "the two batch entries ride as real/imaginary parts"
    a_len = l // LANES
    ph = ph.reshape(b, nch3, a_len, LANES)
    k1n = 2 * a_len
    w1, _, tw, w2, _, _, w2i, v1 = (jnp.asarray(m) for m in _const_hyena(a_len))
    nblk = HY_CH // ct
    smem = pl.BlockSpec(memory_space=pltpu.SMEM)
    blk = lambda off: pl.BlockSpec((b, ct, a_len, LANES), lambda c: (0, c + off * nblk, 0, 0))
    return pl.pallas_call(
        functools.partial(_hconv_kernel, group=group),
        grid=(nblk,),
        in_specs=[smem, smem, smem, blk(0), blk(1), blk(2),
                  pl.BlockSpec((2, ct, k1n, 2 * LANES), lambda c: (0, c, 0, 0)),
                  _resident(w1.shape), _resident(tw.shape), _resident(w2.shape),
                  _resident(w2i.shape), _resident(v1.shape)],
        out_specs=pl.BlockSpec((b, ct, a_len, LANES), lambda c: (0, c, 0, 0)),
        out_shape=jax.ShapeDtypeStruct((b, HY_CH, a_len, LANES), F32),
        scratch_shapes=[pltpu.VMEM((b, ct, a_len, LANES), F32),
                        pltpu.VMEM((ct, k1n, 2 * LANES), BF16),
                        pltpu.VMEM((ct, k1n, 2 * LANES), BF16)],
        compiler_params=_params("parallel"),
        name="hyena_conv_latent",
    )(conv_w.reshape(-1), conv_b, skip.reshape(-1), ph, ph, ph, hf, w1, tw, w2, w2i, v1)


def _hconv_ctx_kernel(cw_ref, cb_ref, sk_ref, p_ref, hf_ref, fwd_ref, inv_ref, o_ref):
    nb, _, l = p_ref.shape
    ch = HY_CH
    lane = lax.broadcasted_iota(jnp.int32, (3 * ch, l), 1)
    cw = cw_ref[...]
    z = []
    for bi in range(nb):
        x = p_ref[bi]
        prev = jnp.where(lane == 0, 0.0, pltpu.roll(x, 1, 1))
        nxt = jnp.where(lane == l - 1, 0.0, pltpu.roll(x, l - 1, 1))
        z.append(cw[:, 0:1] * prev + cw[:, 1:2] * x + cw[:, 2:3] * nxt + cb_ref[...])
    fwd = fwd_ref[...].astype(BF16)
    inv = inv_ref[...].astype(BF16)
    cur = [zz[2 * ch:] for zz in z]
    for order in range(2):
        spec = _dot(jnp.concatenate(cur, 1).astype(BF16), fwd)
        h = hf_ref[order]
        zr, zi, hr, hi = spec[:, 0:2 * l], spec[:, 2 * l:], h[:, 0:2 * l], h[:, 2 * l:]
        prod = jnp.concatenate([zr * hr - zi * hi, zr * hi + zi * hr], 1).astype(BF16)
        y = _dot(prod, inv)
        sk = sk_ref[:, order:order + 1]
        cur = [z[bi][order * ch:(order + 1) * ch] * (y[:, bi * l:(bi + 1) * l] + cur[bi] * sk) for bi in range(nb)]
    for bi in range(nb):
        o_ref[bi] = cur[bi]


def _hconv_ctx(ph, hf, conv_w, conv_b, skip):
    b, nch3, l = ph.shape
    assert b == 2
    fwd, inv = (jnp.asarray(m) for m in _const_dense(l)[4:6])
    cw = jnp.pad(conv_w.T, ((0, 0), (0, 5)))
    return pl.pallas_call(
        _hconv_ctx_kernel,
        in_specs=[pl.BlockSpec(cw.shape, lambda: (0, 0)),
                  pl.BlockSpec((nch3, 1), lambda: (0, 0)),
                  pl.BlockSpec((HY_CH, 2), lambda: (0, 0)),
                  pl.BlockSpec(ph.shape, lambda: (0, 0, 0)),
                  pl.BlockSpec(hf.shape, lambda: (0, 0, 0)),
                  pl.BlockSpec(fwd.shape, lambda: (0, 0)),
                  pl.BlockSpec(inv.shape, lambda: (0, 0))],
        out_specs=pl.BlockSpec((b, HY_CH, l), lambda: (0, 0, 0)),
        out_shape=jax.ShapeDtypeStruct((b, HY_CH, l), F32),
        compiler_params=pltpu.CompilerParams(vmem_limit_bytes=VMEM_LIMIT_BYTES),
        name="hyena_conv_ctx",
    )(cw, conv_b.reshape(-1, 1), skip.T, ph, hf, fwd, inv)


def _mix_ffn_kernel(*refs, channel_major, final, ff_chunk):
    d = D_MODEL
    if channel_major:
        h_ref, yf_ref, yh_ref, wo_ref, mod_ref, g2_ref, wg_ref, wu_ref, wd_ref = refs[:9]
        rest = refs[9:]
        y = (lax.dot_general(yf_ref[...].astype(BF16), wo_ref[0:FOURIER_CH, :], _TN, preferred_element_type=F32)
             + lax.dot_general(yh_ref[...].astype(BF16), wo_ref[FOURIER_CH:, :], _TN, preferred_element_type=F32))
    else:
        h_ref, ya_ref, wo_ref, mod_ref, g2_ref, wg_ref, wu_ref, wd_ref = refs[:8]
        rest = refs[8:]
        y = _dot(ya_ref[...], wo_ref[...])
    if final:
        fg_ref, o_ref = rest
    else:
        (o_ref,) = rest
    h1 = h_ref[...] + mod_ref[:, 2 * d:3 * d] * y
    u = _norm_mod(h1, g2_ref[...], mod_ref[:, 3 * d:4 * d], mod_ref[:, 4 * d:5 * d]).astype(BF16)
    acc = jnp.zeros(h1.shape, F32)
    for j in range(D_FF // ff_chunk):
        cols = slice(j * ff_chunk, (j + 1) * ff_chunk)
        act = (_silu(_dot(u, wg_ref[:, cols])) * _dot(u, wu_ref[:, cols])).astype(BF16)
        acc = acc + _dot(act, wd_ref[cols, :])
    h2 = h1 + mod_ref[:, 5 * d:6 * d] * acc
    if final:
        h2 = _rms(h2, fg_ref[...])
    o_ref[...] = h2


def _mix_ffn(h, ys, wo, mod3, g2, wg, wu, wd, final_g=None, tl=512, ff_chunk=256):
    b, l, d = h.shape
    tl = min(tl, l)
    channel_major = len(ys) == 2
    tok = pl.BlockSpec((None, tl, d), lambda i, t: (i, t, 0))
    if channel_major:
        y_specs = [pl.BlockSpec((None, y.shape[1], tl), lambda i, t: (i, 0, t)) for y in ys]
    else:
        y_specs = [tok]
    in_specs = ([tok] + y_specs + [_resident(wo.shape), pl.BlockSpec((None, 1, 6 * d), lambda i, t: (i, 0, 0)),
                                    pl.BlockSpec((1, d), lambda i, t: (0, 0)),
                                    _resident(wg.shape), _resident(wu.shape), _resident(wd.shape)])
    args = [h, *ys, wo, mod3, g2, wg, wu, wd]
    if final_g is not None:
        in_specs.append(pl.BlockSpec((1, d), lambda i, t: (0, 0)))
        args.append(final_g)
    return pl.pallas_call(
        functools.partial(_mix_ffn_kernel, channel_major=channel_major, final=final_g is not None,
                          ff_chunk=ff_chunk),
        grid=(b, l // tl),
        in_specs=in_specs,
        out_specs=tok,
        out_shape=jax.ShapeDtypeStruct((b, l, d), F32),
        compiler_params=_params("parallel", "parallel"),
        name="mix_ffn_final" if final_g is not None else "mix_ffn",
    )(*args)


def _qkv_kernel(x_ref, g_ref, mod_ref, w_ref, q_ref, k_ref, v_ref):
    d = D_MODEL
    u = _norm_mod(x_ref[...], g_ref[...], mod_ref[:, 0:d], mod_ref[:, d:2 * d]).astype(BF16)
    q_ref[...] = (_dot(u, w_ref[:, 0:d]) * (NA_HEAD_DIM ** -0.5)).astype(BF16)
    k_ref[...] = _dot(u, w_ref[:, d:2 * d]).astype(BF16)
    v_ref[...] = _dot(u, w_ref[:, 2 * d:]).astype(BF16)


def _qkv(h, g, mod3, w, tl=1024):
    b, l, d = h.shape
    tl = min(tl, l)
    tok = pl.BlockSpec((None, tl, d), lambda i, t: (i, t, 0))
    return pl.pallas_call(
        _qkv_kernel,
        grid=(b, l // tl),
        in_specs=[tok, pl.BlockSpec((1, d), lambda i, t: (0, 0)),
                  pl.BlockSpec((None, 1, 6 * d), lambda i, t: (i, 0, 0)), _resident(w.shape)],
        out_specs=[tok, tok, tok],
        out_shape=[jax.ShapeDtypeStruct((b, l, d), BF16)] * 3,
        compiler_params=_params("parallel", "parallel"),
        name="qkv_proj",
    )(h, g, mod3, w)


NA_QROWS = 4
NA_KROWS = 12
NA_PATTERNS = 3


def _na_key_start(r0, grid_rows):
    return jnp.minimum(jnp.clip(r0 - NA_KH // 2, 0, grid_rows - NA_KH), grid_rows - NA_KROWS)


def _na_pattern(pat, i, j):
    if pat == 0:
        return j - i + NA_KH - 1, j < NA_KH
    if pat == 1:
        return j - i + NA_KH // 2 - 1, i <= j < i + NA_KH
    return j - i - 1, j >= NA_KROWS - NA_KH


def _bias_blocks_kernel(rpb_ref, o_ref):
    pair = pl.program_id(0)
    ndr, ndc = 2 * NA_KH - 1, 2 * NA_KW - 1
    q = lax.broadcasted_iota(jnp.int32, (GRID_W, LANES), 0)
    lane = lax.broadcasted_iota(jnp.int32, (GRID_W, LANES), 1)
    kcol = lane % GRID_W
    upper = lane >= GRID_W
    wstart = jnp.clip(q - NA_KW // 2, 0, GRID_W - NA_KW)
    ok = (kcol >= wstart) & (kcol < wstart + NA_KW)
    dc = kcol - q + (NA_KW - 1)
    neg = jnp.full((GRID_W, LANES), NEG_BIG, F32)
    for e in range(2):
        base = (2 * pair + e) * ndr * ndc
        tiles = []
        for dr in range(ndr):
            acc = jnp.zeros((GRID_W, LANES), F32)
            for dd in range(ndc):
                acc = jnp.where(dc == dd, rpb_ref[base + dr * ndc + dd], acc)
            tiles.append(jnp.where(ok, acc, NEG_BIG))
        for pat in range(NA_PATTERNS):
            for i in range(NA_QROWS):
                for p in range(NA_KROWS // 2):
                    halves = []
                    for j in (2 * p, 2 * p + 1):
                        dr, visible = _na_pattern(pat, i, j)
                        halves.append(tiles[dr] if visible else neg)
                    o_ref[e, pat, i * GRID_W:(i + 1) * GRID_W, p * LANES:(p + 1) * LANES] = jnp.where(
                        upper, halves[1], halves[0])


def _bias_blocks(rpb):
    heads = rpb.shape[0]
    nq, nk = NA_QROWS * GRID_W, NA_KROWS * GRID_W
    return pl.pallas_call(
        _bias_blocks_kernel,
        grid=(heads // 2,),
        in_specs=[pl.BlockSpec(memory_space=pltpu.SMEM)],
        out_specs=pl.BlockSpec((None, 2, NA_PATTERNS, nq, nk), lambda p: (p, 0, 0, 0, 0)),
        out_shape=jax.ShapeDtypeStruct((heads // 2, 2, NA_PATTERNS, nq, nk), F32),
        compiler_params=_params("parallel"),
        name="na_bias_blocks",
    )(rpb.reshape(-1))


def _na_kernel(q_ref, k_ref, v_ref, kc_ref, vc_ref, bias_ref, o_ref, *, blocks_per_step, grid_rows):
    nq, nk = NA_QROWS * GRID_W, NA_KROWS * GRID_W
    lane = lax.broadcasted_iota(jnp.int32, (nq, LANES), 1)
    kc = kc_ref[...]
    vc = vc_ref[...]
    step = pl.program_id(2)

    def one_block(i, carry):
        r0 = (step * blocks_per_step + i) * NA_QROWS
        pat = jnp.where(r0 == 0, 0, jnp.where(r0 == grid_rows - NA_QROWS, 2, 1))
        qoff = pl.multiple_of(i * nq, nq)
        koff = pl.multiple_of(_na_key_start(r0, grid_rows) * GRID_W, GRID_W)
        q = q_ref[pl.ds(qoff, nq), :]
        ku = k_ref[pl.ds(koff, nk), :]
        vu = v_ref[pl.ds(koff, nk), :]
        outs = []
        for e in range(2):
            head = (lane >= NA_HEAD_DIM) if e else (lane < NA_HEAD_DIM)
            qh = jnp.where(head, q, jnp.zeros_like(q))
            sl = lax.dot_general(qh, ku, _NT, preferred_element_type=F32) + bias_ref[e, pat]
            sc = lax.dot_general(qh, kc, _NT, preferred_element_type=F32)
            m = jnp.maximum(jnp.max(sl, axis=1, keepdims=True), jnp.max(sc, axis=1, keepdims=True))
            pl_, pc = jnp.exp(sl - m), jnp.exp(sc - m)
            den = jnp.sum(pl_, axis=1, keepdims=True) + jnp.sum(pc, axis=1, keepdims=True)
            o = _dot(pl_.astype(BF16), vu) + _dot(pc.astype(BF16), vc)
            outs.append(o / den)
        o_ref[pl.ds(qoff, nq), :] = jnp.where(lane < NA_HEAD_DIM, outs[0], outs[1]).astype(o_ref.dtype)
        return carry

    lax.fori_loop(0, blocks_per_step, one_block, 0)


def _na(q, k, v, kc, vc, bias_blocks, blocks_per_step=4):
    b, l, d = q.shape
    lc = kc.shape[1]
    grid_rows = l // GRID_W
    assert grid_rows % (NA_QROWS * blocks_per_step) == 0 and grid_rows >= NA_KROWS
    npairs = d // LANES
    tq = blocks_per_step * NA_QROWS * GRID_W
    return pl.pallas_call(
        functools.partial(_na_kernel, blocks_per_step=blocks_per_step, grid_rows=grid_rows),
        grid=(npairs, b, l // tq),
        in_specs=[pl.BlockSpec((None, tq, LANES), lambda p, i, t: (i, t, p)),
                  pl.BlockSpec((None, l, LANES), lambda p, i, t: (i, 0, p)),
                  pl.BlockSpec((None, l, LANES), lambda p, i, t: (i, 0, p)),
                  pl.BlockSpec((None, lc, LANES), lambda p, i, t: (i, 0, p)),
                  pl.BlockSpec((None, lc, LANES), lambda p, i, t: (i, 0, p)),
                  pl.BlockSpec((None,) + bias_blocks.shape[1:], lambda p, i, t: (p, 0, 0, 0, 0))],
        out_specs=pl.BlockSpec((None, tq, LANES), lambda p, i, t: (i, t, p)),
        out_shape=jax.ShapeDtypeStruct((b, l, d), BF16),
        compiler_params=_params("parallel", "parallel", "parallel"),
        name="neighbourhood_attention",
    )(q, k, v, kc, vc, bias_blocks)


def kernel(x, c, ctx, c_ctx, ada_w, ada_b, norm1_g, norm2_g, ffn_w_gate, ffn_w_up, ffn_w_down, ab_w_in, ab_w_out, hy_conv_w, hy_conv_b, hy_f_w1, hy_f_b1, hy_f_freq, hy_f_w2, hy_f_b2, hy_f_w3, hy_bias, na_w_qkv, na_w_out, na_rpb, final_g):
    b, l, d = x.shape
    lc = ctx.shape[1]
    assert b == 2 and d == D_MODEL and l % 1024 == 0 and ada_w.shape[0] == 2

    cond8 = jnp.zeros((8, d), F32).at[0:b].set(c).at[b].set(c_ctx)
    mod = _ada(cond8, ada_w, ada_b)
    mod_lat = [mod[i, 0:b].reshape(b, 1, 6 * d) for i in range(2)]
    mod_ctx = [jnp.broadcast_to(mod[i, b:b + 1], (b, 6 * d)).reshape(b, 1, 6 * d) for i in range(2)]
    n1 = [norm1_g[i].reshape(1, d) for i in range(2)]
    n2 = [norm2_g[i].reshape(1, d) for i in range(2)]
    wg = [ffn_w_gate[i].astype(BF16) for i in range(2)]
    wu = [ffn_w_up[i].astype(BF16) for i in range(2)]
    wd = [ffn_w_down[i].astype(BF16) for i in range(2)]

    w_in_t = ab_w_in[0].T.astype(BF16)
    w_out = ab_w_out[0].astype(BF16)
    w1t = jnp.pad(hy_f_w1[0].T, ((0, 0), (0, LANES - HY_EMB)))
    col = lambda vec: vec.reshape(-1, 1)
    mlp = (w1t, col(hy_f_b1[0]), col(hy_f_freq[0]), hy_f_w2[0].T, col(hy_f_b2[0]),
           hy_f_w3[0].T.reshape(2, 2, HY_CH, HY_HID))

    zf, ph = _inproj(x, n1[0], mod_lat[0], w_in_t)
    y_f = _fourier_latent(zf)
    hf = _hfilter_latent(l, *mlp)
    y_h = _hconv_latent(ph, hf, hy_conv_w[0], hy_conv_b[0], hy_bias[0]).reshape(b, HY_CH, l)
    h_lat = _mix_ffn(x, (y_f, y_h), w_out, mod_lat[0], n2[0], wg[0], wu[0], wd[0])

    zfc, phc = _inproj(ctx, n1[0], mod_ctx[0], w_in_t)
    y_fc = _fourier_ctx(zfc)
    hfc = _hfilter_ctx(lc, *mlp)
    y_hc = _hconv_ctx(phc, hfc, hy_conv_w[0], hy_conv_b[0], hy_bias[0])
    h_ctx = _mix_ffn(ctx, (y_fc, y_hc), w_out, mod_ctx[0], n2[0], wg[0], wu[0], wd[0])

    w_qkv = na_w_qkv[0].astype(BF16)
    q, k, v = _qkv(h_lat, n1[1], mod_lat[1], w_qkv)
    _, kc, vc = _qkv(h_ctx, n1[1], mod_ctx[1], w_qkv)
    attn = _na(q, k, v, kc, vc, _bias_blocks(na_rpb[0]))
    return _mix_ffn(h_lat, (attn,), na_w_out[0].astype(BF16), mod_lat[1], n2[1], wg[1], wu[1], wd[1],
                    final_g=final_g.reshape(1, d))
```

```python
import functools
import math

import numpy as np
import jax
import jax.numpy as jnp
from jax import lax
from jax.experimental import pallas as pl
from jax.experimental.pallas import tpu as pltpu

F32 = jnp.float32
BF16 = jnp.bfloat16

D_MODEL = 1024
FOURIER_CH = 512
FOURIER_GROUPS = 4
GROUP_DIM = 128
HY_CH = 512
HY_BANDS = 16
HY_EMB = 33
HY_HID = 64
HY_FAST_DECAY = 0.3
HY_SLOW_DECAY = 1.5
HY_DECAY_TARGET = 1e-2
D_FF = 2816
NA_HEADS = 16
NA_HEAD_DIM = 64
NA_KH = 8
NA_KW = 16
GRID_W = 64
RMS_EPS = 1e-6
NEG_BIG = -1e30

LANES = 128
VMEM_LIMIT_BYTES = 56 * 1024 * 1024

_NT = (((1,), (1,)), ((), ()))
_TN = (((0,), (0,)), ((), ()))


def _params(*sem):
    return pltpu.CompilerParams(dimension_semantics=sem, vmem_limit_bytes=VMEM_LIMIT_BYTES)


def _dot(a, b):
    return jnp.dot(a, b, preferred_element_type=F32)


def _dot_hi(a, b):
    return jnp.dot(a, b, preferred_element_type=F32, precision=lax.Precision.HIGHEST)


def _resident(shape):
    nd = len(shape)
    return pl.BlockSpec(shape, lambda *_: (0,) * nd, pipeline_mode=pl.Buffered(1))


def _rms(x, g):
    ms = jnp.mean(x * x, axis=-1, keepdims=True)
    return x * lax.rsqrt(ms + RMS_EPS) * g


def _norm_mod(x, g, shift, scale):
    return _rms(x, g) * (1.0 + scale) + shift


def _silu(x):
    return x * jax.nn.sigmoid(x)


def _cs(num, den):
    ang = 2.0 * np.pi * (np.asarray(num, np.float64) % den) / den
    return np.cos(ang), np.sin(ang)


@functools.lru_cache(maxsize=None)
def _const_channel_dft():
    c, s = _cs(np.outer(np.arange(GROUP_DIM), np.arange(GROUP_DIM)), GROUP_DIM)
    return (np.concatenate([c, -s], 0) / math.sqrt(GROUP_DIM)).astype(np.float32)


@functools.lru_cache(maxsize=None)
def _const_fourier(a_len):
    n = a_len * LANES
    c, s = _cs(np.outer(np.arange(a_len), np.arange(a_len)), a_len)
    w1 = np.block([[c, s], [-s, c]])
    tc, ts = _cs(np.outer(np.arange(a_len), np.arange(LANES)), n)
    tw = np.stack([tc, -ts])
    c2, s2 = _cs(np.outer(np.arange(LANES), np.arange(LANES)), LANES)
    w2 = np.concatenate([c2, s2], 0) / math.sqrt(n)
    r = LANES // a_len
    perm = np.array([r * m + h for h in range(r) for m in range(a_len)])
    w2 = w2[:, perm]
    return w1.astype(np.float32), tw.astype(np.float32), w2.astype(np.float32)


@functools.lru_cache(maxsize=None)
def _const_hyena(a_len):
    k1n = 2 * a_len
    n = k1n * LANES
    c, s = _cs(np.outer(np.arange(k1n), np.arange(a_len)), k1n)
    w1 = np.block([[c, s], [-s, c]])
    w1f = np.concatenate([c, -s], 0)
    tc, ts = _cs(np.outer(np.arange(k1n), np.arange(LANES)), n)
    tw = np.stack([tc, -ts])
    c2, s2 = _cs(np.outer(np.arange(LANES), np.arange(LANES)), LANES)
    w2 = np.block([[c2, -s2], [s2, c2]])
    w2s = np.concatenate([c2, s2], 0)
    w2d = np.concatenate([-s2, c2], 0)
    w2inv = np.block([[c2, s2], [-s2, c2]])
    v1 = np.concatenate([c.T, s.T], 0) / n
    f = lambda m: m.astype(np.float32)
    return f(w1), f(w1f), f(tw), f(w2), f(w2s), f(w2d), f(w2inv), f(v1)


@functools.lru_cache(maxsize=None)
def _const_dense(l_len):
    n = 2 * l_len
    cl, sl = _cs(np.outer(np.arange(l_len), np.arange(l_len)), l_len)
    four_c, four_s = cl / math.sqrt(l_len), sl / math.sqrt(l_len)
    cf, sf = _cs(np.outer(np.arange(l_len), np.arange(n)), n)
    fwd = np.block([[cf, -sf], [sf, cf]])
    inv = np.block([[cf.T, sf.T], [-sf.T, cf.T]]) / n
    f = lambda m: m.astype(np.float32)
    return f(four_c), f(four_s), f(cf), f(-sf), f(fwd), f(inv)


@functools.lru_cache(maxsize=None)
def _const_filter_features(l_len):
    t01 = np.linspace(0.0, 1.0, l_len)
    bands = np.linspace(1e-4, HY_BANDS - 1, HY_BANDS)
    ang = (2.0 * np.pi / l_len) * np.arange(l_len)[:, None] * bands[None, :]
    z = np.concatenate([t01[:, None], np.cos(ang), -np.sin(ang)], -1)
    zt = np.zeros((LANES, l_len), np.float64)
    zt[:HY_EMB] = z.T
    max_decay = math.log(HY_DECAY_TARGET) / HY_FAST_DECAY
    min_decay = math.log(HY_DECAY_TARGET) / HY_SLOW_DECAY
    deltas = np.abs(np.linspace(min_decay, max_decay, HY_CH))
    return zt.astype(np.float32), t01[None, :].astype(np.float32), deltas[:, None].astype(np.float32)


def _chunk_time(m):
    rows, l_len = m.shape
    return np.ascontiguousarray(m.reshape(rows, l_len // LANES, LANES).transpose(1, 0, 2))


def _ada_kernel(c_ref, w_ref, b_ref, o_ref):
    s = _silu(c_ref[...]).astype(BF16)
    o_ref[...] = _dot(s, w_ref[...].astype(BF16)) + b_ref[...]


def _ada(cond8, ada_w, ada_b):
    depth, d, n6 = ada_w.shape
    tn = 1536
    return pl.pallas_call(
        _ada_kernel,
        grid=(depth, n6 // tn),
        in_specs=[pl.BlockSpec((8, d), lambda i, j: (0, 0)),
                  pl.BlockSpec((None, d, tn), lambda i, j: (i, 0, j)),
                  pl.BlockSpec((None, 1, tn), lambda i, j: (i, 0, j))],
        out_specs=pl.BlockSpec((None, 8, tn), lambda i, j: (i, 0, j)),
        out_shape=jax.ShapeDtypeStruct((depth, 8, n6), F32),
        compiler_params=_params("parallel", "parallel"),
        name="ada_mod",
    )(cond8, ada_w, ada_b.reshape(depth, 1, n6))


def _inproj_kernel(x_ref, g_ref, mod_ref, wt_ref, cdft_ref, zf_ref, ph_ref):
    d = D_MODEL
    u = _norm_mod(x_ref[...], g_ref[...], mod_ref[:, 0:d], mod_ref[:, d:2 * d]).astype(BF16)
    cdft = cdft_ref[...].astype(BF16)
    for g in range(FOURIER_GROUPS):
        rows = slice(g * GROUP_DIM, (g + 1) * GROUP_DIM)
        pt = lax.dot_general(wt_ref[rows, :], u, _NT, preferred_element_type=F32)
        z = _dot(cdft, pt.astype(BF16))
        zf_ref[0, rows, :] = z[0:GROUP_DIM]
        zf_ref[1, rows, :] = z[GROUP_DIM:]
    rb = 512
    for r0 in range(FOURIER_CH, wt_ref.shape[0], rb):
        ph_ref[r0 - FOURIER_CH:r0 - FOURIER_CH + rb, :] = lax.dot_general(
            wt_ref[r0:r0 + rb, :], u, _NT, preferred_element_type=F32)


def _inproj(x, g, mod3, w_in_t, tl=1024):
    b, l, d = x.shape
    tl = min(tl, l)
    nh = w_in_t.shape[0] - FOURIER_CH
    return pl.pallas_call(
        _inproj_kernel,
        grid=(b, l // tl),
        in_specs=[pl.BlockSpec((None, tl, d), lambda i, t: (i, t, 0)),
                  pl.BlockSpec((1, d), lambda i, t: (0, 0)),
                  pl.BlockSpec((None, 1, 6 * d), lambda i, t: (i, 0, 0)),
                  _resident(w_in_t.shape),
                  _resident((2 * GROUP_DIM, GROUP_DIM))],
        out_specs=[pl.BlockSpec((None, 2, FOURIER_CH, tl), lambda i, t: (i, 0, 0, t)),
                   pl.BlockSpec((None, nh, tl), lambda i, t: (i, 0, t))],
        out_shape=[jax.ShapeDtypeStruct((b, 2, FOURIER_CH, l), F32),
                   jax.ShapeDtypeStruct((b, nh, l), F32)],
        compiler_params=_params("parallel", "parallel"),
        name="inproj",
    )(x, g, mod3, w_in_t, jnp.asarray(_const_channel_dft()))


def _twiddle_store(ys_ref, c, yr, yi, tr, ti):
    ys_ref[c, :, 0:LANES] = (yr * tr - yi * ti).astype(ys_ref.dtype)
    ys_ref[c, :, LANES:] = (yr * ti + yi * tr).astype(ys_ref.dtype)


def _fourier_kernel(z_ref, w1_ref, tw_ref, w2_ref, o_ref, zs_ref, ys_ref):
    _, ct, a_len, _ = zs_ref.shape
    w1 = w1_ref[...].astype(BF16)
    tr, ti = tw_ref[0], tw_ref[1]
    for ri in range(2):
        zs_ref[ri] = z_ref[ri].reshape(ct, a_len, LANES)

    def stage1(p, carry):
        c = 2 * p
        x = jnp.concatenate([jnp.concatenate([zs_ref[0, c + h], zs_ref[1, c + h]], 0) for h in range(2)], 1)
        y = _dot(w1, x.astype(BF16))
        for h in range(2):
            yh = y[:, h * LANES:(h + 1) * LANES]
            _twiddle_store(ys_ref, c + h, yh[0:a_len], yh[a_len:], tr, ti)
        return carry

    lax.fori_loop(0, ct // 2, stage1, 0, unroll=4)
    y = ys_ref[...].reshape(ct * a_len, 2 * LANES)
    zr = _dot(y, w2_ref[...].astype(BF16))
    tiles = []
    for c in range(ct):
        t = zr[c * a_len:(c + 1) * a_len].T
        tiles.append(jnp.concatenate([t[h * a_len:(h + 1) * a_len] for h in range(LANES // a_len)], 1))
    o_ref[...] = jnp.stack(tiles, 0).reshape(ct, a_len * LANES)


def _fourier_latent(zf, ct=32):
    b, _, ch, l = zf.shape
    a_len = l // LANES
    w1, tw, w2 = (jnp.asarray(m) for m in _const_fourier(a_len))
    return pl.pallas_call(
        _fourier_kernel,
        grid=(b, ch // ct),
        in_specs=[pl.BlockSpec((None, 2, ct, l), lambda i, c: (i, 0, c, 0)),
                  _resident(w1.shape), _resident(tw.shape), _resident(w2.shape)],
        out_specs=pl.BlockSpec((None, ct, l), lambda i, c: (i, c, 0)),
        out_shape=jax.ShapeDtypeStruct((b, ch, l), F32),
        scratch_shapes=[pltpu.VMEM((2, ct, a_len, LANES), F32),
                        pltpu.VMEM((ct, a_len, 2 * LANES), BF16)],
        compiler_params=_params("parallel", "parallel"),
        name="fourier_latent",
    )(zf, w1, tw, w2)


def _fourier_ctx_kernel(z_ref, c_ref, s_ref, o_ref):
    zr = z_ref[0].astype(BF16)
    zi = z_ref[1].astype(BF16)
    o_ref[...] = _dot(zr, c_ref[...].astype(BF16)) + _dot(zi, s_ref[...].astype(BF16))


def _fourier_ctx(zf):
    b, _, ch, l = zf.shape
    four_c, four_s = (jnp.asarray(m) for m in _const_dense(l)[:2])
    return pl.pallas_call(
        _fourier_ctx_kernel,
        grid=(b,),
        in_specs=[pl.BlockSpec((None, 2, ch, l), lambda i: (i, 0, 0, 0)),
                  _resident((l, l)), _resident((l, l))],
        out_specs=pl.BlockSpec((None, ch, l), lambda i: (i, 0, 0)),
        out_shape=jax.ShapeDtypeStruct((b, ch, l), F32),
        compiler_params=_params("parallel"),
        name="fourier_ctx",
    )(zf, four_c, four_s)


def _filter_mlp(zt, w1t, b1, freq, w2t, b2):
    h = jnp.sin(freq * (_dot_hi(w1t, zt) + b1))
    return jnp.sin(freq * (_dot_hi(w2t, h) + b2))


def _hfilter_kernel(zt_ref, t_ref, dl_ref, w1t_ref, b1_ref, fr_ref, w2t_ref, b2_ref, w3t_ref,
                    w1f_ref, tw_ref, w2s_ref, w2d_ref, o_ref, h2_ref, xs_ref, ys_ref):
    cf = o_ref.shape[0]
    a_len = xs_ref.shape[1]
    k1n = 2 * a_len
    l = a_len * LANES

    @pl.when((pl.program_id(0) == 0) & (pl.program_id(1) == 0))
    def _():
        step = min(l, 1024)
        for s in range(0, l, step):
            h2_ref[:, s:s + step] = _filter_mlp(zt_ref[:, s:s + step], w1t_ref[...], b1_ref[...], fr_ref[...],
                                                w2t_ref[...], b2_ref[...]).astype(h2_ref.dtype)

    w3 = w3t_ref[...].reshape(2 * cf, HY_HID).astype(BF16)
    decay = jnp.exp(-(dl_ref[...] * t_ref[...]))
    hv = _dot(w3, h2_ref[...])
    f = hv[0:cf] * decay
    bk = hv[cf:] * decay
    tot = jnp.sum(jnp.abs(f), axis=1, keepdims=True) + jnp.sum(jnp.abs(bk), axis=1, keepdims=True)
    inv = 1.0 / tot
    lane = lax.broadcasted_iota(jnp.int32, (cf, l), 1)
    f = f * inv
    bk = jnp.where(lane == 0, 0.0, bk * inv)
    xs_ref[0:cf] = (f + bk).reshape(cf, a_len, LANES).astype(xs_ref.dtype)
    xs_ref[cf:] = (f - bk).reshape(cf, a_len, LANES).astype(xs_ref.dtype)

    w1f = w1f_ref[...].astype(BF16)
    tr, ti = tw_ref[0], tw_ref[1]

    def stage1(p, carry):
        c = 2 * p
        y = _dot(w1f, jnp.concatenate([xs_ref[c], xs_ref[c + 1]], 1))
        for h in range(2):
            yh = y[:, h * LANES:(h + 1) * LANES]
            _twiddle_store(ys_ref, c + h, yh[0:k1n], yh[k1n:], tr, ti)
        return carry

    lax.fori_loop(0, cf, stage1, 0, unroll=4)
    ysum = ys_ref[0:cf].reshape(cf * k1n, 2 * LANES)
    ydif = ys_ref[cf:].reshape(cf * k1n, 2 * LANES)
    o_ref[:, :, 0:LANES] = _dot(ysum, w2s_ref[...].astype(BF16)).reshape(cf, k1n, LANES)
    o_ref[:, :, LANES:] = _dot(ydif, w2d_ref[...].astype(BF16)).reshape(cf, k1n, LANES)


def _hfilter_latent(l, w1t, b1, freq, w2t, b2, w3t, cf=32):
    a_len = l // LANES
    k1n = 2 * a_len
    zt, t01, deltas = (jnp.asarray(m) for m in _const_filter_features(l))
    _, w1f, tw, _, w2s, w2d, _, _ = (jnp.asarray(m) for m in _const_hyena(a_len))
    orders = w3t.shape[1]
    return pl.pallas_call(
        _hfilter_kernel,
        grid=(orders, HY_CH // cf),
        in_specs=[_resident(zt.shape), _resident(t01.shape),
                  pl.BlockSpec((cf, 1), lambda o, c: (c, 0)),
                  _resident(w1t.shape), _resident(b1.shape), _resident(freq.shape),
                  _resident(w2t.shape), _resident(b2.shape),
                  pl.BlockSpec((2, None, cf, HY_HID), lambda o, c: (0, o, c, 0)),
                  _resident(w1f.shape), _resident(tw.shape), _resident(w2s.shape), _resident(w2d.shape)],
        out_specs=pl.BlockSpec((None, cf, k1n, 2 * LANES), lambda o, c: (o, c, 0, 0)),
        out_shape=jax.ShapeDtypeStruct((orders, HY_CH, k1n, 2 * LANES), F32),
        scratch_shapes=[pltpu.VMEM((HY_HID, l), BF16),
                        pltpu.VMEM((2 * cf, a_len, LANES), BF16),
                        pltpu.VMEM((2 * cf, k1n, 2 * LANES), BF16)],
        compiler_params=_params("arbitrary", "arbitrary"),
        name="hyena_filter_latent",
    )(zt, t01, deltas, w1t, b1, freq, w2t, b2, w3t, w1f, tw, w2s, w2d)


def _hfilter_ctx_kernel(zt_ref, t_ref, dl_ref, w1t_ref, b1_ref, fr_ref, w2t_ref, b2_ref, w3t_ref,
                        cf_ref, sf_ref, o_ref):
    l = zt_ref.shape[1]
    h2 = _filter_mlp(zt_ref[...], w1t_ref[...], b1_ref[...], fr_ref[...], w2t_ref[...], b2_ref[...])
    ch = w3t_ref.shape[1]
    w3 = w3t_ref[...].reshape(2 * ch, HY_HID)
    dl = dl_ref[...]
    hv = _dot_hi(w3, h2) * jnp.exp(-(jnp.concatenate([dl, dl], 0) * t_ref[...]))
    tot = jnp.sum(jnp.abs(hv), axis=1, keepdims=True)
    inv = 1.0 / (tot[0:ch] + tot[ch:])
    f = hv[0:ch] * inv
    lane = lax.broadcasted_iota(jnp.int32, (ch, l), 1)
    bk = jnp.where(lane == 0, 0.0, hv[ch:] * inv)
    o_ref[:, 0:2 * l] = _dot((f + bk).astype(BF16), cf_ref[...].astype(BF16))
    o_ref[:, 2 * l:] = _dot((f - bk).astype(BF16), sf_ref[...].astype(BF16))


def _hfilter_ctx(l, w1t, b1, freq, w2t, b2, w3t):
    zt, t01, deltas = (jnp.asarray(m) for m in _const_filter_features(l))
    cfm, nsf = (jnp.asarray(m) for m in _const_dense(l)[2:4])
    orders = w3t.shape[1]
    return pl.pallas_call(
        _hfilter_ctx_kernel,
        grid=(orders,),
        in_specs=[_resident(zt.shape), _resident(t01.shape), _resident(deltas.shape),
                  _resident(w1t.shape), _resident(b1.shape), _resident(freq.shape),
                  _resident(w2t.shape), _resident(b2.shape),
                  pl.BlockSpec((2, None, HY_CH, HY_HID), lambda o: (0, o, 0, 0)),
                  _resident(cfm.shape), _resident(nsf.shape)],
        out_specs=pl.BlockSpec((None, HY_CH, 4 * l), lambda o: (o, 0, 0)),
        out_shape=jax.ShapeDtypeStruct((orders, HY_CH, 4 * l), F32),
        compiler_params=_params("parallel"),
        name="hyena_filter_ctx",
    )(zt, t01, deltas, w1t, b1, freq, w2t, b2, w3t, cfm, nsf)


def _short_conv(x, cw, cb):
    l = x.shape[1]
    lane = lax.broadcasted_iota(jnp.int32, x.shape, 1)
    prev = jnp.where(lane == 0, 0.0, pltpu.roll(x, 1, 1))
    nxt = jnp.where(lane == l - 1, 0.0, pltpu.roll(x, l - 1, 1))
    return cw[:, 0:1] * prev + cw[:, 1:2] * x + cw[:, 2:3] * nxt + cb


def _short_conv_tile(x, w0, w1, w2, bias):
    a_len = x.shape[0]
    lane = lax.broadcasted_iota(jnp.int32, x.shape, 1)
    row = lax.broadcasted_iota(jnp.int32, x.shape, 0)
    xl = pltpu.roll(x, 1, 1)
    prev = jnp.where(lane == 0, jnp.where(row == 0, 0.0, pltpu.roll(xl, 1, 0)), xl)
    xr = pltpu.roll(x, LANES - 1, 1)
    nxt = jnp.where(lane == LANES - 1, jnp.where(row == a_len - 1, 0.0, pltpu.roll(xr, a_len - 1, 0)), xr)
    return w0 * prev + w1 * x + w2 * nxt + bias


def _hconv_kernel(cw_ref, cb_ref, sk_ref, x1_ref, x2_ref, v_ref, hf_ref,
                  w1_ref, tw_ref, w2_ref, w2i_ref, v1_ref, o_ref, g1_ref, g2_ref, vb_ref, ys_ref, us_ref, *, group):
    nb, ct, a_len, _ = vb_ref.shape
    k1n = 2 * a_len
    nch3 = cb_ref.shape[0]
    c0 = pl.program_id(0) * ct
    w1 = w1_ref[...].astype(BF16)
    w2 = w2_ref[...].astype(BF16)
    w2i = w2i_ref[...].astype(BF16)
    v1 = v1_ref[...].astype(BF16)
    tr, ti = tw_ref[0], tw_ref[1]

    for src, dst in ((x1_ref, g1_ref), (x2_ref, g2_ref), (v_ref, vb_ref)):
        for bi in range(nb):
            dst[bi] = src[bi].reshape(ct, a_len, LANES)

    def sconv(x, ch):
        return _short_conv_tile(x, cw_ref[ch], cw_ref[nch3 + ch], cw_ref[2 * nch3 + ch], cb_ref[ch])

    def prep(c, carry):
        for bi in range(nb):
            vb_ref[bi, c] = sconv(vb_ref[bi, c], 2 * HY_CH + c0 + c)
        return carry

    lax.fori_loop(0, ct, prep, 0, unroll=4)

    for order, gate_ref in enumerate((g1_ref, g2_ref)):
        def stage1(p, carry):
            c = 2 * p
            x = jnp.concatenate([jnp.concatenate([vb_ref[0, c + h], vb_ref[1, c + h]], 0) for h in range(2)], 1)
            y = _dot(w1, x.astype(BF16))
            for h in range(2):
                yh = y[:, h * LANES:(h + 1) * LANES]
                _twiddle_store(ys_ref, c + h, yh[0:k1n], yh[k1n:], tr, ti)
            return carry

        lax.fori_loop(0, ct // 2, stage1, 0, unroll=4)

        def middle(g, carry):
            cs = pl.ds(g * group, group)
            z = _dot(ys_ref[cs].reshape(group * k1n, 2 * LANES), w2)
            h = hf_ref[order, cs].reshape(group * k1n, 2 * LANES)
            zr, zi = z[:, 0:LANES], z[:, LANES:]
            hr, hi = h[:, 0:LANES], h[:, LANES:]
            p = jnp.concatenate([zr * hr - zi * hi, zr * hi + zi * hr], 1).astype(BF16)
            u = _dot(p, w2i).reshape(group, k1n, 2 * LANES)
            ur, ui = u[:, :, 0:LANES], u[:, :, LANES:]
            us_ref[cs, :, 0:LANES] = (ur * tr + ui * ti).astype(BF16)
            us_ref[cs, :, LANES:] = (ui * tr - ur * ti).astype(BF16)
            return carry

        lax.fori_loop(0, ct // group, middle, 0)

        def stage1_inv(c, carry):
            q = _dot(v1, us_ref[c])
            conv = (q[0:a_len, 0:LANES] - q[a_len:, LANES:], q[a_len:, 0:LANES] + q[0:a_len, LANES:])
            skip = sk_ref[order * HY_CH + c0 + c]
            for bi in range(nb):
                gate = sconv(gate_ref[bi, c], order * HY_CH + c0 + c)
                vb_ref[bi, c] = gate * (conv[bi] + vb_ref[bi, c] * skip)
            return carry

        lax.fori_loop(0, ct, stage1_inv, 0, unroll=4)

    for bi in range(nb):
        o_ref[bi] = vb_ref[bi].reshape(ct, a_len * LANES)


def _hconv_latent(ph, hf, conv_w, conv_b, skip, ct=16, group=4):
    b, nch3, l = ph.shape
    assert b == 2, "the two batch entries ride as real/imaginary parts"
    a_len = l // LANES
    k1n = 2 * a_len
    w1, _, tw, w2, _, _, w2i, v1 = (jnp.asarray(m) for m in _const_hyena(a_len))
    nblk = HY_CH // ct
    smem = pl.BlockSpec(memory_space=pltpu.SMEM)
    part = lambda off: pl.BlockSpec((b, ct, l), lambda c: (0, c + off * nblk, 0))
    scratch = pltpu.VMEM((b, ct, a_len, LANES), F32)
    return pl.pallas_call(
        functools.partial(_hconv_kernel, group=group),
        grid=(nblk,),
        in_specs=[smem, smem, smem, part(0), part(1), part(2),
                  pl.BlockSpec((2, ct, k1n, 2 * LANES), lambda c: (0, c, 0, 0)),
                  _resident(w1.shape), _resident(tw.shape), _resident(w2.shape),
                  _resident(w2i.shape), _resident(v1.shape)],
        out_specs=pl.BlockSpec((b, ct, l), lambda c: (0, c, 0)),
        out_shape=jax.ShapeDtypeStruct((b, HY_CH, l), F32),
        scratch_shapes=[scratch, scratch, scratch,
                        pltpu.VMEM((ct, k1n, 2 * LANES), BF16),
                        pltpu.VMEM((ct, k1n, 2 * LANES), BF16)],
        compiler_params=_params("parallel"),
        name="hyena_conv_latent",
    )(conv_w.reshape(-1), conv_b, skip.reshape(-1), ph, ph, ph, hf, w1, tw, w2, w2i, v1)


def _hconv_ctx_kernel(cw_ref, cb_ref, sk_ref, p_ref, hf_ref, fwd_ref, inv_ref, o_ref):
    nb, _, l = p_ref.shape
    ch = HY_CH
    z = [_short_conv(p_ref[bi], cw_ref[...], cb_ref[...]) for bi in range(nb)]
    fwd = fwd_ref[...].astype(BF16)
    inv = inv_ref[...].astype(BF16)
    cur = [zz[2 * ch:] for zz in z]
    for order in range(2):
        spec = _dot(jnp.concatenate(cur, 1).astype(BF16), fwd)
        h = hf_ref[order]
        zr, zi, hr, hi = spec[:, 0:2 * l], spec[:, 2 * l:], h[:, 0:2 * l], h[:, 2 * l:]
        prod = jnp.concatenate([zr * hr - zi * hi, zr * hi + zi * hr], 1).astype(BF16)
        y = _dot(prod, inv)
        sk = sk_ref[:, order:order + 1]
        cur = [z[bi][order * ch:(order + 1) * ch] * (y[:, bi * l:(bi + 1) * l] + cur[bi] * sk) for bi in range(nb)]
    for bi in range(nb):
        o_ref[bi] = cur[bi]


def _hconv_ctx(ph, hf, conv_w, conv_b, skip):
    b, nch3, l = ph.shape
    assert b == 2
    fwd, inv = (jnp.asarray(m) for m in _const_dense(l)[4:6])
    cw = jnp.pad(conv_w.T, ((0, 0), (0, 5)))
    return pl.pallas_call(
        _hconv_ctx_kernel,
        in_specs=[pl.BlockSpec(cw.shape, lambda: (0, 0)),
                  pl.BlockSpec((nch3, 1), lambda: (0, 0)),
                  pl.BlockSpec((HY_CH, 2), lambda: (0, 0)),
                  pl.BlockSpec(ph.shape, lambda: (0, 0, 0)),
                  pl.BlockSpec(hf.shape, lambda: (0, 0, 0)),
                  pl.BlockSpec(fwd.shape, lambda: (0, 0)),
                  pl.BlockSpec(inv.shape, lambda: (0, 0))],
        out_specs=pl.BlockSpec((b, HY_CH, l), lambda: (0, 0, 0)),
        out_shape=jax.ShapeDtypeStruct((b, HY_CH, l), F32),
        compiler_params=pltpu.CompilerParams(vmem_limit_bytes=VMEM_LIMIT_BYTES),
        name="hyena_conv_ctx",
    )(cw, conv_b.reshape(-1, 1), skip.T, ph, hf, fwd, inv)


def _mix_ffn_kernel(*refs, channel_major, final, ff_chunk):
    d = D_MODEL
    if channel_major:
        h_ref, yf_ref, yh_ref, wo_ref, mod_ref, g2_ref, wg_ref, wu_ref, wd_ref = refs[:9]
        rest = refs[9:]
        y = (lax.dot_general(yf_ref[...].astype(BF16), wo_ref[0:FOURIER_CH, :], _TN, preferred_element_type=F32)
             + lax.dot_general(yh_ref[...].astype(BF16), wo_ref[FOURIER_CH:, :], _TN, preferred_element_type=F32))
    else:
        h_ref, ya_ref, wo_ref, mod_ref, g2_ref, wg_ref, wu_ref, wd_ref = refs[:8]
        rest = refs[8:]
        y = _dot(ya_ref[...], wo_ref[...])
    if final:
        fg_ref, o_ref = rest
    else:
        (o_ref,) = rest
    h1 = h_ref[...] + mod_ref[:, 2 * d:3 * d] * y
    u = _norm_mod(h1, g2_ref[...], mod_ref[:, 3 * d:4 * d], mod_ref[:, 4 * d:5 * d]).astype(BF16)
    acc = jnp.zeros(h1.shape, F32)
    for j in range(D_FF // ff_chunk):
        cols = slice(j * ff_chunk, (j + 1) * ff_chunk)
        act = (_silu(_dot(u, wg_ref[:, cols])) * _dot(u, wu_ref[:, cols])).astype(BF16)
        acc = acc + _dot(act, wd_ref[cols, :])
    h2 = h1 + mod_ref[:, 5 * d:6 * d] * acc
    if final:
        h2 = _rms(h2, fg_ref[...])
    o_ref[...] = h2


def _mix_ffn(h, ys, wo, mod3, g2, wg, wu, wd, final_g=None, tl=512, ff_chunk=256):
    b, l, d = h.shape
    tl = min(tl, l)
    channel_major = len(ys) == 2
    tok = pl.BlockSpec((None, tl, d), lambda i, t: (i, t, 0))
    if channel_major:
        y_specs = [pl.BlockSpec((None, y.shape[1], tl), lambda i, t: (i, 0, t)) for y in ys]
    else:
        y_specs = [tok]
    in_specs = ([tok] + y_specs + [_resident(wo.shape), pl.BlockSpec((None, 1, 6 * d), lambda i, t: (i, 0, 0)),
                                    pl.BlockSpec((1, d), lambda i, t: (0, 0)),
                                    _resident(wg.shape), _resident(wu.shape), _resident(wd.shape)])
    args = [h, *ys, wo, mod3, g2, wg, wu, wd]
    if final_g is not None:
        in_specs.append(pl.BlockSpec((1, d), lambda i, t: (0, 0)))
        args.append(final_g)
    return pl.pallas_call(
        functools.partial(_mix_ffn_kernel, channel_major=channel_major, final=final_g is not None,
                          ff_chunk=ff_chunk),
        grid=(b, l // tl),
        in_specs=in_specs,
        out_specs=tok,
        out_shape=jax.ShapeDtypeStruct((b, l, d), F32),
        compiler_params=_params("parallel", "parallel"),
        name="mix_ffn_final" if final_g is not None else "mix_ffn",
    )(*args)


def _qkv_kernel(x_ref, g_ref, mod_ref, w_ref, q_ref, k_ref, v_ref):
    d = D_MODEL
    u = _norm_mod(x_ref[...], g_ref[...], mod_ref[:, 0:d], mod_ref[:, d:2 * d]).astype(BF16)
    q_ref[...] = (_dot(u, w_ref[:, 0:d]) * (NA_HEAD_DIM ** -0.5)).astype(BF16)
    k_ref[...] = _dot(u, w_ref[:, d:2 * d]).astype(BF16)
    v_ref[...] = _dot(u, w_ref[:, 2 * d:]).astype(BF16)


def _qkv(h, g, mod3, w, tl=1024):
    b, l, d = h.shape
    tl = min(tl, l)
    tok = pl.BlockSpec((None, tl, d), lambda i, t: (i, t, 0))
    return pl.pallas_call(
        _qkv_kernel,
        grid=(b, l // tl),
        in_specs=[tok, pl.BlockSpec((1, d), lambda i, t: (0, 0)),
                  pl.BlockSpec((None, 1, 6 * d), lambda i, t: (i, 0, 0)), _resident(w.shape)],
        out_specs=[tok, tok, tok],
        out_shape=[jax.ShapeDtypeStruct((b, l, d), BF16)] * 3,
        compiler_params=_params("parallel", "parallel"),
        name="qkv_proj",
    )(h, g, mod3, w)


NA_QROWS = 4
NA_KROWS = 12
NA_PATTERNS = 3


def _na_key_start(r0, grid_rows):
    return jnp.minimum(jnp.clip(r0 - NA_KH // 2, 0, grid_rows - NA_KH), grid_rows - NA_KROWS)


def _na_pattern(pat, i, j):
    if pat == 0:
        return j - i + NA_KH - 1, j < NA_KH
    if pat == 1:
        return j - i + NA_KH // 2 - 1, i <= j < i + NA_KH
    return j - i - 1, j >= NA_KROWS - NA_KH


def _bias_blocks_kernel(rpb_ref, o_ref):
    pair = pl.program_id(0)
    ndr, ndc = 2 * NA_KH - 1, 2 * NA_KW - 1
    q = lax.broadcasted_iota(jnp.int32, (GRID_W, LANES), 0)
    lane = lax.broadcasted_iota(jnp.int32, (GRID_W, LANES), 1)
    kcol = lane % GRID_W
    upper = lane >= GRID_W
    wstart = jnp.clip(q - NA_KW // 2, 0, GRID_W - NA_KW)
    ok = (kcol >= wstart) & (kcol < wstart + NA_KW)
    dc = kcol - q + (NA_KW - 1)
    neg = jnp.full((GRID_W, LANES), NEG_BIG, F32)
    for e in range(2):
        base = (2 * pair + e) * ndr * ndc
        tiles = []
        for dr in range(ndr):
            acc = jnp.zeros((GRID_W, LANES), F32)
            for dd in range(ndc):
                acc = jnp.where(dc == dd, rpb_ref[base + dr * ndc + dd], acc)
            tiles.append(jnp.where(ok, acc, NEG_BIG))
        for pat in range(NA_PATTERNS):
            for i in range(NA_QROWS):
                for p in range(NA_KROWS // 2):
                    halves = []
                    for j in (2 * p, 2 * p + 1):
                        dr, visible = _na_pattern(pat, i, j)
                        halves.append(tiles[dr] if visible else neg)
                    o_ref[e, pat, i * GRID_W:(i + 1) * GRID_W, p * LANES:(p + 1) * LANES] = jnp.where(
                        upper, halves[1], halves[0])


def _bias_blocks(rpb):
    heads = rpb.shape[0]
    nq, nk = NA_QROWS * GRID_W, NA_KROWS * GRID_W
    return pl.pallas_call(
        _bias_blocks_kernel,
        grid=(heads // 2,),
        in_specs=[pl.BlockSpec(memory_space=pltpu.SMEM)],
        out_specs=pl.BlockSpec((None, 2, NA_PATTERNS, nq, nk), lambda p: (p, 0, 0, 0, 0)),
        out_shape=jax.ShapeDtypeStruct((heads // 2, 2, NA_PATTERNS, nq, nk), F32),
        compiler_params=_params("parallel"),
        name="na_bias_blocks",
    )(rpb.reshape(-1))


def _na_kernel(q_ref, k_ref, v_ref, kc_ref, vc_ref, bias_ref, o_ref, *, blocks_per_step, grid_rows):
    nq, nk = NA_QROWS * GRID_W, NA_KROWS * GRID_W
    lane = lax.broadcasted_iota(jnp.int32, (nq, LANES), 1)
    kc = kc_ref[...]
    vc = vc_ref[...]
    step = pl.program_id(2)

    def one_block(i, carry):
        r0 = (step * blocks_per_step + i) * NA_QROWS
        pat = jnp.where(r0 == 0, 0, jnp.where(r0 == grid_rows - NA_QROWS, 2, 1))
        qoff = pl.multiple_of(i * nq, nq)
        koff = pl.multiple_of(_na_key_start(r0, grid_rows) * GRID_W, GRID_W)
        q = q_ref[pl.ds(qoff, nq), :]
        ku = k_ref[pl.ds(koff, nk), :]
        vu = v_ref[pl.ds(koff, nk), :]
        outs = []
        for e in range(2):
            head = (lane >= NA_HEAD_DIM) if e else (lane < NA_HEAD_DIM)
            qh = jnp.where(head, q, jnp.zeros_like(q))
            sl = lax.dot_general(qh, ku, _NT, preferred_element_type=F32) + bias_ref[e, pat]
            sc = lax.dot_general(qh, kc, _NT, preferred_element_type=F32)
            s = jnp.concatenate([sl, sc], 1)
            p = jnp.exp(s - jnp.max(s, axis=1, keepdims=True))
            den = jnp.sum(p, axis=1, keepdims=True)
            p = p.astype(BF16)
            o = _dot(p[:, 0:nk], vu) + _dot(p[:, nk:], vc)
            outs.append(o / den)
        o_ref[pl.ds(qoff, nq), :] = jnp.where(lane < NA_HEAD_DIM, outs[0], outs[1]).astype(o_ref.dtype)
        return carry

    lax.fori_loop(0, blocks_per_step, one_block, 0, unroll=2)


def _na(q, k, v, kc, vc, bias_blocks, blocks_per_step=4):
    b, l, d = q.shape
    lc = kc.shape[1]
    grid_rows = l // GRID_W
    assert grid_rows % (NA_QROWS * blocks_per_step) == 0 and grid_rows >= NA_KROWS
    npairs = d // LANES
    tq = blocks_per_step * NA_QROWS * GRID_W
    return pl.pallas_call(
        functools.partial(_na_kernel, blocks_per_step=blocks_per_step, grid_rows=grid_rows),
        grid=(npairs, b, l // tq),
        in_specs=[pl.BlockSpec((None, tq, LANES), lambda p, i, t: (i, t, p)),
                  pl.BlockSpec((None, l, LANES), lambda p, i, t: (i, 0, p)),
                  pl.BlockSpec((None, l, LANES), lambda p, i, t: (i, 0, p)),
                  pl.BlockSpec((None, lc, LANES), lambda p, i, t: (i, 0, p)),
                  pl.BlockSpec((None, lc, LANES), lambda p, i, t: (i, 0, p)),
                  pl.BlockSpec((None,) + bias_blocks.shape[1:], lambda p, i, t: (p, 0, 0, 0, 0))],
        out_specs=pl.BlockSpec((None, tq, LANES), lambda p, i, t: (i, t, p)),
        out_shape=jax.ShapeDtypeStruct((b, l, d), BF16),
        compiler_params=_params("parallel", "parallel", "parallel"),
        name="neighbourhood_attention",
    )(q, k, v, kc, vc, bias_blocks)


def kernel(x, c, ctx, c_ctx, ada_w, ada_b, norm1_g, norm2_g, ffn_w_gate, ffn_w_up, ffn_w_down, ab_w_in, ab_w_out, hy_conv_w, hy_conv_b, hy_f_w1, hy_f_b1, hy_f_freq, hy_f_w2, hy_f_b2, hy_f_w3, hy_bias, na_w_qkv, na_w_out, na_rpb, final_g):
    b, l, d = x.shape
    lc = ctx.shape[1]
    assert b == 2 and d == D_MODEL and l % 1024 == 0 and ada_w.shape[0] == 2

    cond8 = jnp.zeros((8, d), F32).at[0:b].set(c).at[b].set(c_ctx)
    mod = _ada(cond8, ada_w, ada_b)
    mod_lat = [mod[i, 0:b].reshape(b, 1, 6 * d) for i in range(2)]
    mod_ctx = [jnp.broadcast_to(mod[i, b:b + 1], (b, 6 * d)).reshape(b, 1, 6 * d) for i in range(2)]
    n1 = [norm1_g[i].reshape(1, d) for i in range(2)]
    n2 = [norm2_g[i].reshape(1, d) for i in range(2)]
    wg = [ffn_w_gate[i].astype(BF16) for i in range(2)]
    wu = [ffn_w_up[i].astype(BF16) for i in range(2)]
    wd = [ffn_w_down[i].astype(BF16) for i in range(2)]

    w_in_t = ab_w_in[0].T.astype(BF16)
    w_out = ab_w_out[0].astype(BF16)
    w1t = jnp.pad(hy_f_w1[0].T, ((0, 0), (0, LANES - HY_EMB)))
    col = lambda vec: vec.reshape(-1, 1)
    mlp = (w1t, col(hy_f_b1[0]), col(hy_f_freq[0]), hy_f_w2[0].T, col(hy_f_b2[0]),
           hy_f_w3[0].T.reshape(2, 2, HY_CH, HY_HID))

    zf, ph = _inproj(x, n1[0], mod_lat[0], w_in_t)
    y_f = _fourier_latent(zf)
    hf = _hfilter_latent(l, *mlp)
    y_h = _hconv_latent(ph, hf, hy_conv_w[0], hy_conv_b[0], hy_bias[0]).reshape(b, HY_CH, l)
    h_lat = _mix_ffn(x, (y_f, y_h), w_out, mod_lat[0], n2[0], wg[0], wu[0], wd[0])

    zfc, phc = _inproj(ctx, n1[0], mod_ctx[0], w_in_t)
    y_fc = _fourier_ctx(zfc)
    hfc = _hfilter_ctx(lc, *mlp)
    y_hc = _hconv_ctx(phc, hfc, hy_conv_w[0], hy_conv_b[0], hy_bias[0])
    h_ctx = _mix_ffn(ctx, (y_fc, y_hc), w_out, mod_ctx[0], n2[0], wg[0], wu[0], wd[0])

    w_qkv = na_w_qkv[0].astype(BF16)
    q, k, v = _qkv(h_lat, n1[1], mod_lat[1], w_qkv)
    _, kc, vc = _qkv(h_ctx, n1[1], mod_ctx[1], w_qkv)
    attn = _na(q, k, v, kc, vc, _bias_blocks(na_rpb[0]))
    return _mix_ffn(h_lat, (attn,), na_w_out[0].astype(BF16), mod_lat[1], n2[1], wg[1], wu[1], wd[1],
                    final_g=final_g.reshape(1, d))
```

```python
import functools
import math

import numpy as np
import jax
import jax.numpy as jnp
from jax import lax
from jax.experimental import pallas as pl
from jax.experimental.pallas import tpu as pltpu

F32 = jnp.float32
BF16 = jnp.bfloat16

D_MODEL = 1024
FOURIER_CH = 512
FOURIER_GROUPS = 4
GROUP_DIM = 128
HY_CH = 512
HY_BANDS = 16
HY_EMB = 33
HY_HID = 64
HY_FAST_DECAY = 0.3
HY_SLOW_DECAY = 1.5
HY_DECAY_TARGET = 1e-2
D_FF = 2816
NA_HEADS = 16
NA_HEAD_DIM = 64
NA_KH = 8
NA_KW = 16
GRID_W = 64
RMS_EPS = 1e-6
NEG_BIG = -1e30

LANES = 128
VMEM_LIMIT_BYTES = 56 * 1024 * 1024

_NT = (((1,), (1,)), ((), ()))
_TN = (((0,), (0,)), ((), ()))


def _params(*sem):
    return pltpu.CompilerParams(dimension_semantics=sem, vmem_limit_bytes=VMEM_LIMIT_BYTES)


def _dot(a, b):
    return jnp.dot(a, b, preferred_element_type=F32)


def _dot_hi(a, b):
    return jnp.dot(a, b, preferred_element_type=F32, precision=lax.Precision.HIGHEST)


def _resident(shape):
    nd = len(shape)
    return pl.BlockSpec(shape, lambda *_: (0,) * nd, pipeline_mode=pl.Buffered(1))


def _rms(x, g):
    ms = jnp.mean(x * x, axis=-1, keepdims=True)
    return x * lax.rsqrt(ms + RMS_EPS) * g


def _norm_mod(x, g, shift, scale):
    return _rms(x, g) * (1.0 + scale) + shift


def _silu(x):
    return x * jax.nn.sigmoid(x)


def _cs(num, den):
    ang = 2.0 * np.pi * (np.asarray(num, np.float64) % den) / den
    return np.cos(ang), np.sin(ang)


@functools.lru_cache(maxsize=None)
def _const_channel_dft():
    c, s = _cs(np.outer(np.arange(GROUP_DIM), np.arange(GROUP_DIM)), GROUP_DIM)
    return (np.concatenate([c, -s], 0) / math.sqrt(GROUP_DIM)).astype(np.float32)


@functools.lru_cache(maxsize=None)
def _const_fourier(a_len):
    n = a_len * LANES
    c, s = _cs(np.outer(np.arange(a_len), np.arange(a_len)), a_len)
    w1 = np.block([[c, s], [-s, c]])
    tc, ts = _cs(np.outer(np.arange(a_len), np.arange(LANES)), n)
    tw = np.stack([tc, -ts])
    c2, s2 = _cs(np.outer(np.arange(LANES), np.arange(LANES)), LANES)
    w2 = np.concatenate([c2, s2], 0) / math.sqrt(n)
    r = LANES // a_len
    perm = np.array([r * m + h for h in range(r) for m in range(a_len)])
    w2 = w2[:, perm]
    return w1.astype(np.float32), tw.astype(np.float32), w2.astype(np.float32)


@functools.lru_cache(maxsize=None)
def _const_hyena(a_len):
    k1n = 2 * a_len
    n = k1n * LANES
    c, s = _cs(np.outer(np.arange(k1n), np.arange(a_len)), k1n)
    w1 = np.block([[c, s], [-s, c]])
    w1f = np.concatenate([c, -s], 0)
    tc, ts = _cs(np.outer(np.arange(k1n), np.arange(LANES)), n)
    tw = np.stack([tc, -ts])
    c2, s2 = _cs(np.outer(np.arange(LANES), np.arange(LANES)), LANES)
    w2 = np.block([[c2, -s2], [s2, c2]])
    w2s = np.concatenate([c2, s2], 0)
    w2d = np.concatenate([-s2, c2], 0)
    w2inv = np.block([[c2, s2], [-s2, c2]])
    v1 = np.concatenate([c.T, s.T], 0) / n
    f = lambda m: m.astype(np.float32)
    return f(w1), f(w1f), f(tw), f(w2), f(w2s), f(w2d), f(w2inv), f(v1)


@functools.lru_cache(maxsize=None)
def _const_dense(l_len):
    n = 2 * l_len
    cl, sl = _cs(np.outer(np.arange(l_len), np.arange(l_len)), l_len)
    four_c, four_s = cl / math.sqrt(l_len), sl / math.sqrt(l_len)
    cf, sf = _cs(np.outer(np.arange(l_len), np.arange(n)), n)
    fwd = np.block([[cf, -sf], [sf, cf]])
    inv = np.block([[cf.T, sf.T], [-sf.T, cf.T]]) / n
    f = lambda m: m.astype(np.float32)
    return f(four_c), f(four_s), f(cf), f(-sf), f(fwd), f(inv)


@functools.lru_cache(maxsize=None)
def _const_filter_features(l_len):
    t01 = np.linspace(0.0, 1.0, l_len)
    bands = np.linspace(1e-4, HY_BANDS - 1, HY_BANDS)
    ang = (2.0 * np.pi / l_len) * np.arange(l_len)[:, None] * bands[None, :]
    z = np.concatenate([t01[:, None], np.cos(ang), -np.sin(ang)], -1)
    zt = np.zeros((LANES, l_len), np.float64)
    zt[:HY_EMB] = z.T
    max_decay = math.log(HY_DECAY_TARGET) / HY_FAST_DECAY
    min_decay = math.log(HY_DECAY_TARGET) / HY_SLOW_DECAY
    deltas = np.abs(np.linspace(min_decay, max_decay, HY_CH))
    return zt.astype(np.float32), t01[None, :].astype(np.float32), deltas[:, None].astype(np.float32)


def _chunk_time(m):
    rows, l_len = m.shape
    return np.ascontiguousarray(m.reshape(rows, l_len // LANES, LANES).transpose(1, 0, 2))


def _ada_kernel(c_ref, w_ref, b_ref, o_ref):
    s = _silu(c_ref[...]).astype(BF16)
    o_ref[...] = _dot(s, w_ref[...].astype(BF16)) + b_ref[...]


def _ada(cond8, ada_w, ada_b):
    depth, d, n6 = ada_w.shape
    tn = 1536
    return pl.pallas_call(
        _ada_kernel,
        grid=(depth, n6 // tn),
        in_specs=[pl.BlockSpec((8, d), lambda i, j: (0, 0)),
                  pl.BlockSpec((None, d, tn), lambda i, j: (i, 0, j)),
                  pl.BlockSpec((None, 1, tn), lambda i, j: (i, 0, j))],
        out_specs=pl.BlockSpec((None, 8, tn), lambda i, j: (i, 0, j)),
        out_shape=jax.ShapeDtypeStruct((depth, 8, n6), F32),
        compiler_params=_params("parallel", "parallel"),
        name="ada_mod",
    )(cond8, ada_w, ada_b.reshape(depth, 1, n6))


def _inproj_kernel(x_ref, g_ref, mod_ref, wt_ref, cdft_ref, zf_ref, ph_ref):
    d = D_MODEL
    u = _norm_mod(x_ref[...], g_ref[...], mod_ref[:, 0:d], mod_ref[:, d:2 * d]).astype(BF16)
    cdft = cdft_ref[...].astype(BF16)
    for g in range(FOURIER_GROUPS):
        rows = slice(g * GROUP_DIM, (g + 1) * GROUP_DIM)
        pt = lax.dot_general(wt_ref[rows, :], u, _NT, preferred_element_type=F32)
        z = _dot(cdft, pt.astype(BF16))
        zf_ref[0, rows, :] = z[0:GROUP_DIM]
        zf_ref[1, rows, :] = z[GROUP_DIM:]
    rb = 512
    for r0 in range(FOURIER_CH, wt_ref.shape[0], rb):
        ph_ref[r0 - FOURIER_CH:r0 - FOURIER_CH + rb, :] = lax.dot_general(
            wt_ref[r0:r0 + rb, :], u, _NT, preferred_element_type=F32)


def _inproj(x, g, mod3, w_in_t, tl=1024):
    b, l, d = x.shape
    tl = min(tl, l)
    nh = w_in_t.shape[0] - FOURIER_CH
    return pl.pallas_call(
        _inproj_kernel,
        grid=(b, l // tl),
        in_specs=[pl.BlockSpec((None, tl, d), lambda i, t: (i, t, 0)),
                  pl.BlockSpec((1, d), lambda i, t: (0, 0)),
                  pl.BlockSpec((None, 1, 6 * d), lambda i, t: (i, 0, 0)),
                  _resident(w_in_t.shape),
                  _resident((2 * GROUP_DIM, GROUP_DIM))],
        out_specs=[pl.BlockSpec((None, 2, FOURIER_CH, tl), lambda i, t: (i, 0, 0, t)),
                   pl.BlockSpec((None, nh, tl), lambda i, t: (i, 0, t))],
        out_shape=[jax.ShapeDtypeStruct((b, 2, FOURIER_CH, l), F32),
                   jax.ShapeDtypeStruct((b, nh, l), F32)],
        compiler_params=_params("parallel", "parallel"),
        name="inproj",
    )(x, g, mod3, w_in_t, jnp.asarray(_const_channel_dft()))


def _twiddle_store(ys_ref, c, yr, yi, tr, ti):
    ys_ref[c, :, 0:LANES] = (yr * tr - yi * ti).astype(ys_ref.dtype)
    ys_ref[c, :, LANES:] = (yr * ti + yi * tr).astype(ys_ref.dtype)


def _fourier_kernel(z_ref, w1_ref, tw_ref, w2_ref, o_ref, zs_ref, ys_ref):
    _, ct, a_len, _ = zs_ref.shape
    w1 = w1_ref[...].astype(BF16)
    tr, ti = tw_ref[0], tw_ref[1]
    for ri in range(2):
        zs_ref[ri] = z_ref[ri].reshape(ct, a_len, LANES)

    def stage1(p, carry):
        c = 2 * p
        x = jnp.concatenate([jnp.concatenate([zs_ref[0, c + h], zs_ref[1, c + h]], 0) for h in range(2)], 1)
        y = _dot(w1, x.astype(BF16))
        for h in range(2):
            yh = y[:, h * LANES:(h + 1) * LANES]
            _twiddle_store(ys_ref, c + h, yh[0:a_len], yh[a_len:], tr, ti)
        return carry

    lax.fori_loop(0, ct // 2, stage1, 0, unroll=4)
    y = ys_ref[...].reshape(ct * a_len, 2 * LANES)
    zr = _dot(y, w2_ref[...].astype(BF16))
    tiles = []
    for c in range(ct):
        t = zr[c * a_len:(c + 1) * a_len].T
        tiles.append(jnp.concatenate([t[h * a_len:(h + 1) * a_len] for h in range(LANES // a_len)], 1))
    o_ref[...] = jnp.stack(tiles, 0).reshape(ct, a_len * LANES)


def _fourier_latent(zf, ct=32):
    b, _, ch, l = zf.shape
    a_len = l // LANES
    w1, tw, w2 = (jnp.asarray(m) for m in _const_fourier(a_len))
    return pl.pallas_call(
        _fourier_kernel,
        grid=(b, ch // ct),
        in_specs=[pl.BlockSpec((None, 2, ct, l), lambda i, c: (i, 0, c, 0)),
                  _resident(w1.shape), _resident(tw.shape), _resident(w2.shape)],
        out_specs=pl.BlockSpec((None, ct, l), lambda i, c: (i, c, 0)),
        out_shape=jax.ShapeDtypeStruct((b, ch, l), F32),
        scratch_shapes=[pltpu.VMEM((2, ct, a_len, LANES), F32),
                        pltpu.VMEM((ct, a_len, 2 * LANES), BF16)],
        compiler_params=_params("parallel", "parallel"),
        name="fourier_latent",
    )(zf, w1, tw, w2)


def _fourier_ctx_kernel(z_ref, c_ref, s_ref, o_ref):
    zr = z_ref[0].astype(BF16)
    zi = z_ref[1].astype(BF16)
    o_ref[...] = _dot(zr, c_ref[...].astype(BF16)) + _dot(zi, s_ref[...].astype(BF16))


def _fourier_ctx(zf):
    b, _, ch, l = zf.shape
    four_c, four_s = (jnp.asarray(m) for m in _const_dense(l)[:2])
    return pl.pallas_call(
        _fourier_ctx_kernel,
        grid=(b,),
        in_specs=[pl.BlockSpec((None, 2, ch, l), lambda i: (i, 0, 0, 0)),
                  _resident((l, l)), _resident((l, l))],
        out_specs=pl.BlockSpec((None, ch, l), lambda i: (i, 0, 0)),
        out_shape=jax.ShapeDtypeStruct((b, ch, l), F32),
        compiler_params=_params("parallel"),
        name="fourier_ctx",
    )(zf, four_c, four_s)


def _filter_mlp(zt, w1t, b1, freq, w2t, b2):
    h = jnp.sin(freq * (_dot_hi(w1t, zt) + b1))
    return jnp.sin(freq * (_dot_hi(w2t, h) + b2))


def _hfilter_kernel(zt_ref, t_ref, dl_ref, w1t_ref, b1_ref, fr_ref, w2t_ref, b2_ref, w3t_ref,
                    w1f_ref, tw_ref, w2s_ref, w2d_ref, o_ref, h2_ref, xs_ref, ys_ref):
    cf = o_ref.shape[0]
    a_len = xs_ref.shape[1]
    k1n = 2 * a_len
    l = a_len * LANES

    @pl.when((pl.program_id(0) == 0) & (pl.program_id(1) == 0))
    def _():
        step = min(l, 1024)
        for s in range(0, l, step):
            h2_ref[:, s:s + step] = _filter_mlp(zt_ref[:, s:s + step], w1t_ref[...], b1_ref[...], fr_ref[...],
                                                w2t_ref[...], b2_ref[...]).astype(h2_ref.dtype)

    w3 = w3t_ref[...].reshape(2 * cf, HY_HID).astype(BF16)
    decay = jnp.exp(-(dl_ref[...] * t_ref[...]))
    hv = _dot(w3, h2_ref[...])
    f = hv[0:cf] * decay
    bk = hv[cf:] * decay
    tot = jnp.sum(jnp.abs(f), axis=1, keepdims=True) + jnp.sum(jnp.abs(bk), axis=1, keepdims=True)
    inv = 1.0 / tot
    lane = lax.broadcasted_iota(jnp.int32, (cf, l), 1)
    f = f * inv
    bk = jnp.where(lane == 0, 0.0, bk * inv)
    xs_ref[0:cf] = (f + bk).reshape(cf, a_len, LANES).astype(xs_ref.dtype)
    xs_ref[cf:] = (f - bk).reshape(cf, a_len, LANES).astype(xs_ref.dtype)

    w1f = w1f_ref[...].astype(BF16)
    tr, ti = tw_ref[0], tw_ref[1]

    def stage1(p, carry):
        c = 2 * p
        y = _dot(w1f, jnp.concatenate([xs_ref[c], xs_ref[c + 1]], 1))
        for h in range(2):
            yh = y[:, h * LANES:(h + 1) * LANES]
            _twiddle_store(ys_ref, c + h, yh[0:k1n], yh[k1n:], tr, ti)
        return carry

    lax.fori_loop(0, cf, stage1, 0, unroll=4)
    ysum = ys_ref[0:cf].reshape(cf * k1n, 2 * LANES)
    ydif = ys_ref[cf:].reshape(cf * k1n, 2 * LANES)
    o_ref[:, :, 0:LANES] = _dot(ysum, w2s_ref[...].astype(BF16)).reshape(cf, k1n, LANES)
    o_ref[:, :, LANES:] = _dot(ydif, w2d_ref[...].astype(BF16)).reshape(cf, k1n, LANES)


def _hfilter_latent(l, w1t, b1, freq, w2t, b2, w3t, cf=64):
    a_len = l // LANES
    k1n = 2 * a_len
    zt, t01, deltas = (jnp.asarray(m) for m in _const_filter_features(l))
    _, w1f, tw, _, w2s, w2d, _, _ = (jnp.asarray(m) for m in _const_hyena(a_len))
    orders = w3t.shape[1]
    return pl.pallas_call(
        _hfilter_kernel,
        grid=(orders, HY_CH // cf),
        in_specs=[_resident(zt.shape), _resident(t01.shape),
                  pl.BlockSpec((cf, 1), lambda o, c: (c, 0)),
                  _resident(w1t.shape), _resident(b1.shape), _resident(freq.shape),
                  _resident(w2t.shape), _resident(b2.shape),
                  pl.BlockSpec((2, None, cf, HY_HID), lambda o, c: (0, o, c, 0)),
                  _resident(w1f.shape), _resident(tw.shape), _resident(w2s.shape), _resident(w2d.shape)],
        out_specs=pl.BlockSpec((None, cf, k1n, 2 * LANES), lambda o, c: (o, c, 0, 0)),
        out_shape=jax.ShapeDtypeStruct((orders, HY_CH, k1n, 2 * LANES), F32),
        scratch_shapes=[pltpu.VMEM((HY_HID, l), BF16),
                        pltpu.VMEM((2 * cf, a_len, LANES), BF16),
                        pltpu.VMEM((2 * cf, k1n, 2 * LANES), BF16)],
        compiler_params=_params("arbitrary", "arbitrary"),
        name="hyena_filter_latent",
    )(zt, t01, deltas, w1t, b1, freq, w2t, b2, w3t, w1f, tw, w2s, w2d)


def _hfilter_ctx_kernel(zt_ref, t_ref, dl_ref, w1t_ref, b1_ref, fr_ref, w2t_ref, b2_ref, w3t_ref,
                        cf_ref, sf_ref, o_ref):
    l = zt_ref.shape[1]
    h2 = _filter_mlp(zt_ref[...], w1t_ref[...], b1_ref[...], fr_ref[...], w2t_ref[...], b2_ref[...])
    ch = w3t_ref.shape[1]
    w3 = w3t_ref[...].reshape(2 * ch, HY_HID)
    dl = dl_ref[...]
    hv = _dot_hi(w3, h2) * jnp.exp(-(jnp.concatenate([dl, dl], 0) * t_ref[...]))
    tot = jnp.sum(jnp.abs(hv), axis=1, keepdims=True)
    inv = 1.0 / (tot[0:ch] + tot[ch:])
    f = hv[0:ch] * inv
    lane = lax.broadcasted_iota(jnp.int32, (ch, l), 1)
    bk = jnp.where(lane == 0, 0.0, hv[ch:] * inv)
    o_ref[:, 0:2 * l] = _dot((f + bk).astype(BF16), cf_ref[...].astype(BF16))
    o_ref[:, 2 * l:] = _dot((f - bk).astype(BF16), sf_ref[...].astype(BF16))


def _hfilter_ctx(l, w1t, b1, freq, w2t, b2, w3t):
    zt, t01, deltas = (jnp.asarray(m) for m in _const_filter_features(l))
    cfm, nsf = (jnp.asarray(m) for m in _const_dense(l)[2:4])
    orders = w3t.shape[1]
    return pl.pallas_call(
        _hfilter_ctx_kernel,
        grid=(orders,),
        in_specs=[_resident(zt.shape), _resident(t01.shape), _resident(deltas.shape),
                  _resident(w1t.shape), _resident(b1.shape), _resident(freq.shape),
                  _resident(w2t.shape), _resident(b2.shape),
                  pl.BlockSpec((2, None, HY_CH, HY_HID), lambda o: (0, o, 0, 0)),
                  _resident(cfm.shape), _resident(nsf.shape)],
        out_specs=pl.BlockSpec((None, HY_CH, 4 * l), lambda o: (o, 0, 0)),
        out_shape=jax.ShapeDtypeStruct((orders, HY_CH, 4 * l), F32),
        compiler_params=_params("parallel"),
        name="hyena_filter_ctx",
    )(zt, t01, deltas, w1t, b1, freq, w2t, b2, w3t, cfm, nsf)


def _short_conv(x, cw, cb):
    l = x.shape[1]
    lane = lax.broadcasted_iota(jnp.int32, x.shape, 1)
    prev = jnp.where(lane == 0, 0.0, pltpu.roll(x, 1, 1))
    nxt = jnp.where(lane == l - 1, 0.0, pltpu.roll(x, l - 1, 1))
    return cw[:, 0:1] * prev + cw[:, 1:2] * x + cw[:, 2:3] * nxt + cb


def _short_conv_tile(x, w0, w1, w2, bias):
    a_len = x.shape[0]
    lane = lax.broadcasted_iota(jnp.int32, x.shape, 1)
    row = lax.broadcasted_iota(jnp.int32, x.shape, 0)
    xl = pltpu.roll(x, 1, 1)
    prev = jnp.where(lane == 0, jnp.where(row == 0, 0.0, pltpu.roll(xl, 1, 0)), xl)
    xr = pltpu.roll(x, LANES - 1, 1)
    nxt = jnp.where(lane == LANES - 1, jnp.where(row == a_len - 1, 0.0, pltpu.roll(xr, a_len - 1, 0)), xr)
    return w0 * prev + w1 * x + w2 * nxt + bias


def _hconv_kernel(cw_ref, cb_ref, sk_ref, x1_ref, x2_ref, v_ref, hf_ref,
                  w1_ref, tw_ref, w2_ref, w2i_ref, v1_ref, o_ref, g1_ref, g2_ref, vb_ref, ys_ref, us_ref, *, group):
    nb, ct, a_len, _ = vb_ref.shape
    k1n = 2 * a_len
    nch3 = cb_ref.shape[0]
    c0 = pl.program_id(0) * ct
    w1 = w1_ref[...].astype(BF16)
    w2 = w2_ref[...].astype(BF16)
    w2i = w2i_ref[...].astype(BF16)
    v1 = v1_ref[...].astype(BF16)
    tr, ti = tw_ref[0], tw_ref[1]

    for src, dst in ((x1_ref, g1_ref), (x2_ref, g2_ref), (v_ref, vb_ref)):
        for bi in range(nb):
            dst[bi] = src[bi].reshape(ct, a_len, LANES)

    def sconv(x, ch):
        return _short_conv_tile(x, cw_ref[ch], cw_ref[nch3 + ch], cw_ref[2 * nch3 + ch], cb_ref[ch])

    def prep(c, carry):
        for bi in range(nb):
            vb_ref[bi, c] = sconv(vb_ref[bi, c], 2 * HY_CH + c0 + c)
        return carry

    lax.fori_loop(0, ct, prep, 0, unroll=4)

    for order, gate_ref in enumerate((g1_ref, g2_ref)):
        def stage1(p, carry):
            c = 2 * p
            x = jnp.concatenate([jnp.concatenate([vb_ref[0, c + h], vb_ref[1, c + h]], 0) for h in range(2)], 1)
            y = _dot(w1, x.astype(BF16))
            for h in range(2):
                yh = y[:, h * LANES:(h + 1) * LANES]
                _twiddle_store(ys_ref, c + h, yh[0:k1n], yh[k1n:], tr, ti)
            return carry

        lax.fori_loop(0, ct // 2, stage1, 0, unroll=4)

        def middle(g, carry):
            cs = pl.ds(g * group, group)
            z = _dot(ys_ref[cs].reshape(group * k1n, 2 * LANES), w2)
            h = hf_ref[order, cs].reshape(group * k1n, 2 * LANES)
            zr, zi = z[:, 0:LANES], z[:, LANES:]
            hr, hi = h[:, 0:LANES], h[:, LANES:]
            p = jnp.concatenate([zr * hr - zi * hi, zr * hi + zi * hr], 1).astype(BF16)
            u = _dot(p, w2i).reshape(group, k1n, 2 * LANES)
            ur, ui = u[:, :, 0:LANES], u[:, :, LANES:]
            us_ref[cs, :, 0:LANES] = (ur * tr + ui * ti).astype(BF16)
            us_ref[cs, :, LANES:] = (ui * tr - ur * ti).astype(BF16)
            return carry

        lax.fori_loop(0, ct // group, middle, 0)

        def stage1_inv(c, carry):
            q = _dot(v1, us_ref[c])
            conv = (q[0:a_len, 0:LANES] - q[a_len:, LANES:], q[a_len:, 0:LANES] + q[0:a_len, LANES:])
            skip = sk_ref[order * HY_CH + c0 + c]
            for bi in range(nb):
                gate = sconv(gate_ref[bi, c], order * HY_CH + c0 + c)
                vb_ref[bi, c] = gate * (conv[bi] + vb_ref[bi, c] * skip)
            return carry

        lax.fori_loop(0, ct, stage1_inv, 0, unroll=4)

    for bi in range(nb):
        o_ref[bi] = vb_ref[bi].reshape(ct, a_len * LANES)


def _hconv_latent(ph, hf, conv_w, conv_b, skip, ct=32, group=4):
    b, nch3, l = ph.shape
    assert b == 2, "the two batch entries ride as real/imaginary parts"
    a_len = l // LANES
    k1n = 2 * a_len
    w1, _, tw, w2, _, _, w2i, v1 = (jnp.asarray(m) for m in _const_hyena(a_len))
    nblk = HY_CH // ct
    smem = pl.BlockSpec(memory_space=pltpu.SMEM)
    part = lambda off: pl.BlockSpec((b, ct, l), lambda c: (0, c + off * nblk, 0))
    scratch = pltpu.VMEM((b, ct, a_len, LANES), F32)
    return pl.pallas_call(
        functools.partial(_hconv_kernel, group=group),
        grid=(nblk,),
        in_specs=[smem, smem, smem, part(0), part(1), part(2),
                  pl.BlockSpec((2, ct, k1n, 2 * LANES), lambda c: (0, c, 0, 0)),
                  _resident(w1.shape), _resident(tw.shape), _resident(w2.shape),
                  _resident(w2i.shape), _resident(v1.shape)],
        out_specs=pl.BlockSpec((b, ct, l), lambda c: (0, c, 0)),
        out_shape=jax.ShapeDtypeStruct((b, HY_CH, l), F32),
        scratch_shapes=[scratch, scratch, scratch,
                        pltpu.VMEM((ct, k1n, 2 * LANES), BF16),
                        pltpu.VMEM((ct, k1n, 2 * LANES), BF16)],
        compiler_params=_params("parallel"),
        name="hyena_conv_latent",
    )(conv_w.reshape(-1), conv_b, skip.reshape(-1), ph, ph, ph, hf, w1, tw, w2, w2i, v1)


def _hconv_ctx_kernel(cw_ref, cb_ref, sk_ref, p_ref, hf_ref, fwd_ref, inv_ref, o_ref):
    nb, _, l = p_ref.shape
    ch = HY_CH
    z = [_short_conv(p_ref[bi], cw_ref[...], cb_ref[...]) for bi in range(nb)]
    fwd = fwd_ref[...].astype(BF16)
    inv = inv_ref[...].astype(BF16)
    cur = [zz[2 * ch:] for zz in z]
    for order in range(2):
        spec = _dot(jnp.concatenate(cur, 1).astype(BF16), fwd)
        h = hf_ref[order]
        zr, zi, hr, hi = spec[:, 0:2 * l], spec[:, 2 * l:], h[:, 0:2 * l], h[:, 2 * l:]
        prod = jnp.concatenate([zr * hr - zi * hi, zr * hi + zi * hr], 1).astype(BF16)
        y = _dot(prod, inv)
        sk = sk_ref[:, order:order + 1]
        cur = [z[bi][order * ch:(order + 1) * ch] * (y[:, bi * l:(bi + 1) * l] + cur[bi] * sk) for bi in range(nb)]
    for bi in range(nb):
        o_ref[bi] = cur[bi]


def _hconv_ctx(ph, hf, conv_w, conv_b, skip):
    b, nch3, l = ph.shape
    assert b == 2
    fwd, inv = (jnp.asarray(m) for m in _const_dense(l)[4:6])
    cw = jnp.pad(conv_w.T, ((0, 0), (0, 5)))
    return pl.pallas_call(
        _hconv_ctx_kernel,
        in_specs=[pl.BlockSpec(cw.shape, lambda: (0, 0)),
                  pl.BlockSpec((nch3, 1), lambda: (0, 0)),
                  pl.BlockSpec((HY_CH, 2), lambda: (0, 0)),
                  pl.BlockSpec(ph.shape, lambda: (0, 0, 0)),
                  pl.BlockSpec(hf.shape, lambda: (0, 0, 0)),
                  pl.BlockSpec(fwd.shape, lambda: (0, 0)),
                  pl.BlockSpec(inv.shape, lambda: (0, 0))],
        out_specs=pl.BlockSpec((b, HY_CH, l), lambda: (0, 0, 0)),
        out_shape=jax.ShapeDtypeStruct((b, HY_CH, l), F32),
        compiler_params=pltpu.CompilerParams(vmem_limit_bytes=VMEM_LIMIT_BYTES),
        name="hyena_conv_ctx",
    )(cw, conv_b.reshape(-1, 1), skip.T, ph, hf, fwd, inv)


def _mix_ffn_kernel(*refs, n_mix, final, ff_chunk):
    d = D_MODEL
    h_ref, y_refs = refs[0], refs[1:1 + n_mix]
    wo_ref, mod_ref, g2_ref, wg_ref, wu_ref, wd_ref = refs[1 + n_mix:7 + n_mix]
    rest = refs[7 + n_mix:]
    y, row = None, 0
    for y_ref in y_refs:
        rows = y_ref.shape[0]
        part = lax.dot_general(y_ref[...].astype(BF16), wo_ref[row:row + rows, :], _TN, preferred_element_type=F32)
        y = part if y is None else y + part
        row += rows
    if final:
        fg_ref, o_ref = rest
    else:
        (o_ref,) = rest
    h1 = h_ref[...] + mod_ref[:, 2 * d:3 * d] * y
    u = _norm_mod(h1, g2_ref[...], mod_ref[:, 3 * d:4 * d], mod_ref[:, 4 * d:5 * d]).astype(BF16)
    acc = jnp.zeros(h1.shape, F32)
    for j in range(D_FF // ff_chunk):
        cols = slice(j * ff_chunk, (j + 1) * ff_chunk)
        act = (_silu(_dot(u, wg_ref[:, cols])) * _dot(u, wu_ref[:, cols])).astype(BF16)
        acc = acc + _dot(act, wd_ref[cols, :])
    h2 = h1 + mod_ref[:, 5 * d:6 * d] * acc
    if final:
        h2 = _rms(h2, fg_ref[...])
    o_ref[...] = h2


def _mix_ffn(h, ys, wo, mod3, g2, wg, wu, wd, final_g=None, tl=512, ff_chunk=256):
    b, l, d = h.shape
    tl = min(tl, l)
    tok = pl.BlockSpec((None, tl, d), lambda i, t: (i, t, 0))
    y_specs = [pl.BlockSpec((None, y.shape[1], tl), lambda i, t: (i, 0, t)) for y in ys]
    in_specs = ([tok] + y_specs + [_resident(wo.shape), pl.BlockSpec((None, 1, 6 * d), lambda i, t: (i, 0, 0)),
                                    pl.BlockSpec((1, d), lambda i, t: (0, 0)),
                                    _resident(wg.shape), _resident(wu.shape), _resident(wd.shape)])
    args = [h, *ys, wo, mod3, g2, wg, wu, wd]
    if final_g is not None:
        in_specs.append(pl.BlockSpec((1, d), lambda i, t: (0, 0)))
        args.append(final_g)
    return pl.pallas_call(
        functools.partial(_mix_ffn_kernel, n_mix=len(ys), final=final_g is not None, ff_chunk=ff_chunk),
        grid=(b, l // tl),
        in_specs=in_specs,
        out_specs=tok,
        out_shape=jax.ShapeDtypeStruct((b, l, d), F32),
        compiler_params=_params("parallel", "parallel"),
        name="mix_ffn_final" if final_g is not None else "mix_ffn",
    )(*args)


def _qkv_kernel(x_ref, g_ref, mod_ref, w_ref, q_ref, k_ref, v_ref):
    d = D_MODEL
    u = _norm_mod(x_ref[...], g_ref[...], mod_ref[:, 0:d], mod_ref[:, d:2 * d]).astype(BF16)
    q_ref[...] = (_dot(u, w_ref[:, 0:d]) * (NA_HEAD_DIM ** -0.5)).astype(BF16)
    k_ref[...] = _dot(u, w_ref[:, d:2 * d]).astype(BF16)
    v_ref[...] = _dot(u, w_ref[:, 2 * d:]).astype(BF16)


def _qkv(h, g, mod3, w, tl=1024):
    b, l, d = h.shape
    tl = min(tl, l)
    tok = pl.BlockSpec((None, tl, d), lambda i, t: (i, t, 0))
    return pl.pallas_call(
        _qkv_kernel,
        grid=(b, l // tl),
        in_specs=[tok, pl.BlockSpec((1, d), lambda i, t: (0, 0)),
                  pl.BlockSpec((None, 1, 6 * d), lambda i, t: (i, 0, 0)), _resident(w.shape)],
        out_specs=[tok, tok, tok],
        out_shape=[jax.ShapeDtypeStruct((b, l, d), BF16)] * 3,
        compiler_params=_params("parallel", "parallel"),
        name="qkv_proj",
    )(h, g, mod3, w)


NA_QROWS = 4
NA_KROWS = 12
NA_PATTERNS = 3


def _na_key_start(r0, grid_rows):
    return jnp.minimum(jnp.clip(r0 - NA_KH // 2, 0, grid_rows - NA_KH), grid_rows - NA_KROWS)


def _na_pattern(pat, i, j):
    if pat == 0:
        return j - i + NA_KH - 1, j < NA_KH
    if pat == 1:
        return j - i + NA_KH // 2 - 1, i <= j < i + NA_KH
    return j - i - 1, j >= NA_KROWS - NA_KH


def _bias_blocks_kernel(rpb_ref, o_ref):
    pair = pl.program_id(0)
    ndr, ndc = 2 * NA_KH - 1, 2 * NA_KW - 1
    kcol = lax.broadcasted_iota(jnp.int32, (GRID_W, LANES), 0)
    lane = lax.broadcasted_iota(jnp.int32, (GRID_W, LANES), 1)
    q = lane % GRID_W
    upper = lane >= GRID_W
    wstart = jnp.clip(q - NA_KW // 2, 0, GRID_W - NA_KW)
    ok = (kcol >= wstart) & (kcol < wstart + NA_KW)
    dc = kcol - q + (NA_KW - 1)
    neg = jnp.full((GRID_W, LANES), NEG_BIG, F32)
    for e in range(2):
        base = (2 * pair + e) * ndr * ndc
        tiles = []
        for dr in range(ndr):
            acc = jnp.zeros((GRID_W, LANES), F32)
            for dd in range(ndc):
                acc = jnp.where(dc == dd, rpb_ref[base + dr * ndc + dd], acc)
            tiles.append(jnp.where(ok, acc, NEG_BIG))
        for pat in range(NA_PATTERNS):
            for j in range(NA_KROWS):
                for p in range(NA_QROWS // 2):
                    halves = []
                    for i in (2 * p, 2 * p + 1):
                        dr, visible = _na_pattern(pat, i, j)
                        halves.append(tiles[dr] if visible else neg)
                    o_ref[e, pat, j * GRID_W:(j + 1) * GRID_W, p * LANES:(p + 1) * LANES] = jnp.where(
                        upper, halves[1], halves[0])


def _bias_blocks(rpb):
    heads = rpb.shape[0]
    nq, nk = NA_QROWS * GRID_W, NA_KROWS * GRID_W
    return pl.pallas_call(
        _bias_blocks_kernel,
        grid=(heads // 2,),
        in_specs=[pl.BlockSpec(memory_space=pltpu.SMEM)],
        out_specs=pl.BlockSpec((None, 2, NA_PATTERNS, nk, nq), lambda p: (p, 0, 0, 0, 0)),
        out_shape=jax.ShapeDtypeStruct((heads // 2, 2, NA_PATTERNS, nk, nq), F32),
        compiler_params=_params("parallel"),
        name="na_bias_blocks",
    )(rpb.reshape(-1))


def _na_kernel(q_ref, k_ref, v_ref, kc_ref, vc_ref, bias_ref, o_ref, *, blocks_per_step, grid_rows):
    nq, nk = NA_QROWS * GRID_W, NA_KROWS * GRID_W
    lane = lax.broadcasted_iota(jnp.int32, (nq, LANES), 1)
    row = lax.broadcasted_iota(jnp.int32, (LANES, nq), 0)
    kc = kc_ref[...]
    vc = vc_ref[...]
    step = pl.program_id(2)
    for i in range(blocks_per_step):
        r0 = (step * blocks_per_step + i) * NA_QROWS
        pat = jnp.where(r0 == 0, 0, jnp.where(r0 == grid_rows - NA_QROWS, 2, 1))
        koff = pl.multiple_of(_na_key_start(r0, grid_rows) * GRID_W, GRID_W)
        q = q_ref[i * nq:(i + 1) * nq, :]
        ku = k_ref[pl.ds(koff, nk), :]
        vt = jnp.concatenate([v_ref[pl.ds(koff, nk), :], vc], 0).T
        vt = jnp.concatenate([vt, jnp.ones((16, vt.shape[1]), vt.dtype)], 0)
        outs = []
        for e in range(2):
            head = (lane >= NA_HEAD_DIM) if e else (lane < NA_HEAD_DIM)
            qh = jnp.where(head, q, jnp.zeros_like(q))
            sl = lax.dot_general(ku, qh, _NT, preferred_element_type=F32) + bias_ref[e, pat]
            sc = lax.dot_general(kc, qh, _NT, preferred_element_type=F32)
            s = jnp.concatenate([sl, sc], 0)
            p = jnp.exp((s - jnp.max(s, axis=0, keepdims=True)).astype(BF16))
            o = _dot(vt, p)
            outs.append(o[0:LANES] / o[LANES:LANES + 1])
        o_ref[:, i * nq:(i + 1) * nq] = jnp.where(row < NA_HEAD_DIM, outs[0], outs[1]).astype(o_ref.dtype)


def _na(q, k, v, kc, vc, bias_blocks, blocks_per_step=4):
    b, l, d = q.shape
    lc = kc.shape[1]
    grid_rows = l // GRID_W
    assert grid_rows % (NA_QROWS * blocks_per_step) == 0 and grid_rows >= NA_KROWS
    npairs = d // LANES
    tq = blocks_per_step * NA_QROWS * GRID_W
    return pl.pallas_call(
        functools.partial(_na_kernel, blocks_per_step=blocks_per_step, grid_rows=grid_rows),
        grid=(npairs, b, l // tq),
        in_specs=[pl.BlockSpec((None, tq, LANES), lambda p, i, t: (i, t, p)),
                  pl.BlockSpec((None, l, LANES), lambda p, i, t: (i, 0, p)),
                  pl.BlockSpec((None, l, LANES), lambda p, i, t: (i, 0, p)),
                  pl.BlockSpec((None, lc, LANES), lambda p, i, t: (i, 0, p)),
                  pl.BlockSpec((None, lc, LANES), lambda p, i, t: (i, 0, p)),
                  pl.BlockSpec((None,) + bias_blocks.shape[1:], lambda p, i, t: (p, 0, 0, 0, 0))],
        out_specs=pl.BlockSpec((None, LANES, tq), lambda p, i, t: (i, p, t)),
        out_shape=jax.ShapeDtypeStruct((b, d, l), BF16),
        compiler_params=_params("parallel", "parallel", "parallel"),
        name="neighbourhood_attention",
    )(q, k, v, kc, vc, bias_blocks)


def kernel(x, c, ctx, c_ctx, ada_w, ada_b, norm1_g, norm2_g, ffn_w_gate, ffn_w_up, ffn_w_down, ab_w_in, ab_w_out, hy_conv_w, hy_conv_b, hy_f_w1, hy_f_b1, hy_f_freq, hy_f_w2, hy_f_b2, hy_f_w3, hy_bias, na_w_qkv, na_w_out, na_rpb, final_g):
    b, l, d = x.shape
    lc = ctx.shape[1]
    assert b == 2 and d == D_MODEL and l % 1024 == 0 and ada_w.shape[0] == 2

    cond8 = jnp.zeros((8, d), F32).at[0:b].set(c).at[b].set(c_ctx)
    mod = _ada(cond8, ada_w, ada_b)
    mod_lat = [mod[i, 0:b].reshape(b, 1, 6 * d) for i in range(2)]
    mod_ctx = [jnp.broadcast_to(mod[i, b:b + 1], (b, 6 * d)).reshape(b, 1, 6 * d) for i in range(2)]
    n1 = [norm1_g[i].reshape(1, d) for i in range(2)]
    n2 = [norm2_g[i].reshape(1, d) for i in range(2)]
    wg = [ffn_w_gate[i].astype(BF16) for i in range(2)]
    wu = [ffn_w_up[i].astype(BF16) for i in range(2)]
    wd = [ffn_w_down[i].astype(BF16) for i in range(2)]

    w_in_t = ab_w_in[0].T.astype(BF16)
    w_out = ab_w_out[0].astype(BF16)
    w1t = jnp.pad(hy_f_w1[0].T, ((0, 0), (0, LANES - HY_EMB)))
    col = lambda vec: vec.reshape(-1, 1)
    mlp = (w1t, col(hy_f_b1[0]), col(hy_f_freq[0]), hy_f_w2[0].T, col(hy_f_b2[0]),
           hy_f_w3[0].T.reshape(2, 2, HY_CH, HY_HID))

    zf, ph = _inproj(x, n1[0], mod_lat[0], w_in_t)
    y_f = _fourier_latent(zf)
    hf = _hfilter_latent(l, *mlp)
    y_h = _hconv_latent(ph, hf, hy_conv_w[0], hy_conv_b[0], hy_bias[0]).reshape(b, HY_CH, l)
    h_lat = _mix_ffn(x, (y_f, y_h), w_out, mod_lat[0], n2[0], wg[0], wu[0], wd[0])

    zfc, phc = _inproj(ctx, n1[0], mod_ctx[0], w_in_t)
    y_fc = _fourier_ctx(zfc)
    hfc = _hfilter_ctx(lc, *mlp)
    y_hc = _hconv_ctx(phc, hfc, hy_conv_w[0], hy_conv_b[0], hy_bias[0])
    h_ctx = _mix_ffn(ctx, (y_fc, y_hc), w_out, mod_ctx[0], n2[0], wg[0], wu[0], wd[0])

    w_qkv = na_w_qkv[0].astype(BF16)
    q, k, v = _qkv(h_lat, n1[1], mod_lat[1], w_qkv)
    _, kc, vc = _qkv(h_ctx, n1[1], mod_ctx[1], w_qkv)
    attn = _na(q, k, v, kc, vc, _bias_blocks(na_rpb[0]))
    return _mix_ffn(h_lat, (attn,), na_w_out[0].astype(BF16), mod_lat[1], n2[1], wg[1], wu[1], wd[1],
                    final_g=final_g.reshape(1, d))
```

```python
import functools
import math

import numpy as np
import jax
import jax.numpy as jnp
from jax import lax
from jax.experimental import pallas as pl
from jax.experimental.pallas import tpu as pltpu

F32 = jnp.float32
BF16 = jnp.bfloat16

D_MODEL = 1024
FOURIER_CH = 512
FOURIER_GROUPS = 4
GROUP_DIM = 128
HY_CH = 512
HY_BANDS = 16
HY_EMB = 33
HY_HID = 64
HY_FAST_DECAY = 0.3
HY_SLOW_DECAY = 1.5
HY_DECAY_TARGET = 1e-2
D_FF = 2816
NA_HEADS = 16
NA_HEAD_DIM = 64
NA_KH = 8
NA_KW = 16
GRID_W = 64
RMS_EPS = 1e-6
NEG_BIG = -1e30

LANES = 128
VMEM_LIMIT_BYTES = 56 * 1024 * 1024

_NT = (((1,), (1,)), ((), ()))
_TN = (((0,), (0,)), ((), ()))


def _params(*sem):
    return pltpu.CompilerParams(dimension_semantics=sem, vmem_limit_bytes=VMEM_LIMIT_BYTES)


def _dot(a, b):
    return jnp.dot(a, b, preferred_element_type=F32)


def _dot_hi(a, b):
    return jnp.dot(a, b, preferred_element_type=F32, precision=lax.Precision.HIGHEST)


def _resident(shape):
    nd = len(shape)
    return pl.BlockSpec(shape, lambda *_: (0,) * nd, pipeline_mode=pl.Buffered(1))


def _rms(x, g):
    ms = jnp.mean(x * x, axis=-1, keepdims=True)
    return x * lax.rsqrt(ms + RMS_EPS) * g


def _norm_mod(x, g, shift, scale):
    return _rms(x, g) * (1.0 + scale) + shift


def _silu(x):
    return x * jax.nn.sigmoid(x)


def _cs(num, den):
    ang = 2.0 * np.pi * (np.asarray(num, np.float64) % den) / den
    return np.cos(ang), np.sin(ang)


@functools.lru_cache(maxsize=None)
def _const_channel_dft():
    c, s = _cs(np.outer(np.arange(GROUP_DIM), np.arange(GROUP_DIM)), GROUP_DIM)
    return (np.concatenate([c, -s], 0) / math.sqrt(GROUP_DIM)).astype(np.float32)


@functools.lru_cache(maxsize=None)
def _const_fourier(a_len):
    n = a_len * LANES
    c, s = _cs(np.outer(np.arange(a_len), np.arange(a_len)), a_len)
    w1 = np.block([[c, s], [-s, c]])
    tc, ts = _cs(np.outer(np.arange(a_len), np.arange(LANES)), n)
    tw = np.stack([tc, -ts])
    c2, s2 = _cs(np.outer(np.arange(LANES), np.arange(LANES)), LANES)
    w2 = np.concatenate([c2, s2], 0) / math.sqrt(n)
    r = LANES // a_len
    perm = np.array([r * m + h for h in range(r) for m in range(a_len)])
    w2 = w2[:, perm]
    return w1.astype(np.float32), tw.astype(np.float32), w2.astype(np.float32)


@functools.lru_cache(maxsize=None)
def _const_hyena(a_len):
    k1n = 2 * a_len
    n = k1n * LANES
    c, s = _cs(np.outer(np.arange(k1n), np.arange(a_len)), k1n)
    w1 = np.block([[c, s], [-s, c]])
    w1f = np.concatenate([c, -s], 0)
    tc, ts = _cs(np.outer(np.arange(k1n), np.arange(LANES)), n)
    tw = np.stack([tc, -ts])
    c2, s2 = _cs(np.outer(np.arange(LANES), np.arange(LANES)), LANES)
    w2 = np.block([[c2, -s2], [s2, c2]])
    w2s = np.concatenate([c2, s2], 0)
    w2d = np.concatenate([-s2, c2], 0)
    w2inv = np.block([[c2, s2], [-s2, c2]])
    v1 = np.concatenate([c.T, s.T], 0) / n
    f = lambda m: m.astype(np.float32)
    return f(w1), f(w1f), f(tw), f(w2), f(w2s), f(w2d), f(w2inv), f(v1)


@functools.lru_cache(maxsize=None)
def _const_dense(l_len):
    n = 2 * l_len
    cl, sl = _cs(np.outer(np.arange(l_len), np.arange(l_len)), l_len)
    four_c, four_s = cl / math.sqrt(l_len), sl / math.sqrt(l_len)
    cf, sf = _cs(np.outer(np.arange(l_len), np.arange(n)), n)
    fwd = np.block([[cf, -sf], [sf, cf]])
    inv = np.block([[cf.T, sf.T], [-sf.T, cf.T]]) / n
    f = lambda m: m.astype(np.float32)
    return f(four_c), f(four_s), f(cf), f(-sf), f(fwd), f(inv)


@functools.lru_cache(maxsize=None)
def _const_filter_features(l_len):
    t01 = np.linspace(0.0, 1.0, l_len)
    bands = np.linspace(1e-4, HY_BANDS - 1, HY_BANDS)
    ang = (2.0 * np.pi / l_len) * np.arange(l_len)[:, None] * bands[None, :]
    z = np.concatenate([t01[:, None], np.cos(ang), -np.sin(ang)], -1)
    zt = np.zeros((LANES, l_len), np.float64)
    zt[:HY_EMB] = z.T
    max_decay = math.log(HY_DECAY_TARGET) / HY_FAST_DECAY
    min_decay = math.log(HY_DECAY_TARGET) / HY_SLOW_DECAY
    deltas = np.abs(np.linspace(min_decay, max_decay, HY_CH))
    return zt.astype(np.float32), t01[None, :].astype(np.float32), deltas[:, None].astype(np.float32)


def _chunk_time(m):
    rows, l_len = m.shape
    return np.ascontiguousarray(m.reshape(rows, l_len // LANES, LANES).transpose(1, 0, 2))


def _ada_kernel(c_ref, w_ref, b_ref, o_ref):
    s = _silu(c_ref[...]).astype(BF16)
    o_ref[...] = _dot(s, w_ref[...].astype(BF16)) + b_ref[...]


def _ada(cond8, ada_w, ada_b):
    depth, d, n6 = ada_w.shape
    tn = 1536
    return pl.pallas_call(
        _ada_kernel,
        grid=(depth, n6 // tn),
        in_specs=[pl.BlockSpec((8, d), lambda i, j: (0, 0)),
                  pl.BlockSpec((None, d, tn), lambda i, j: (i, 0, j)),
                  pl.BlockSpec((None, 1, tn), lambda i, j: (i, 0, j))],
        out_specs=pl.BlockSpec((None, 8, tn), lambda i, j: (i, 0, j)),
        out_shape=jax.ShapeDtypeStruct((depth, 8, n6), F32),
        compiler_params=_params("parallel", "parallel"),
        name="ada_mod",
    )(cond8, ada_w, ada_b.reshape(depth, 1, n6))


def _inproj_kernel(x_ref, g_ref, mod_ref, wt_ref, cdft_ref, zf_ref, ph_ref):
    d = D_MODEL
    u = _norm_mod(x_ref[...], g_ref[...], mod_ref[:, 0:d], mod_ref[:, d:2 * d]).astype(BF16)
    cdft = cdft_ref[...].astype(BF16)
    for g in range(FOURIER_GROUPS):
        rows = slice(g * GROUP_DIM, (g + 1) * GROUP_DIM)
        pt = lax.dot_general(wt_ref[rows, :], u, _NT, preferred_element_type=F32)
        z = _dot(cdft, pt.astype(BF16))
        zf_ref[0, rows, :] = z[0:GROUP_DIM]
        zf_ref[1, rows, :] = z[GROUP_DIM:]
    rb = 512
    for r0 in range(FOURIER_CH, wt_ref.shape[0], rb):
        ph_ref[r0 - FOURIER_CH:r0 - FOURIER_CH + rb, :] = lax.dot_general(
            wt_ref[r0:r0 + rb, :], u, _NT, preferred_element_type=F32)


def _inproj(x, g, mod3, w_in_t, tl=1024):
    b, l, d = x.shape
    tl = min(tl, l)
    nh = w_in_t.shape[0] - FOURIER_CH
    return pl.pallas_call(
        _inproj_kernel,
        grid=(b, l // tl),
        in_specs=[pl.BlockSpec((None, tl, d), lambda i, t: (i, t, 0)),
                  pl.BlockSpec((1, d), lambda i, t: (0, 0)),
                  pl.BlockSpec((None, 1, 6 * d), lambda i, t: (i, 0, 0)),
                  _resident(w_in_t.shape),
                  _resident((2 * GROUP_DIM, GROUP_DIM))],
        out_specs=[pl.BlockSpec((None, 2, FOURIER_CH, tl), lambda i, t: (i, 0, 0, t)),
                   pl.BlockSpec((None, nh, tl), lambda i, t: (i, 0, t))],
        out_shape=[jax.ShapeDtypeStruct((b, 2, FOURIER_CH, l), F32),
                   jax.ShapeDtypeStruct((b, nh, l), F32)],
        compiler_params=_params("parallel", "parallel"),
        name="inproj",
    )(x, g, mod3, w_in_t, jnp.asarray(_const_channel_dft()))


def _twiddle_store(ys_ref, c, yr, yi, tr, ti):
    ys_ref[c, :, 0:LANES] = (yr * tr - yi * ti).astype(ys_ref.dtype)
    ys_ref[c, :, LANES:] = (yr * ti + yi * tr).astype(ys_ref.dtype)


def _fourier_kernel(z_ref, w1_ref, tw_ref, w2_ref, o_ref, zs_ref, ys_ref):
    _, ct, a_len, _ = zs_ref.shape
    w1 = w1_ref[...].astype(BF16)
    tr, ti = tw_ref[0], tw_ref[1]
    for ri in range(2):
        zs_ref[ri] = z_ref[ri].reshape(ct, a_len, LANES)

    def stage1(p, carry):
        c = 2 * p
        x = jnp.concatenate([jnp.concatenate([zs_ref[0, c + h], zs_ref[1, c + h]], 0) for h in range(2)], 1)
        y = _dot(w1, x.astype(BF16))
        for h in range(2):
            yh = y[:, h * LANES:(h + 1) * LANES]
            _twiddle_store(ys_ref, c + h, yh[0:a_len], yh[a_len:], tr, ti)
        return carry

    lax.fori_loop(0, ct // 2, stage1, 0, unroll=8)
    y = ys_ref[...].reshape(ct * a_len, 2 * LANES)
    zr = _dot(y, w2_ref[...].astype(BF16))
    tiles = []
    for c in range(ct):
        t = zr[c * a_len:(c + 1) * a_len].T
        tiles.append(jnp.concatenate([t[h * a_len:(h + 1) * a_len] for h in range(LANES // a_len)], 1))
    o_ref[...] = jnp.stack(tiles, 0).reshape(ct, a_len * LANES)


def _fourier_latent(zf, ct=32):
    b, _, ch, l = zf.shape
    a_len = l // LANES
    w1, tw, w2 = (jnp.asarray(m) for m in _const_fourier(a_len))
    return pl.pallas_call(
        _fourier_kernel,
        grid=(b, ch // ct),
        in_specs=[pl.BlockSpec((None, 2, ct, l), lambda i, c: (i, 0, c, 0)),
                  _resident(w1.shape), _resident(tw.shape), _resident(w2.shape)],
        out_specs=pl.BlockSpec((None, ct, l), lambda i, c: (i, c, 0)),
        out_shape=jax.ShapeDtypeStruct((b, ch, l), F32),
        scratch_shapes=[pltpu.VMEM((2, ct, a_len, LANES), F32),
                        pltpu.VMEM((ct, a_len, 2 * LANES), BF16)],
        compiler_params=_params("parallel", "parallel"),
        name="fourier_latent",
    )(zf, w1, tw, w2)


def _fourier_ctx_kernel(z_ref, c_ref, s_ref, o_ref):
    zr = z_ref[0].astype(BF16)
    zi = z_ref[1].astype(BF16)
    o_ref[...] = _dot(zr, c_ref[...].astype(BF16)) + _dot(zi, s_ref[...].astype(BF16))


def _fourier_ctx(zf):
    b, _, ch, l = zf.shape
    four_c, four_s = (jnp.asarray(m) for m in _const_dense(l)[:2])
    return pl.pallas_call(
        _fourier_ctx_kernel,
        grid=(b,),
        in_specs=[pl.BlockSpec((None, 2, ch, l), lambda i: (i, 0, 0, 0)),
                  _resident((l, l)), _resident((l, l))],
        out_specs=pl.BlockSpec((None, ch, l), lambda i: (i, 0, 0)),
        out_shape=jax.ShapeDtypeStruct((b, ch, l), F32),
        compiler_params=_params("parallel"),
        name="fourier_ctx",
    )(zf, four_c, four_s)


def _filter_mlp(zt, w1t, b1, freq, w2t, b2):
    h = jnp.sin(freq * (_dot_hi(w1t, zt) + b1))
    return jnp.sin(freq * (_dot_hi(w2t, h) + b2))


def _hfilter_kernel(zt_ref, t_ref, dl_ref, w1t_ref, b1_ref, fr_ref, w2t_ref, b2_ref, w3t_ref,
                    w1f_ref, tw_ref, w2s_ref, w2d_ref, o_ref, h2_ref, xs_ref, ys_ref):
    cf = o_ref.shape[0]
    a_len = xs_ref.shape[1]
    k1n = 2 * a_len
    l = a_len * LANES

    @pl.when((pl.program_id(0) == 0) & (pl.program_id(1) == 0))
    def _():
        step = min(l, 1024)
        for s in range(0, l, step):
            h2_ref[:, s:s + step] = _filter_mlp(zt_ref[:, s:s + step], w1t_ref[...], b1_ref[...], fr_ref[...],
                                                w2t_ref[...], b2_ref[...]).astype(h2_ref.dtype)

    w3 = w3t_ref[...].reshape(2 * cf, HY_HID).astype(BF16)
    decay = jnp.exp(-(dl_ref[...] * t_ref[...]))
    hv = _dot(w3, h2_ref[...])
    f = hv[0:cf] * decay
    bk = hv[cf:] * decay
    tot = jnp.sum(jnp.abs(f), axis=1, keepdims=True) + jnp.sum(jnp.abs(bk), axis=1, keepdims=True)
    inv = 1.0 / tot
    lane = lax.broadcasted_iota(jnp.int32, (cf, l), 1)
    f = f * inv
    bk = jnp.where(lane == 0, 0.0, bk * inv)
    xs_ref[0:cf] = (f + bk).reshape(cf, a_len, LANES).astype(xs_ref.dtype)
    xs_ref[cf:] = (f - bk).reshape(cf, a_len, LANES).astype(xs_ref.dtype)

    w1f = w1f_ref[...].astype(BF16)
    tr, ti = tw_ref[0], tw_ref[1]

    def stage1(p, carry):
        c = 2 * p
        y = _dot(w1f, jnp.concatenate([xs_ref[c], xs_ref[c + 1]], 1))
        for h in range(2):
            yh = y[:, h * LANES:(h + 1) * LANES]
            _twiddle_store(ys_ref, c + h, yh[0:k1n], yh[k1n:], tr, ti)
        return carry

    lax.fori_loop(0, cf, stage1, 0, unroll=8)
    ysum = ys_ref[0:cf].reshape(cf * k1n, 2 * LANES)
    ydif = ys_ref[cf:].reshape(cf * k1n, 2 * LANES)
    o_ref[:, :, 0:LANES] = _dot(ysum, w2s_ref[...].astype(BF16)).reshape(cf, k1n, LANES)
    o_ref[:, :, LANES:] = _dot(ydif, w2d_ref[...].astype(BF16)).reshape(cf, k1n, LANES)


def _hfilter_latent(l, w1t, b1, freq, w2t, b2, w3t, cf=64):
    a_len = l // LANES
    k1n = 2 * a_len
    zt, t01, deltas = (jnp.asarray(m) for m in _const_filter_features(l))
    _, w1f, tw, _, w2s, w2d, _, _ = (jnp.asarray(m) for m in _const_hyena(a_len))
    orders = w3t.shape[1]
    return pl.pallas_call(
        _hfilter_kernel,
        grid=(orders, HY_CH // cf),
        in_specs=[_resident(zt.shape), _resident(t01.shape),
                  pl.BlockSpec((cf, 1), lambda o, c: (c, 0)),
                  _resident(w1t.shape), _resident(b1.shape), _resident(freq.shape),
                  _resident(w2t.shape), _resident(b2.shape),
                  pl.BlockSpec((2, None, cf, HY_HID), lambda o, c: (0, o, c, 0)),
                  _resident(w1f.shape), _resident(tw.shape), _resident(w2s.shape), _resident(w2d.shape)],
        out_specs=pl.BlockSpec((None, cf, k1n, 2 * LANES), lambda o, c: (o, c, 0, 0)),
        out_shape=jax.ShapeDtypeStruct((orders, HY_CH, k1n, 2 * LANES), F32),
        scratch_shapes=[pltpu.VMEM((HY_HID, l), BF16),
                        pltpu.VMEM((2 * cf, a_len, LANES), BF16),
                        pltpu.VMEM((2 * cf, k1n, 2 * LANES), BF16)],
        compiler_params=_params("arbitrary", "arbitrary"),
        name="hyena_filter_latent",
    )(zt, t01, deltas, w1t, b1, freq, w2t, b2, w3t, w1f, tw, w2s, w2d)


def _hfilter_ctx_kernel(zt_ref, t_ref, dl_ref, w1t_ref, b1_ref, fr_ref, w2t_ref, b2_ref, w3t_ref,
                        cf_ref, sf_ref, o_ref):
    l = zt_ref.shape[1]
    h2 = _filter_mlp(zt_ref[...], w1t_ref[...], b1_ref[...], fr_ref[...], w2t_ref[...], b2_ref[...])
    ch = w3t_ref.shape[1]
    w3 = w3t_ref[...].reshape(2 * ch, HY_HID)
    dl = dl_ref[...]
    hv = _dot_hi(w3, h2) * jnp.exp(-(jnp.concatenate([dl, dl], 0) * t_ref[...]))
    tot = jnp.sum(jnp.abs(hv), axis=1, keepdims=True)
    inv = 1.0 / (tot[0:ch] + tot[ch:])
    f = hv[0:ch] * inv
    lane = lax.broadcasted_iota(jnp.int32, (ch, l), 1)
    bk = jnp.where(lane == 0, 0.0, hv[ch:] * inv)
    o_ref[:, 0:2 * l] = _dot((f + bk).astype(BF16), cf_ref[...].astype(BF16))
    o_ref[:, 2 * l:] = _dot((f - bk).astype(BF16), sf_ref[...].astype(BF16))


def _hfilter_ctx(l, w1t, b1, freq, w2t, b2, w3t):
    zt, t01, deltas = (jnp.asarray(m) for m in _const_filter_features(l))
    cfm, nsf = (jnp.asarray(m) for m in _const_dense(l)[2:4])
    orders = w3t.shape[1]
    return pl.pallas_call(
        _hfilter_ctx_kernel,
        grid=(orders,),
        in_specs=[_resident(zt.shape), _resident(t01.shape), _resident(deltas.shape),
                  _resident(w1t.shape), _resident(b1.shape), _resident(freq.shape),
                  _resident(w2t.shape), _resident(b2.shape),
                  pl.BlockSpec((2, None, HY_CH, HY_HID), lambda o: (0, o, 0, 0)),
                  _resident(cfm.shape), _resident(nsf.shape)],
        out_specs=pl.BlockSpec((None, HY_CH, 4 * l), lambda o: (o, 0, 0)),
        out_shape=jax.ShapeDtypeStruct((orders, HY_CH, 4 * l), F32),
        compiler_params=_params("parallel"),
        name="hyena_filter_ctx",
    )(zt, t01, deltas, w1t, b1, freq, w2t, b2, w3t, cfm, nsf)


def _short_conv(x, cw, cb):
    l = x.shape[1]
    lane = lax.broadcasted_iota(jnp.int32, x.shape, 1)
    prev = jnp.where(lane == 0, 0.0, pltpu.roll(x, 1, 1))
    nxt = jnp.where(lane == l - 1, 0.0, pltpu.roll(x, l - 1, 1))
    return cw[:, 0:1] * prev + cw[:, 1:2] * x + cw[:, 2:3] * nxt + cb


def _short_conv_tile(x, w0, w1, w2, bias):
    a_len = x.shape[0]
    lane = lax.broadcasted_iota(jnp.int32, x.shape, 1)
    row = lax.broadcasted_iota(jnp.int32, x.shape, 0)
    xl = pltpu.roll(x, 1, 1)
    prev = jnp.where(lane == 0, jnp.where(row == 0, 0.0, pltpu.roll(xl, 1, 0)), xl)
    xr = pltpu.roll(x, LANES - 1, 1)
    nxt = jnp.where(lane == LANES - 1, jnp.where(row == a_len - 1, 0.0, pltpu.roll(xr, a_len - 1, 0)), xr)
    return w0 * prev + w1 * x + w2 * nxt + bias


def _hconv_kernel(cw_ref, cb_ref, sk_ref, x1_ref, x2_ref, v_ref, hf_ref,
                  w1_ref, tw_ref, w2_ref, w2i_ref, v1_ref, o_ref, g1_ref, g2_ref, vb_ref, ys_ref, us_ref, *, group):
    nb, ct, a_len, _ = vb_ref.shape
    k1n = 2 * a_len
    nch3 = cb_ref.shape[0]
    c0 = pl.program_id(0) * ct
    w1 = w1_ref[...].astype(BF16)
    w2 = w2_ref[...].astype(BF16)
    w2i = w2i_ref[...].astype(BF16)
    v1 = v1_ref[...].astype(BF16)
    tr, ti = tw_ref[0], tw_ref[1]

    for src, dst in ((x1_ref, g1_ref), (x2_ref, g2_ref), (v_ref, vb_ref)):
        for bi in range(nb):
            dst[bi] = src[bi].reshape(ct, a_len, LANES)

    def sconv(x, ch):
        return _short_conv_tile(x, cw_ref[ch], cw_ref[nch3 + ch], cw_ref[2 * nch3 + ch], cb_ref[ch])

    def prep(c, carry):
        for bi in range(nb):
            vb_ref[bi, c] = sconv(vb_ref[bi, c], 2 * HY_CH + c0 + c)
        return carry

    lax.fori_loop(0, ct, prep, 0, unroll=4)

    for order, gate_ref in enumerate((g1_ref, g2_ref)):
        def stage1(p, carry):
            c = 2 * p
            x = jnp.concatenate([jnp.concatenate([vb_ref[0, c + h], vb_ref[1, c + h]], 0) for h in range(2)], 1)
            y = _dot(w1, x.astype(BF16))
            for h in range(2):
                yh = y[:, h * LANES:(h + 1) * LANES]
                _twiddle_store(ys_ref, c + h, yh[0:k1n], yh[k1n:], tr, ti)
            return carry

        lax.fori_loop(0, ct // 2, stage1, 0, unroll=8)

        def middle(g, carry):
            cs = pl.ds(g * group, group)
            z = _dot(ys_ref[cs].reshape(group * k1n, 2 * LANES), w2)
            h = hf_ref[order, cs].reshape(group * k1n, 2 * LANES)
            zr, zi = z[:, 0:LANES], z[:, LANES:]
            hr, hi = h[:, 0:LANES], h[:, LANES:]
            p = jnp.concatenate([zr * hr - zi * hi, zr * hi + zi * hr], 1).astype(BF16)
            u = _dot(p, w2i).reshape(group, k1n, 2 * LANES)
            ur, ui = u[:, :, 0:LANES], u[:, :, LANES:]
            us_ref[cs, :, 0:LANES] = (ur * tr + ui * ti).astype(BF16)
            us_ref[cs, :, LANES:] = (ui * tr - ur * ti).astype(BF16)
            return carry

        lax.fori_loop(0, ct // group, middle, 0, unroll=2)

        def stage1_inv(c, carry):
            q = _dot(v1, us_ref[c])
            conv = (q[0:a_len, 0:LANES] - q[a_len:, LANES:], q[a_len:, 0:LANES] + q[0:a_len, LANES:])
            skip = sk_ref[order * HY_CH + c0 + c]
            for bi in range(nb):
                gate = sconv(gate_ref[bi, c], order * HY_CH + c0 + c)
                vb_ref[bi, c] = gate * (conv[bi] + vb_ref[bi, c] * skip)
            return carry

        lax.fori_loop(0, ct, stage1_inv, 0, unroll=8)

    for bi in range(nb):
        o_ref[bi] = vb_ref[bi].reshape(ct, a_len * LANES)


def _hconv_latent(ph, hf, conv_w, conv_b, skip, ct=32, group=16):
    b, nch3, l = ph.shape
    assert b == 2, "the two batch entries ride as real/imaginary parts"
    a_len = l // LANES
    k1n = 2 * a_len
    w1, _, tw, w2, _, _, w2i, v1 = (jnp.asarray(m) for m in _const_hyena(a_len))
    nblk = HY_CH // ct
    smem = pl.BlockSpec(memory_space=pltpu.SMEM)
    part = lambda off: pl.BlockSpec((b, ct, l), lambda c: (0, c + off * nblk, 0))
    scratch = pltpu.VMEM((b, ct, a_len, LANES), F32)
    return pl.pallas_call(
        functools.partial(_hconv_kernel, group=group),
        grid=(nblk,),
        in_specs=[smem, smem, smem, part(0), part(1), part(2),
                  pl.BlockSpec((2, ct, k1n, 2 * LANES), lambda c: (0, c, 0, 0)),
                  _resident(w1.shape), _resident(tw.shape), _resident(w2.shape),
                  _resident(w2i.shape), _resident(v1.shape)],
        out_specs=pl.BlockSpec((b, ct, l), lambda c: (0, c, 0)),
        out_shape=jax.ShapeDtypeStruct((b, HY_CH, l), F32),
        scratch_shapes=[scratch, scratch, scratch,
                        pltpu.VMEM((ct, k1n, 2 * LANES), BF16),
                        pltpu.VMEM((ct, k1n, 2 * LANES), BF16)],
        compiler_params=_params("parallel"),
        name="hyena_conv_latent",
    )(conv_w.reshape(-1), conv_b, skip.reshape(-1), ph, ph, ph, hf, w1, tw, w2, w2i, v1)


def _hconv_ctx_kernel(cw_ref, cb_ref, sk_ref, p_ref, hf_ref, fwd_ref, inv_ref, o_ref):
    nb, _, l = p_ref.shape
    ch = HY_CH
    z = [_short_conv(p_ref[bi], cw_ref[...], cb_ref[...]) for bi in range(nb)]
    fwd = fwd_ref[...].astype(BF16)
    inv = inv_ref[...].astype(BF16)
    cur = [zz[2 * ch:] for zz in z]
    for order in range(2):
        spec = _dot(jnp.concatenate(cur, 1).astype(BF16), fwd)
        h = hf_ref[order]
        zr, zi, hr, hi = spec[:, 0:2 * l], spec[:, 2 * l:], h[:, 0:2 * l], h[:, 2 * l:]
        prod = jnp.concatenate([zr * hr - zi * hi, zr * hi + zi * hr], 1).astype(BF16)
        y = _dot(prod, inv)
        sk = sk_ref[:, order:order + 1]
        cur = [z[bi][order * ch:(order + 1) * ch] * (y[:, bi * l:(bi + 1) * l] + cur[bi] * sk) for bi in range(nb)]
    for bi in range(nb):
        o_ref[bi] = cur[bi]


def _hconv_ctx(ph, hf, conv_w, conv_b, skip):
    b, nch3, l = ph.shape
    assert b == 2
    fwd, inv = (jnp.asarray(m) for m in _const_dense(l)[4:6])
    cw = jnp.pad(conv_w.T, ((0, 0), (0, 5)))
    return pl.pallas_call(
        _hconv_ctx_kernel,
        in_specs=[pl.BlockSpec(cw.shape, lambda: (0, 0)),
                  pl.BlockSpec((nch3, 1), lambda: (0, 0)),
                  pl.BlockSpec((HY_CH, 2), lambda: (0, 0)),
                  pl.BlockSpec(ph.shape, lambda: (0, 0, 0)),
                  pl.BlockSpec(hf.shape, lambda: (0, 0, 0)),
                  pl.BlockSpec(fwd.shape, lambda: (0, 0)),
                  pl.BlockSpec(inv.shape, lambda: (0, 0))],
        out_specs=pl.BlockSpec((b, HY_CH, l), lambda: (0, 0, 0)),
        out_shape=jax.ShapeDtypeStruct((b, HY_CH, l), F32),
        compiler_params=pltpu.CompilerParams(vmem_limit_bytes=VMEM_LIMIT_BYTES),
        name="hyena_conv_ctx",
    )(cw, conv_b.reshape(-1, 1), skip.T, ph, hf, fwd, inv)


def _mix_ffn_kernel(*refs, n_mix, final, ff_chunk):
    d = D_MODEL
    h_ref, y_refs = refs[0], refs[1:1 + n_mix]
    wo_ref, mod_ref, g2_ref, wg_ref, wu_ref, wd_ref = refs[1 + n_mix:7 + n_mix]
    rest = refs[7 + n_mix:]
    y, row = None, 0
    for y_ref in y_refs:
        rows = y_ref.shape[0]
        part = lax.dot_general(y_ref[...].astype(BF16), wo_ref[row:row + rows, :], _TN, preferred_element_type=F32)
        y = part if y is None else y + part
        row += rows
    if final:
        fg_ref, o_ref = rest
    else:
        (o_ref,) = rest
    h1 = h_ref[...] + mod_ref[:, 2 * d:3 * d] * y
    u = _norm_mod(h1, g2_ref[...], mod_ref[:, 3 * d:4 * d], mod_ref[:, 4 * d:5 * d]).astype(BF16)
    acc = jnp.zeros(h1.shape, F32)
    for j in range(D_FF // ff_chunk):
        cols = slice(j * ff_chunk, (j + 1) * ff_chunk)
        act = (_silu(_dot(u, wg_ref[:, cols])) * _dot(u, wu_ref[:, cols])).astype(BF16)
        acc = acc + _dot(act, wd_ref[cols, :])
    h2 = h1 + mod_ref[:, 5 * d:6 * d] * acc
    if final:
        h2 = _rms(h2, fg_ref[...])
    o_ref[...] = h2


def _mix_ffn(h, ys, wo, mod3, g2, wg, wu, wd, final_g=None, tl=512, ff_chunk=256):
    b, l, d = h.shape
    tl = min(tl, l)
    tok = pl.BlockSpec((None, tl, d), lambda i, t: (i, t, 0))
    y_specs = [pl.BlockSpec((None, y.shape[1], tl), lambda i, t: (i, 0, t)) for y in ys]
    in_specs = ([tok] + y_specs + [_resident(wo.shape), pl.BlockSpec((None, 1, 6 * d), lambda i, t: (i, 0, 0)),
                                    pl.BlockSpec((1, d), lambda i, t: (0, 0)),
                                    _resident(wg.shape), _resident(wu.shape), _resident(wd.shape)])
    args = [h, *ys, wo, mod3, g2, wg, wu, wd]
    if final_g is not None:
        in_specs.append(pl.BlockSpec((1, d), lambda i, t: (0, 0)))
        args.append(final_g)
    return pl.pallas_call(
        functools.partial(_mix_ffn_kernel, n_mix=len(ys), final=final_g is not None, ff_chunk=ff_chunk),
        grid=(b, l // tl),
        in_specs=in_specs,
        out_specs=tok,
        out_shape=jax.ShapeDtypeStruct((b, l, d), F32),
        compiler_params=_params("parallel", "parallel"),
        name="mix_ffn_final" if final_g is not None else "mix_ffn",
    )(*args)


def _qkv_kernel(x_ref, g_ref, mod_ref, w_ref, q_ref, k_ref, v_ref):
    d = D_MODEL
    u = _norm_mod(x_ref[...], g_ref[...], mod_ref[:, 0:d], mod_ref[:, d:2 * d]).astype(BF16)
    q_ref[...] = (_dot(u, w_ref[:, 0:d]) * (NA_HEAD_DIM ** -0.5)).astype(BF16)
    k_ref[...] = _dot(u, w_ref[:, d:2 * d]).astype(BF16)
    v_ref[...] = _dot(u, w_ref[:, 2 * d:]).astype(BF16)


def _qkv(h, g, mod3, w, tl=1024):
    b, l, d = h.shape
    tl = min(tl, l)
    tok = pl.BlockSpec((None, tl, d), lambda i, t: (i, t, 0))
    return pl.pallas_call(
        _qkv_kernel,
        grid=(b, l // tl),
        in_specs=[tok, pl.BlockSpec((1, d), lambda i, t: (0, 0)),
                  pl.BlockSpec((None, 1, 6 * d), lambda i, t: (i, 0, 0)), _resident(w.shape)],
        out_specs=[tok, tok, tok],
        out_shape=[jax.ShapeDtypeStruct((b, l, d), BF16)] * 3,
        compiler_params=_params("parallel", "parallel"),
        name="qkv_proj",
    )(h, g, mod3, w)


NA_QROWS = 4
NA_KROWS = 12
NA_PATTERNS = 3


def _na_key_start(r0, grid_rows):
    return jnp.minimum(jnp.clip(r0 - NA_KH // 2, 0, grid_rows - NA_KH), grid_rows - NA_KROWS)


def _na_pattern(pat, i, j):
    if pat == 0:
        return j - i + NA_KH - 1, j < NA_KH
    if pat == 1:
        return j - i + NA_KH // 2 - 1, i <= j < i + NA_KH
    return j - i - 1, j >= NA_KROWS - NA_KH


def _bias_blocks_kernel(rpb_ref, o_ref):
    pair = pl.program_id(0)
    ndr, ndc = 2 * NA_KH - 1, 2 * NA_KW - 1
    kcol = lax.broadcasted_iota(jnp.int32, (GRID_W, LANES), 0)
    lane = lax.broadcasted_iota(jnp.int32, (GRID_W, LANES), 1)
    q = lane % GRID_W
    upper = lane >= GRID_W
    wstart = jnp.clip(q - NA_KW // 2, 0, GRID_W - NA_KW)
    ok = (kcol >= wstart) & (kcol < wstart + NA_KW)
    dc = kcol - q + (NA_KW - 1)
    neg = jnp.full((GRID_W, LANES), NEG_BIG, F32)
    for e in range(2):
        base = (2 * pair + e) * ndr * ndc
        tiles = []
        for dr in range(ndr):
            acc = jnp.zeros((GRID_W, LANES), F32)
            for dd in range(ndc):
                acc = jnp.where(dc == dd, rpb_ref[base + dr * ndc + dd], acc)
            tiles.append(jnp.where(ok, acc, NEG_BIG))
        for pat in range(NA_PATTERNS):
            for j in range(NA_KROWS):
                for p in range(NA_QROWS // 2):
                    halves = []
                    for i in (2 * p, 2 * p + 1):
                        dr, visible = _na_pattern(pat, i, j)
                        halves.append(tiles[dr] if visible else neg)
                    o_ref[e, pat, j * GRID_W:(j + 1) * GRID_W, p * LANES:(p + 1) * LANES] = jnp.where(
                        upper, halves[1], halves[0])


def _bias_blocks(rpb):
    heads = rpb.shape[0]
    nq, nk = NA_QROWS * GRID_W, NA_KROWS * GRID_W
    return pl.pallas_call(
        _bias_blocks_kernel,
        grid=(heads // 2,),
        in_specs=[pl.BlockSpec(memory_space=pltpu.SMEM)],
        out_specs=pl.BlockSpec((None, 2, NA_PATTERNS, nk, nq), lambda p: (p, 0, 0, 0, 0)),
        out_shape=jax.ShapeDtypeStruct((heads // 2, 2, NA_PATTERNS, nk, nq), F32),
        compiler_params=_params("parallel"),
        name="na_bias_blocks",
    )(rpb.reshape(-1))


def _na_kernel(q_ref, k_ref, v_ref, kc_ref, vc_ref, bias_ref, o_ref, *, blocks_per_step, grid_rows):
    nq, nk = NA_QROWS * GRID_W, NA_KROWS * GRID_W
    lane = lax.broadcasted_iota(jnp.int32, (nq, LANES), 1)
    row = lax.broadcasted_iota(jnp.int32, (LANES, nq), 0)
    kc = kc_ref[...]
    vc = vc_ref[...]
    step = pl.program_id(2)
    for i in range(blocks_per_step):
        r0 = (step * blocks_per_step + i) * NA_QROWS
        pat = jnp.where(r0 == 0, 0, jnp.where(r0 == grid_rows - NA_QROWS, 2, 1))
        koff = pl.multiple_of(_na_key_start(r0, grid_rows) * GRID_W, GRID_W)
        q = q_ref[i * nq:(i + 1) * nq, :]
        ku = k_ref[pl.ds(koff, nk), :]
        vt = jnp.concatenate([v_ref[pl.ds(koff, nk), :], vc], 0).T
        vt = jnp.concatenate([vt, jnp.ones((16, vt.shape[1]), vt.dtype)], 0)
        outs = []
        for e in range(2):
            head = (lane >= NA_HEAD_DIM) if e else (lane < NA_HEAD_DIM)
            qh = jnp.where(head, q, jnp.zeros_like(q))
            sl = lax.dot_general(ku, qh, _NT, preferred_element_type=F32) + bias_ref[e, pat]
            sc = lax.dot_general(kc, qh, _NT, preferred_element_type=F32)
            s = jnp.concatenate([sl, sc], 0)
            p = jnp.exp((s - jnp.max(s, axis=0, keepdims=True)).astype(BF16))
            o = _dot(vt, p)
            outs.append(o[0:LANES] / o[LANES:LANES + 1])
        o_ref[:, i * nq:(i + 1) * nq] = jnp.where(row < NA_HEAD_DIM, outs[0], outs[1]).astype(o_ref.dtype)


def _na(q, k, v, kc, vc, bias_blocks, blocks_per_step=8):
    b, l, d = q.shape
    lc = kc.shape[1]
    grid_rows = l // GRID_W
    blocks_per_step = min(blocks_per_step, grid_rows // NA_QROWS)
    assert grid_rows % (NA_QROWS * blocks_per_step) == 0 and grid_rows >= NA_KROWS
    npairs = d // LANES
    tq = blocks_per_step * NA_QROWS * GRID_W
    return pl.pallas_call(
        functools.partial(_na_kernel, blocks_per_step=blocks_per_step, grid_rows=grid_rows),
        grid=(npairs, b, l // tq),
        in_specs=[pl.BlockSpec((None, tq, LANES), lambda p, i, t: (i, t, p)),
                  pl.BlockSpec((None, l, LANES), lambda p, i, t: (i, 0, p)),
                  pl.BlockSpec((None, l, LANES), lambda p, i, t: (i, 0, p)),
                  pl.BlockSpec((None, lc, LANES), lambda p, i, t: (i, 0, p)),
                  pl.BlockSpec((None, lc, LANES), lambda p, i, t: (i, 0, p)),
                  pl.BlockSpec((None,) + bias_blocks.shape[1:], lambda p, i, t: (p, 0, 0, 0, 0))],
        out_specs=pl.BlockSpec((None, LANES, tq), lambda p, i, t: (i, p, t)),
        out_shape=jax.ShapeDtypeStruct((b, d, l), BF16),
        compiler_params=_params("parallel", "parallel", "parallel"),
        name="neighbourhood_attention",
    )(q, k, v, kc, vc, bias_blocks)


def kernel(x, c, ctx, c_ctx, ada_w, ada_b, norm1_g, norm2_g, ffn_w_gate, ffn_w_up, ffn_w_down, ab_w_in, ab_w_out, hy_conv_w, hy_conv_b, hy_f_w1, hy_f_b1, hy_f_freq, hy_f_w2, hy_f_b2, hy_f_w3, hy_bias, na_w_qkv, na_w_out, na_rpb, final_g):
    b, l, d = x.shape
    lc = ctx.shape[1]
    assert b == 2 and d == D_MODEL and l % 1024 == 0 and ada_w.shape[0] == 2

    cond8 = jnp.zeros((8, d), F32).at[0:b].set(c).at[b].set(c_ctx)
    mod = _ada(cond8, ada_w, ada_b)
    mod_lat = [mod[i, 0:b].reshape(b, 1, 6 * d) for i in range(2)]
    mod_ctx = [jnp.broadcast_to(mod[i, b:b + 1], (b, 6 * d)).reshape(b, 1, 6 * d) for i in range(2)]
    n1 = [norm1_g[i].reshape(1, d) for i in range(2)]
    n2 = [norm2_g[i].reshape(1, d) for i in range(2)]
    wg = [ffn_w_gate[i].astype(BF16) for i in range(2)]
    wu = [ffn_w_up[i].astype(BF16) for i in range(2)]
    wd = [ffn_w_down[i].astype(BF16) for i in range(2)]

    w_in_t = ab_w_in[0].T.astype(BF16)
    w_out = ab_w_out[0].astype(BF16)
    w1t = jnp.pad(hy_f_w1[0].T, ((0, 0), (0, LANES - HY_EMB)))
    col = lambda vec: vec.reshape(-1, 1)
    mlp = (w1t, col(hy_f_b1[0]), col(hy_f_freq[0]), hy_f_w2[0].T, col(hy_f_b2[0]),
           hy_f_w3[0].T.reshape(2, 2, HY_CH, HY_HID))

    zf, ph = _inproj(x, n1[0], mod_lat[0], w_in_t)
    y_f = _fourier_latent(zf)
    hf = _hfilter_latent(l, *mlp)
    y_h = _hconv_latent(ph, hf, hy_conv_w[0], hy_conv_b[0], hy_bias[0]).reshape(b, HY_CH, l)
    h_lat = _mix_ffn(x, (y_f, y_h), w_out, mod_lat[0], n2[0], wg[0], wu[0], wd[0])

    zfc, phc = _inproj(ctx, n1[0], mod_ctx[0], w_in_t)
    y_fc = _fourier_ctx(zfc)
    hfc = _hfilter_ctx(lc, *mlp)
    y_hc = _hconv_ctx(phc, hfc, hy_conv_w[0], hy_conv_b[0], hy_bias[0])
    h_ctx = _mix_ffn(ctx, (y_fc, y_hc), w_out, mod_ctx[0], n2[0], wg[0], wu[0], wd[0])

    w_qkv = na_w_qkv[0].astype(BF16)
    q, k, v = _qkv(h_lat, n1[1], mod_lat[1], w_qkv)
    _, kc, vc = _qkv(h_ctx, n1[1], mod_ctx[1], w_qkv)
    attn = _na(q, k, v, kc, vc, _bias_blocks(na_rpb[0]))
    return _mix_ffn(h_lat, (attn,), na_w_out[0].astype(BF16), mod_lat[1], n2[1], wg[1], wu[1], wd[1],
                    final_g=final_g.reshape(1, d))
```

```python
import functools
import math

import numpy as np
import jax
import jax.numpy as jnp
from jax import lax
from jax.experimental import pallas as pl
from jax.experimental.pallas import tpu as pltpu

F32 = jnp.float32
BF16 = jnp.bfloat16

D_MODEL = 1024
FOURIER_CH = 512
FOURIER_GROUPS = 4
GROUP_DIM = 128
HY_CH = 512
HY_BANDS = 16
HY_EMB = 33
HY_HID = 64
HY_FAST_DECAY = 0.3
HY_SLOW_DECAY = 1.5
HY_DECAY_TARGET = 1e-2
D_FF = 2816
NA_HEADS = 16
NA_HEAD_DIM = 64
NA_KH = 8
NA_KW = 16
GRID_W = 64
RMS_EPS = 1e-6
NEG_BIG = -1e30

LANES = 128
VMEM_LIMIT_BYTES = 56 * 1024 * 1024

_NT = (((1,), (1,)), ((), ()))
_TN = (((0,), (0,)), ((), ()))


def _params(*sem):
    return pltpu.CompilerParams(dimension_semantics=sem, vmem_limit_bytes=VMEM_LIMIT_BYTES)


def _dot(a, b):
    return jnp.dot(a, b, preferred_element_type=F32)


def _dot_hi(a, b):
    return jnp.dot(a, b, preferred_element_type=F32, precision=lax.Precision.HIGHEST)


def _resident(shape):
    nd = len(shape)
    return pl.BlockSpec(shape, lambda *_: (0,) * nd, pipeline_mode=pl.Buffered(1))


def _rms(x, g):
    ms = jnp.mean(x * x, axis=-1, keepdims=True)
    return x * lax.rsqrt(ms + RMS_EPS) * g


def _norm_mod(x, g, shift, scale):
    return _rms(x, g) * (1.0 + scale) + shift


def _silu(x):
    return x * jax.nn.sigmoid(x)


def _cs(num, den):
    ang = 2.0 * np.pi * (np.asarray(num, np.float64) % den) / den
    return np.cos(ang), np.sin(ang)


@functools.lru_cache(maxsize=None)
def _const_channel_dft():
    c, s = _cs(np.outer(np.arange(GROUP_DIM), np.arange(GROUP_DIM)), GROUP_DIM)
    return (np.concatenate([c, -s], 0) / math.sqrt(GROUP_DIM)).astype(np.float32)


@functools.lru_cache(maxsize=None)
def _const_fourier(a_len):
    n = a_len * LANES
    c, s = _cs(np.outer(np.arange(a_len), np.arange(a_len)), a_len)
    w1 = np.block([[c, s], [-s, c]])
    tc, ts = _cs(np.outer(np.arange(a_len), np.arange(LANES)), n)
    tw = np.stack([tc, -ts])
    c2, s2 = _cs(np.outer(np.arange(LANES), np.arange(LANES)), LANES)
    w2 = np.concatenate([c2, s2], 0) / math.sqrt(n)
    r = LANES // a_len
    perm = np.array([r * m + h for h in range(r) for m in range(a_len)])
    w2 = w2[:, perm]
    return w1.astype(np.float32), tw.astype(np.float32), w2.astype(np.float32)


@functools.lru_cache(maxsize=None)
def _const_hyena(a_len):
    k1n = 2 * a_len
    n = k1n * LANES
    c, s = _cs(np.outer(np.arange(k1n), np.arange(a_len)), k1n)
    w1 = np.block([[c, s], [-s, c]])
    w1f = np.concatenate([c, -s], 0)
    tc, ts = _cs(np.outer(np.arange(k1n), np.arange(LANES)), n)
    tw = np.stack([tc, -ts])
    c2, s2 = _cs(np.outer(np.arange(LANES), np.arange(LANES)), LANES)
    w2 = np.block([[c2, -s2], [s2, c2]])
    w2s = np.concatenate([c2, s2], 0)
    w2d = np.concatenate([-s2, c2], 0)
    w2inv = np.block([[c2, s2], [-s2, c2]])
    v1 = np.concatenate([c.T, s.T], 0) / n
    f = lambda m: m.astype(np.float32)
    return f(w1), f(w1f), f(tw), f(w2), f(w2s), f(w2d), f(w2inv), f(v1)


@functools.lru_cache(maxsize=None)
def _const_dense(l_len):
    n = 2 * l_len
    cl, sl = _cs(np.outer(np.arange(l_len), np.arange(l_len)), l_len)
    four_c, four_s = cl / math.sqrt(l_len), sl / math.sqrt(l_len)
    cf, sf = _cs(np.outer(np.arange(l_len), np.arange(n)), n)
    fwd = np.block([[cf, -sf], [sf, cf]])
    inv = np.block([[cf.T, sf.T], [-sf.T, cf.T]]) / n
    f = lambda m: m.astype(np.float32)
    return f(four_c), f(four_s), f(cf), f(-sf), f(fwd), f(inv)


@functools.lru_cache(maxsize=None)
def _const_filter_features(l_len):
    t01 = np.linspace(0.0, 1.0, l_len)
    bands = np.linspace(1e-4, HY_BANDS - 1, HY_BANDS)
    ang = (2.0 * np.pi / l_len) * np.arange(l_len)[:, None] * bands[None, :]
    z = np.concatenate([t01[:, None], np.cos(ang), -np.sin(ang)], -1)
    zt = np.zeros((LANES, l_len), np.float64)
    zt[:HY_EMB] = z.T
    max_decay = math.log(HY_DECAY_TARGET) / HY_FAST_DECAY
    min_decay = math.log(HY_DECAY_TARGET) / HY_SLOW_DECAY
    deltas = np.abs(np.linspace(min_decay, max_decay, HY_CH))
    return zt.astype(np.float32), t01[None, :].astype(np.float32), deltas[:, None].astype(np.float32)


def _chunk_time(m):
    rows, l_len = m.shape
    return np.ascontiguousarray(m.reshape(rows, l_len // LANES, LANES).transpose(1, 0, 2))


def _ada_kernel(c_ref, w_ref, b_ref, o_ref):
    s = _silu(c_ref[...]).astype(BF16)
    o_ref[...] = _dot(s, w_ref[...].astype(BF16)) + b_ref[...]


def _ada(cond8, ada_w, ada_b):
    depth, d, n6 = ada_w.shape
    tn = 1536
    return pl.pallas_call(
        _ada_kernel,
        grid=(depth, n6 // tn),
        in_specs=[pl.BlockSpec((8, d), lambda i, j: (0, 0)),
                  pl.BlockSpec((None, d, tn), lambda i, j: (i, 0, j)),
                  pl.BlockSpec((None, 1, tn), lambda i, j: (i, 0, j))],
        out_specs=pl.BlockSpec((None, 8, tn), lambda i, j: (i, 0, j)),
        out_shape=jax.ShapeDtypeStruct((depth, 8, n6), F32),
        compiler_params=_params("parallel", "parallel"),
        name="ada_mod",
    )(cond8, ada_w, ada_b.reshape(depth, 1, n6))


def _inproj_kernel(x_ref, g_ref, mod_ref, wt_ref, cdft_ref, zf_ref, ph_ref):
    d = D_MODEL
    u = _norm_mod(x_ref[...], g_ref[...], mod_ref[:, 0:d], mod_ref[:, d:2 * d]).astype(BF16)
    cdft = cdft_ref[...].astype(BF16)
    for g in range(FOURIER_GROUPS):
        rows = slice(g * GROUP_DIM, (g + 1) * GROUP_DIM)
        pt = lax.dot_general(wt_ref[rows, :], u, _NT, preferred_element_type=F32)
        z = _dot(cdft, pt.astype(BF16))
        zf_ref[0, rows, :] = z[0:GROUP_DIM]
        zf_ref[1, rows, :] = z[GROUP_DIM:]
    rb = 512
    for r0 in range(FOURIER_CH, wt_ref.shape[0], rb):
        ph_ref[r0 - FOURIER_CH:r0 - FOURIER_CH + rb, :] = lax.dot_general(
            wt_ref[r0:r0 + rb, :], u, _NT, preferred_element_type=F32)


def _inproj(x, g, mod3, w_in_t, tl=1024):
    b, l, d = x.shape
    tl = min(tl, l)
    nh = w_in_t.shape[0] - FOURIER_CH
    return pl.pallas_call(
        _inproj_kernel,
        grid=(b, l // tl),
        in_specs=[pl.BlockSpec((None, tl, d), lambda i, t: (i, t, 0)),
                  pl.BlockSpec((1, d), lambda i, t: (0, 0)),
                  pl.BlockSpec((None, 1, 6 * d), lambda i, t: (i, 0, 0)),
                  _resident(w_in_t.shape),
                  _resident((2 * GROUP_DIM, GROUP_DIM))],
        out_specs=[pl.BlockSpec((None, 2, FOURIER_CH, tl), lambda i, t: (i, 0, 0, t)),
                   pl.BlockSpec((None, nh, tl), lambda i, t: (i, 0, t))],
        out_shape=[jax.ShapeDtypeStruct((b, 2, FOURIER_CH, l), F32),
                   jax.ShapeDtypeStruct((b, nh, l), F32)],
        compiler_params=_params("parallel", "parallel"),
        name="inproj",
    )(x, g, mod3, w_in_t, jnp.asarray(_const_channel_dft()))


def _twiddle_store(ys_ref, c, yr, yi, tr, ti):
    ys_ref[c, :, 0:LANES] = (yr * tr - yi * ti).astype(ys_ref.dtype)
    ys_ref[c, :, LANES:] = (yr * ti + yi * tr).astype(ys_ref.dtype)


def _fourier_kernel(z_ref, w1_ref, tw_ref, w2_ref, o_ref, zs_ref, ys_ref):
    _, ct, a_len, _ = zs_ref.shape
    w1 = w1_ref[...].astype(BF16)
    tr, ti = tw_ref[0], tw_ref[1]
    for ri in range(2):
        zs_ref[ri] = z_ref[ri].reshape(ct, a_len, LANES)

    def stage1(p, carry):
        c = 2 * p
        x = jnp.concatenate([jnp.concatenate([zs_ref[0, c + h], zs_ref[1, c + h]], 0) for h in range(2)], 1)
        y = _dot(w1, x.astype(BF16))
        for h in range(2):
            yh = y[:, h * LANES:(h + 1) * LANES]
            _twiddle_store(ys_ref, c + h, yh[0:a_len], yh[a_len:], tr, ti)
        return carry

    lax.fori_loop(0, ct // 2, stage1, 0, unroll=8)
    y = ys_ref[...].reshape(ct * a_len, 2 * LANES)
    zr = _dot(y, w2_ref[...].astype(BF16))
    tiles = []
    for c in range(ct):
        t = zr[c * a_len:(c + 1) * a_len].T
        tiles.append(jnp.concatenate([t[h * a_len:(h + 1) * a_len] for h in range(LANES // a_len)], 1))
    o_ref[...] = jnp.stack(tiles, 0).reshape(ct, a_len * LANES)


def _fourier_latent(zf, ct=32):
    b, _, ch, l = zf.shape
    a_len = l // LANES
    w1, tw, w2 = (jnp.asarray(m) for m in _const_fourier(a_len))
    return pl.pallas_call(
        _fourier_kernel,
        grid=(b, ch // ct),
        in_specs=[pl.BlockSpec((None, 2, ct, l), lambda i, c: (i, 0, c, 0)),
                  _resident(w1.shape), _resident(tw.shape), _resident(w2.shape)],
        out_specs=pl.BlockSpec((None, ct, l), lambda i, c: (i, c, 0)),
        out_shape=jax.ShapeDtypeStruct((b, ch, l), F32),
        scratch_shapes=[pltpu.VMEM((2, ct, a_len, LANES), F32),
                        pltpu.VMEM((ct, a_len, 2 * LANES), BF16)],
        compiler_params=_params("parallel", "parallel"),
        name="fourier_latent",
    )(zf, w1, tw, w2)


def _fourier_ctx_kernel(z_ref, c_ref, s_ref, o_ref):
    zr = z_ref[0].astype(BF16)
    zi = z_ref[1].astype(BF16)
    o_ref[...] = _dot(zr, c_ref[...].astype(BF16)) + _dot(zi, s_ref[...].astype(BF16))


def _fourier_ctx(zf):
    b, _, ch, l = zf.shape
    four_c, four_s = (jnp.asarray(m) for m in _const_dense(l)[:2])
    return pl.pallas_call(
        _fourier_ctx_kernel,
        grid=(b,),
        in_specs=[pl.BlockSpec((None, 2, ch, l), lambda i: (i, 0, 0, 0)),
                  _resident((l, l)), _resident((l, l))],
        out_specs=pl.BlockSpec((None, ch, l), lambda i: (i, 0, 0)),
        out_shape=jax.ShapeDtypeStruct((b, ch, l), F32),
        compiler_params=_params("parallel"),
        name="fourier_ctx",
    )(zf, four_c, four_s)


def _filter_mlp(zt, w1t, b1, freq, w2t, b2):
    h = jnp.sin(freq * (_dot_hi(w1t, zt) + b1))
    return jnp.sin(freq * (_dot_hi(w2t, h) + b2))


def _hfilter_kernel(zt_ref, t_ref, dl_ref, w1t_ref, b1_ref, fr_ref, w2t_ref, b2_ref, w3t_ref,
                    w1f_ref, tw_ref, w2s_ref, w2d_ref, o_ref, h2_ref, xs_ref, ys_ref):
    cf = o_ref.shape[0]
    a_len = xs_ref.shape[1]
    k1n = 2 * a_len
    l = a_len * LANES

    @pl.when((pl.program_id(0) == 0) & (pl.program_id(1) == 0))
    def _():
        step = min(l, 1024)
        for s in range(0, l, step):
            h2_ref[:, s:s + step] = _filter_mlp(zt_ref[:, s:s + step], w1t_ref[...], b1_ref[...], fr_ref[...],
                                                w2t_ref[...], b2_ref[...]).astype(h2_ref.dtype)

    w3 = w3t_ref[...].reshape(2 * cf, HY_HID).astype(BF16)
    decay = jnp.exp(-(dl_ref[...] * t_ref[...]))
    hv = _dot(w3, h2_ref[...])
    f = hv[0:cf] * decay
    bk = hv[cf:] * decay
    tot = jnp.sum(jnp.abs(f), axis=1, keepdims=True) + jnp.sum(jnp.abs(bk), axis=1, keepdims=True)
    inv = 1.0 / tot
    lane = lax.broadcasted_iota(jnp.int32, (cf, l), 1)
    f = f * inv
    bk = jnp.where(lane == 0, 0.0, bk * inv)
    xs_ref[0:cf] = (f + bk).reshape(cf, a_len, LANES).astype(xs_ref.dtype)
    xs_ref[cf:] = (f - bk).reshape(cf, a_len, LANES).astype(xs_ref.dtype)

    w1f = w1f_ref[...].astype(BF16)
    tr, ti = tw_ref[0], tw_ref[1]

    def stage1(p, carry):
        c = 2 * p
        y = _dot(w1f, jnp.concatenate([xs_ref[c], xs_ref[c + 1]], 1))
        for h in range(2):
            yh = y[:, h * LANES:(h + 1) * LANES]
            _twiddle_store(ys_ref, c + h, yh[0:k1n], yh[k1n:], tr, ti)
        return carry

    lax.fori_loop(0, cf, stage1, 0, unroll=8)
    ysum = ys_ref[0:cf].reshape(cf * k1n, 2 * LANES)
    ydif = ys_ref[cf:].reshape(cf * k1n, 2 * LANES)
    o_ref[:, :, 0:LANES] = _dot(ysum, w2s_ref[...].astype(BF16)).reshape(cf, k1n, LANES)
    o_ref[:, :, LANES:] = _dot(ydif, w2d_ref[...].astype(BF16)).reshape(cf, k1n, LANES)


def _hfilter_latent(l, w1t, b1, freq, w2t, b2, w3t, cf=64):
    a_len = l // LANES
    k1n = 2 * a_len
    zt, t01, deltas = (jnp.asarray(m) for m in _const_filter_features(l))
    _, w1f, tw, _, w2s, w2d, _, _ = (jnp.asarray(m) for m in _const_hyena(a_len))
    orders = w3t.shape[1]
    return pl.pallas_call(
        _hfilter_kernel,
        grid=(orders, HY_CH // cf),
        in_specs=[_resident(zt.shape), _resident(t01.shape),
                  pl.BlockSpec((cf, 1), lambda o, c: (c, 0)),
                  _resident(w1t.shape), _resident(b1.shape), _resident(freq.shape),
                  _resident(w2t.shape), _resident(b2.shape),
                  pl.BlockSpec((2, None, cf, HY_HID), lambda o, c: (0, o, c, 0)),
                  _resident(w1f.shape), _resident(tw.shape), _resident(w2s.shape), _resident(w2d.shape)],
        out_specs=pl.BlockSpec((None, cf, k1n, 2 * LANES), lambda o, c: (o, c, 0, 0)),
        out_shape=jax.ShapeDtypeStruct((orders, HY_CH, k1n, 2 * LANES), F32),
        scratch_shapes=[pltpu.VMEM((HY_HID, l), BF16),
                        pltpu.VMEM((2 * cf, a_len, LANES), BF16),
                        pltpu.VMEM((2 * cf, k1n, 2 * LANES), BF16)],
        compiler_params=_params("arbitrary", "arbitrary"),
        name="hyena_filter_latent",
    )(zt, t01, deltas, w1t, b1, freq, w2t, b2, w3t, w1f, tw, w2s, w2d)


def _hfilter_ctx_kernel(zt_ref, t_ref, dl_ref, w1t_ref, b1_ref, fr_ref, w2t_ref, b2_ref, w3t_ref,
                        cf_ref, sf_ref, o_ref):
    l = zt_ref.shape[1]
    h2 = _filter_mlp(zt_ref[...], w1t_ref[...], b1_ref[...], fr_ref[...], w2t_ref[...], b2_ref[...])
    ch = w3t_ref.shape[1]
    w3 = w3t_ref[...].reshape(2 * ch, HY_HID)
    dl = dl_ref[...]
    hv = _dot_hi(w3, h2) * jnp.exp(-(jnp.concatenate([dl, dl], 0) * t_ref[...]))
    tot = jnp.sum(jnp.abs(hv), axis=1, keepdims=True)
    inv = 1.0 / (tot[0:ch] + tot[ch:])
    f = hv[0:ch] * inv
    lane = lax.broadcasted_iota(jnp.int32, (ch, l), 1)
    bk = jnp.where(lane == 0, 0.0, hv[ch:] * inv)
    o_ref[:, 0:2 * l] = _dot((f + bk).astype(BF16), cf_ref[...].astype(BF16))
    o_ref[:, 2 * l:] = _dot((f - bk).astype(BF16), sf_ref[...].astype(BF16))


def _hfilter_ctx(l, w1t, b1, freq, w2t, b2, w3t):
    zt, t01, deltas = (jnp.asarray(m) for m in _const_filter_features(l))
    cfm, nsf = (jnp.asarray(m) for m in _const_dense(l)[2:4])
    orders = w3t.shape[1]
    return pl.pallas_call(
        _hfilter_ctx_kernel,
        grid=(orders,),
        in_specs=[_resident(zt.shape), _resident(t01.shape), _resident(deltas.shape),
                  _resident(w1t.shape), _resident(b1.shape), _resident(freq.shape),
                  _resident(w2t.shape), _resident(b2.shape),
                  pl.BlockSpec((2, None, HY_CH, HY_HID), lambda o: (0, o, 0, 0)),
                  _resident(cfm.shape), _resident(nsf.shape)],
        out_specs=pl.BlockSpec((None, HY_CH, 4 * l), lambda o: (o, 0, 0)),
        out_shape=jax.ShapeDtypeStruct((orders, HY_CH, 4 * l), F32),
        compiler_params=_params("parallel"),
        name="hyena_filter_ctx",
    )(zt, t01, deltas, w1t, b1, freq, w2t, b2, w3t, cfm, nsf)


def _short_conv(x, cw, cb):
    l = x.shape[1]
    lane = lax.broadcasted_iota(jnp.int32, x.shape, 1)
    prev = jnp.where(lane == 0, 0.0, pltpu.roll(x, 1, 1))
    nxt = jnp.where(lane == l - 1, 0.0, pltpu.roll(x, l - 1, 1))
    return cw[:, 0:1] * prev + cw[:, 1:2] * x + cw[:, 2:3] * nxt + cb


def _short_conv_tile(x, w0, w1, w2, bias):
    a_len = x.shape[0]
    lane = lax.broadcasted_iota(jnp.int32, x.shape, 1)
    row = lax.broadcasted_iota(jnp.int32, x.shape, 0)
    xl = pltpu.roll(x, 1, 1)
    prev = jnp.where(lane == 0, jnp.where(row == 0, 0.0, pltpu.roll(xl, 1, 0)), xl)
    xr = pltpu.roll(x, LANES - 1, 1)
    nxt = jnp.where(lane == LANES - 1, jnp.where(row == a_len - 1, 0.0, pltpu.roll(xr, a_len - 1, 0)), xr)
    return w0 * prev + w1 * x + w2 * nxt + bias


def _hconv_kernel(cw_ref, cb_ref, sk_ref, x1_ref, x2_ref, v_ref, hf_ref,
                  w1_ref, tw_ref, w2_ref, w2i_ref, v1_ref, o_ref, g1_ref, g2_ref, vb_ref, ys_ref, us_ref, *, group):
    nb, ct, a_len, _ = vb_ref.shape
    k1n = 2 * a_len
    nch3 = cb_ref.shape[0]
    c0 = pl.program_id(0) * ct
    w1 = w1_ref[...].astype(BF16)
    w2 = w2_ref[...].astype(BF16)
    w2i = w2i_ref[...].astype(BF16)
    v1 = v1_ref[...].astype(BF16)
    tr, ti = tw_ref[0], tw_ref[1]

    for src, dst in ((x1_ref, g1_ref), (x2_ref, g2_ref), (v_ref, vb_ref)):
        for bi in range(nb):
            dst[bi] = src[bi].reshape(ct, a_len, LANES)

    def sconv(x, ch):
        return _short_conv_tile(x, cw_ref[ch], cw_ref[nch3 + ch], cw_ref[2 * nch3 + ch], cb_ref[ch])

    def prep(c, carry):
        for bi in range(nb):
            vb_ref[bi, c] = sconv(vb_ref[bi, c], 2 * HY_CH + c0 + c)
        return carry

    lax.fori_loop(0, ct, prep, 0, unroll=4)

    for order, gate_ref in enumerate((g1_ref, g2_ref)):
        def stage1(p, carry):
            c = 2 * p
            x = jnp.concatenate([jnp.concatenate([vb_ref[0, c + h], vb_ref[1, c + h]], 0) for h in range(2)], 1)
            y = _dot(w1, x.astype(BF16))
            for h in range(2):
                yh = y[:, h * LANES:(h + 1) * LANES]
                _twiddle_store(ys_ref, c + h, yh[0:k1n], yh[k1n:], tr, ti)
            return carry

        lax.fori_loop(0, ct // 2, stage1, 0, unroll=8)

        def middle(g, carry):
            cs = pl.ds(g * group, group)
            z = _dot(ys_ref[cs].reshape(group * k1n, 2 * LANES), w2)
            h = hf_ref[order, cs].reshape(group * k1n, 2 * LANES)
            zr, zi = z[:, 0:LANES], z[:, LANES:]
            hr, hi = h[:, 0:LANES], h[:, LANES:]
            p = jnp.concatenate([zr * hr - zi * hi, zr * hi + zi * hr], 1).astype(BF16)
            u = _dot(p, w2i).reshape(group, k1n, 2 * LANES)
            ur, ui = u[:, :, 0:LANES], u[:, :, LANES:]
            us_ref[cs, :, 0:LANES] = (ur * tr + ui * ti).astype(BF16)
            us_ref[cs, :, LANES:] = (ui * tr - ur * ti).astype(BF16)
            return carry

        lax.fori_loop(0, ct // group, middle, 0, unroll=2)

        def stage1_inv(c, carry):
            q = _dot(v1, us_ref[c])
            conv = (q[0:a_len, 0:LANES] - q[a_len:, LANES:], q[a_len:, 0:LANES] + q[0:a_len, LANES:])
            skip = sk_ref[order * HY_CH + c0 + c]
            for bi in range(nb):
                gate = sconv(gate_ref[bi, c], order * HY_CH + c0 + c)
                vb_ref[bi, c] = gate * (conv[bi] + vb_ref[bi, c] * skip)
            return carry

        lax.fori_loop(0, ct, stage1_inv, 0, unroll=8)

    for bi in range(nb):
        o_ref[bi] = vb_ref[bi].reshape(ct, a_len * LANES)


def _hconv_latent(ph, hf, conv_w, conv_b, skip, ct=32, group=16):
    b, nch3, l = ph.shape
    assert b == 2, "the two batch entries ride as real/imaginary parts"
    a_len = l // LANES
    k1n = 2 * a_len
    w1, _, tw, w2, _, _, w2i, v1 = (jnp.asarray(m) for m in _const_hyena(a_len))
    nblk = HY_CH // ct
    smem = pl.BlockSpec(memory_space=pltpu.SMEM)
    part = lambda off: pl.BlockSpec((b, ct, l), lambda c: (0, c + off * nblk, 0))
    scratch = pltpu.VMEM((b, ct, a_len, LANES), F32)
    return pl.pallas_call(
        functools.partial(_hconv_kernel, group=group),
        grid=(nblk,),
        in_specs=[smem, smem, smem, part(0), part(1), part(2),
                  pl.BlockSpec((2, ct, k1n, 2 * LANES), lambda c: (0, c, 0, 0)),
                  _resident(w1.shape), _resident(tw.shape), _resident(w2.shape),
                  _resident(w2i.shape), _resident(v1.shape)],
        out_specs=pl.BlockSpec((b, ct, l), lambda c: (0, c, 0)),
        out_shape=jax.ShapeDtypeStruct((b, HY_CH, l), F32),
        scratch_shapes=[scratch, scratch, scratch,
                        pltpu.VMEM((ct, k1n, 2 * LANES), BF16),
                        pltpu.VMEM((ct, k1n, 2 * LANES), BF16)],
        compiler_params=_params("parallel"),
        name="hyena_conv_latent",
    )(conv_w.reshape(-1), conv_b, skip.reshape(-1), ph, ph, ph, hf, w1, tw, w2, w2i, v1)


def _hconv_ctx_kernel(cw_ref, cb_ref, sk_ref, p_ref, hf_ref, fwd_ref, inv_ref, o_ref):
    nb, _, l = p_ref.shape
    ch = HY_CH
    z = [_short_conv(p_ref[bi], cw_ref[...], cb_ref[...]) for bi in range(nb)]
    fwd = fwd_ref[...].astype(BF16)
    inv = inv_ref[...].astype(BF16)
    cur = [zz[2 * ch:] for zz in z]
    for order in range(2):
        spec = _dot(jnp.concatenate(cur, 1).astype(BF16), fwd)
        h = hf_ref[order]
        zr, zi, hr, hi = spec[:, 0:2 * l], spec[:, 2 * l:], h[:, 0:2 * l], h[:, 2 * l:]
        prod = jnp.concatenate([zr * hr - zi * hi, zr * hi + zi * hr], 1).astype(BF16)
        y = _dot(prod, inv)
        sk = sk_ref[:, order:order + 1]
        cur = [z[bi][order * ch:(order + 1) * ch] * (y[:, bi * l:(bi + 1) * l] + cur[bi] * sk) for bi in range(nb)]
    for bi in range(nb):
        o_ref[bi] = cur[bi]


def _hconv_ctx(ph, hf, conv_w, conv_b, skip):
    b, nch3, l = ph.shape
    assert b == 2
    fwd, inv = (jnp.asarray(m) for m in _const_dense(l)[4:6])
    cw = jnp.pad(conv_w.T, ((0, 0), (0, 5)))
    return pl.pallas_call(
        _hconv_ctx_kernel,
        in_specs=[pl.BlockSpec(cw.shape, lambda: (0, 0)),
                  pl.BlockSpec((nch3, 1), lambda: (0, 0)),
                  pl.BlockSpec((HY_CH, 2), lambda: (0, 0)),
                  pl.BlockSpec(ph.shape, lambda: (0, 0, 0)),
                  pl.BlockSpec(hf.shape, lambda: (0, 0, 0)),
                  pl.BlockSpec(fwd.shape, lambda: (0, 0)),
                  pl.BlockSpec(inv.shape, lambda: (0, 0))],
        out_specs=pl.BlockSpec((b, HY_CH, l), lambda: (0, 0, 0)),
        out_shape=jax.ShapeDtypeStruct((b, HY_CH, l), F32),
        compiler_params=pltpu.CompilerParams(vmem_limit_bytes=VMEM_LIMIT_BYTES),
        name="hyena_conv_ctx",
    )(cw, conv_b.reshape(-1, 1), skip.T, ph, hf, fwd, inv)


def _mix_ffn_kernel(*refs, n_mix, final, ff_chunk):
    d = D_MODEL
    h_ref, y_refs = refs[0], refs[1:1 + n_mix]
    wo_ref, mod_ref, g2_ref, wg_ref, wu_ref, wd_ref = refs[1 + n_mix:7 + n_mix]
    rest = refs[7 + n_mix:]
    y, row = None, 0
    for y_ref in y_refs:
        rows = y_ref.shape[0]
        part = lax.dot_general(y_ref[...].astype(BF16), wo_ref[row:row + rows, :], _TN, preferred_element_type=F32)
        y = part if y is None else y + part
        row += rows
    if final:
        fg_ref, o_ref = rest
    else:
        (o_ref,) = rest
    h1 = h_ref[...] + mod_ref[:, 2 * d:3 * d] * y
    u = _norm_mod(h1, g2_ref[...], mod_ref[:, 3 * d:4 * d], mod_ref[:, 4 * d:5 * d]).astype(BF16)
    acc = jnp.zeros(h1.shape, F32)
    for j in range(D_FF // ff_chunk):
        cols = slice(j * ff_chunk, (j + 1) * ff_chunk)
        act = (_silu(_dot(u, wg_ref[:, cols])) * _dot(u, wu_ref[:, cols])).astype(BF16)
        acc = acc + _dot(act, wd_ref[cols, :])
    h2 = h1 + mod_ref[:, 5 * d:6 * d] * acc
    if final:
        h2 = _rms(h2, fg_ref[...])
    o_ref[...] = h2


def _layer_resident(stacked, layer):
    nd = stacked.ndim - 1
    return pl.BlockSpec((None,) + stacked.shape[1:], lambda *_: (layer,) + (0,) * nd, pipeline_mode=pl.Buffered(1))


def _mix_ffn(h, ys, wo, mod3, g2, wg, wu, wd, layer, final_g=None, tl=512, ff_chunk=256):
    b, l, d = h.shape
    tl = min(tl, l)
    tok = pl.BlockSpec((None, tl, d), lambda i, t: (i, t, 0))
    y_specs = [pl.BlockSpec((None, y.shape[1], tl), lambda i, t: (i, 0, t)) for y in ys]
    in_specs = ([tok] + y_specs + [_resident(wo.shape), pl.BlockSpec((None, 1, 6 * d), lambda i, t: (i, 0, 0)),
                                    pl.BlockSpec((1, d), lambda i, t: (0, 0)),
                                    _layer_resident(wg, layer), _layer_resident(wu, layer),
                                    _layer_resident(wd, layer)])
    args = [h, *ys, wo, mod3, g2, wg, wu, wd]
    if final_g is not None:
        in_specs.append(pl.BlockSpec((1, d), lambda i, t: (0, 0)))
        args.append(final_g)
    return pl.pallas_call(
        functools.partial(_mix_ffn_kernel, n_mix=len(ys), final=final_g is not None, ff_chunk=ff_chunk),
        grid=(b, l // tl),
        in_specs=in_specs,
        out_specs=tok,
        out_shape=jax.ShapeDtypeStruct((b, l, d), F32),
        compiler_params=_params("parallel", "parallel"),
        name="mix_ffn_final" if final_g is not None else "mix_ffn",
    )(*args)


def _qkv_kernel(x_ref, g_ref, mod_ref, w_ref, q_ref, k_ref, v_ref):
    d = D_MODEL
    u = _norm_mod(x_ref[...], g_ref[...], mod_ref[:, 0:d], mod_ref[:, d:2 * d]).astype(BF16)
    q_ref[...] = (_dot(u, w_ref[:, 0:d]) * (NA_HEAD_DIM ** -0.5)).astype(BF16)
    k_ref[...] = _dot(u, w_ref[:, d:2 * d]).astype(BF16)
    v_ref[...] = _dot(u, w_ref[:, 2 * d:]).astype(BF16)


def _qkv(h, g, mod3, w, tl=1024):
    b, l, d = h.shape
    tl = min(tl, l)
    tok = pl.BlockSpec((None, tl, d), lambda i, t: (i, t, 0))
    return pl.pallas_call(
        _qkv_kernel,
        grid=(b, l // tl),
        in_specs=[tok, pl.BlockSpec((1, d), lambda i, t: (0, 0)),
                  pl.BlockSpec((None, 1, 6 * d), lambda i, t: (i, 0, 0)), _resident(w.shape)],
        out_specs=[tok, tok, tok],
        out_shape=[jax.ShapeDtypeStruct((b, l, d), BF16)] * 3,
        compiler_params=_params("parallel", "parallel"),
        name="qkv_proj",
    )(h, g, mod3, w)


NA_QROWS = 4
NA_KROWS = 12
NA_PATTERNS = 3


def _na_key_start(r0, grid_rows):
    return jnp.minimum(jnp.clip(r0 - NA_KH // 2, 0, grid_rows - NA_KH), grid_rows - NA_KROWS)


def _na_pattern(pat, i, j):
    if pat == 0:
        return j - i + NA_KH - 1, j < NA_KH
    if pat == 1:
        return j - i + NA_KH // 2 - 1, i <= j < i + NA_KH
    return j - i - 1, j >= NA_KROWS - NA_KH


def _bias_blocks_kernel(rpb_ref, o_ref):
    pair = pl.program_id(0)
    ndr, ndc = 2 * NA_KH - 1, 2 * NA_KW - 1
    kcol = lax.broadcasted_iota(jnp.int32, (GRID_W, LANES), 0)
    lane = lax.broadcasted_iota(jnp.int32, (GRID_W, LANES), 1)
    q = lane % GRID_W
    upper = lane >= GRID_W
    wstart = jnp.clip(q - NA_KW // 2, 0, GRID_W - NA_KW)
    ok = (kcol >= wstart) & (kcol < wstart + NA_KW)
    dc = kcol - q + (NA_KW - 1)
    neg = jnp.full((GRID_W, LANES), NEG_BIG, F32)
    for e in range(2):
        base = (2 * pair + e) * ndr * ndc
        tiles = []
        for dr in range(ndr):
            acc = jnp.zeros((GRID_W, LANES), F32)
            for dd in range(ndc):
                acc = jnp.where(dc == dd, rpb_ref[base + dr * ndc + dd], acc)
            tiles.append(jnp.where(ok, acc, NEG_BIG))
        for pat in range(NA_PATTERNS):
            for j in range(NA_KROWS):
                for p in range(NA_QROWS // 2):
                    halves = []
                    for i in (2 * p, 2 * p + 1):
                        dr, visible = _na_pattern(pat, i, j)
                        halves.append(tiles[dr] if visible else neg)
                    o_ref[e, pat, j * GRID_W:(j + 1) * GRID_W, p * LANES:(p + 1) * LANES] = jnp.where(
                        upper, halves[1], halves[0])


def _bias_blocks(rpb):
    heads = rpb.shape[0]
    nq, nk = NA_QROWS * GRID_W, NA_KROWS * GRID_W
    return pl.pallas_call(
        _bias_blocks_kernel,
        grid=(heads // 2,),
        in_specs=[pl.BlockSpec(memory_space=pltpu.SMEM)],
        out_specs=pl.BlockSpec((None, 2, NA_PATTERNS, nk, nq), lambda p: (p, 0, 0, 0, 0)),
        out_shape=jax.ShapeDtypeStruct((heads // 2, 2, NA_PATTERNS, nk, nq), F32),
        compiler_params=_params("parallel"),
        name="na_bias_blocks",
    )(rpb.reshape(-1))


def _na_kernel(q_ref, k_ref, v_ref, kc_ref, vc_ref, bias_ref, o_ref, *, blocks_per_step, grid_rows):
    nq, nk = NA_QROWS * GRID_W, NA_KROWS * GRID_W
    lane = lax.broadcasted_iota(jnp.int32, (nq, LANES), 1)
    row = lax.broadcasted_iota(jnp.int32, (LANES, nq), 0)
    kc = kc_ref[...]
    vc = vc_ref[...]
    step = pl.program_id(2)
    for i in range(blocks_per_step):
        r0 = (step * blocks_per_step + i) * NA_QROWS
        pat = jnp.where(r0 == 0, 0, jnp.where(r0 == grid_rows - NA_QROWS, 2, 1))
        koff = pl.multiple_of(_na_key_start(r0, grid_rows) * GRID_W, GRID_W)
        q = q_ref[i * nq:(i + 1) * nq, :]
        ku = k_ref[pl.ds(koff, nk), :]
        vt = jnp.concatenate([v_ref[pl.ds(koff, nk), :], vc], 0).T
        vt = jnp.concatenate([vt, jnp.ones((16, vt.shape[1]), vt.dtype)], 0)
        outs = []
        for e in range(2):
            head = (lane >= NA_HEAD_DIM) if e else (lane < NA_HEAD_DIM)
            qh = jnp.where(head, q, jnp.zeros_like(q))
            sl = lax.dot_general(ku, qh, _NT, preferred_element_type=F32) + bias_ref[e, pat]
            sc = lax.dot_general(kc, qh, _NT, preferred_element_type=F32)
            s = jnp.concatenate([sl, sc], 0)
            p = jnp.exp((s - jnp.max(s, axis=0, keepdims=True)).astype(BF16))
            o = _dot(vt, p)
            outs.append(o[0:LANES] / o[LANES:LANES + 1])
        o_ref[:, i * nq:(i + 1) * nq] = jnp.where(row < NA_HEAD_DIM, outs[0], outs[1]).astype(o_ref.dtype)


def _na(q, k, v, kc, vc, bias_blocks, blocks_per_step=16):
    b, l, d = q.shape
    lc = kc.shape[1]
    grid_rows = l // GRID_W
    blocks_per_step = min(blocks_per_step, grid_rows // NA_QROWS)
    assert grid_rows % (NA_QROWS * blocks_per_step) == 0 and grid_rows >= NA_KROWS
    npairs = d // LANES
    tq = blocks_per_step * NA_QROWS * GRID_W
    return pl.pallas_call(
        functools.partial(_na_kernel, blocks_per_step=blocks_per_step, grid_rows=grid_rows),
        grid=(npairs, b, l // tq),
        in_specs=[pl.BlockSpec((None, tq, LANES), lambda p, i, t: (i, t, p)),
                  pl.BlockSpec((None, l, LANES), lambda p, i, t: (i, 0, p)),
                  pl.BlockSpec((None, l, LANES), lambda p, i, t: (i, 0, p)),
                  pl.BlockSpec((None, lc, LANES), lambda p, i, t: (i, 0, p)),
                  pl.BlockSpec((None, lc, LANES), lambda p, i, t: (i, 0, p)),
                  pl.BlockSpec((None,) + bias_blocks.shape[1:], lambda p, i, t: (p, 0, 0, 0, 0))],
        out_specs=pl.BlockSpec((None, LANES, tq), lambda p, i, t: (i, p, t)),
        out_shape=jax.ShapeDtypeStruct((b, d, l), BF16),
        compiler_params=_params("parallel", "parallel", "parallel"),
        name="neighbourhood_attention",
    )(q, k, v, kc, vc, bias_blocks)


def kernel(x, c, ctx, c_ctx, ada_w, ada_b, norm1_g, norm2_g, ffn_w_gate, ffn_w_up, ffn_w_down, ab_w_in, ab_w_out, hy_conv_w, hy_conv_b, hy_f_w1, hy_f_b1, hy_f_freq, hy_f_w2, hy_f_b2, hy_f_w3, hy_bias, na_w_qkv, na_w_out, na_rpb, final_g):
    b, l, d = x.shape
    lc = ctx.shape[1]
    assert b == 2 and d == D_MODEL and l % 1024 == 0 and ada_w.shape[0] == 2

    cond8 = jnp.zeros((8, d), F32).at[0:b].set(c).at[b].set(c_ctx)
    mod = _ada(cond8, ada_w, ada_b)
    mod_lat = [mod[i, 0:b].reshape(b, 1, 6 * d) for i in range(2)]
    mod_ctx = [jnp.broadcast_to(mod[i, b:b + 1], (b, 6 * d)).reshape(b, 1, 6 * d) for i in range(2)]
    n1 = [norm1_g[i].reshape(1, d) for i in range(2)]
    n2 = [norm2_g[i].reshape(1, d) for i in range(2)]
    ffn = (ffn_w_gate.astype(BF16), ffn_w_up.astype(BF16), ffn_w_down.astype(BF16))

    w_in_t = ab_w_in[0].T.astype(BF16)
    w_out = ab_w_out[0].astype(BF16)
    w1t = jnp.pad(hy_f_w1[0].T, ((0, 0), (0, LANES - HY_EMB)))
    col = lambda vec: vec.reshape(-1, 1)
    mlp = (w1t, col(hy_f_b1[0]), col(hy_f_freq[0]), hy_f_w2[0].T, col(hy_f_b2[0]),
           hy_f_w3[0].T.reshape(2, 2, HY_CH, HY_HID))

    zf, ph = _inproj(x, n1[0], mod_lat[0], w_in_t)
    y_f = _fourier_latent(zf)
    hf = _hfilter_latent(l, *mlp)
    y_h = _hconv_latent(ph, hf, hy_conv_w[0], hy_conv_b[0], hy_bias[0]).reshape(b, HY_CH, l)
    h_lat = _mix_ffn(x, (y_f, y_h), w_out, mod_lat[0], n2[0], *ffn, layer=0)

    zfc, phc = _inproj(ctx, n1[0], mod_ctx[0], w_in_t)
    y_fc = _fourier_ctx(zfc)
    hfc = _hfilter_ctx(lc, *mlp)
    y_hc = _hconv_ctx(phc, hfc, hy_conv_w[0], hy_conv_b[0], hy_bias[0])
    h_ctx = _mix_ffn(ctx, (y_fc, y_hc), w_out, mod_ctx[0], n2[0], *ffn, layer=0)

    w_qkv = na_w_qkv[0].astype(BF16)
    q, k, v = _qkv(h_lat, n1[1], mod_lat[1], w_qkv)
    _, kc, vc = _qkv(h_ctx, n1[1], mod_ctx[1], w_qkv)
    attn = _na(q, k, v, kc, vc, _bias_blocks(na_rpb[0]))
    return _mix_ffn(h_lat, (attn,), na_w_out[0].astype(BF16), mod_lat[1], n2[1], *ffn, layer=1,
                    final_g=final_g.reshape(1, d))
```

```python
import functools
import math

import numpy as np
import jax
import jax.numpy as jnp
from jax import lax
from jax.experimental import pallas as pl
from jax.experimental.pallas import tpu as pltpu

F32 = jnp.float32
BF16 = jnp.bfloat16

D_MODEL = 1024
FOURIER_CH = 512
FOURIER_GROUPS = 4
GROUP_DIM = 128
HY_CH = 512
HY_BANDS = 16
HY_EMB = 33
HY_HID = 64
HY_FAST_DECAY = 0.3
HY_SLOW_DECAY = 1.5
HY_DECAY_TARGET = 1e-2
D_FF = 2816
NA_HEADS = 16
NA_HEAD_DIM = 64
NA_KH = 8
NA_KW = 16
GRID_W = 64
RMS_EPS = 1e-6
NEG_BIG = -1e30

LANES = 128
VMEM_LIMIT_BYTES = 56 * 1024 * 1024

_NT = (((1,), (1,)), ((), ()))
_TN = (((0,), (0,)), ((), ()))


def _params(*sem):
    return pltpu.CompilerParams(dimension_semantics=sem, vmem_limit_bytes=VMEM_LIMIT_BYTES)


def _dot(a, b):
    return jnp.dot(a, b, preferred_element_type=F32)


def _dot_hi(a, b):
    return jnp.dot(a, b, preferred_element_type=F32, precision=lax.Precision.HIGHEST)


def _resident(shape):
    nd = len(shape)
    return pl.BlockSpec(shape, lambda *_: (0,) * nd, pipeline_mode=pl.Buffered(1))


def _rms(x, g):
    ms = jnp.mean(x * x, axis=-1, keepdims=True)
    return x * lax.rsqrt(ms + RMS_EPS) * g


def _norm_mod(x, g, shift, scale):
    return _rms(x, g) * (1.0 + scale) + shift


def _silu(x):
    return x * jax.nn.sigmoid(x)


def _cs(num, den):
    ang = 2.0 * np.pi * (np.asarray(num, np.float64) % den) / den
    return np.cos(ang), np.sin(ang)


@functools.lru_cache(maxsize=None)
def _const_channel_dft():
    c, s = _cs(np.outer(np.arange(GROUP_DIM), np.arange(GROUP_DIM)), GROUP_DIM)
    return (np.concatenate([c, -s], 0) / math.sqrt(GROUP_DIM)).astype(np.float32)


@functools.lru_cache(maxsize=None)
def _const_fourier(a_len):
    n = a_len * LANES
    c, s = _cs(np.outer(np.arange(a_len), np.arange(a_len)), a_len)
    w1 = np.block([[c, s], [-s, c]])
    tc, ts = _cs(np.outer(np.arange(a_len), np.arange(LANES)), n)
    tw = np.stack([tc, -ts])
    c2, s2 = _cs(np.outer(np.arange(LANES), np.arange(LANES)), LANES)
    w2 = np.concatenate([c2, s2], 0) / math.sqrt(n)
    r = LANES // a_len
    perm = np.array([r * m + h for h in range(r) for m in range(a_len)])
    w2 = w2[:, perm]
    return w1.astype(np.float32), tw.astype(np.float32), w2.astype(np.float32)


@functools.lru_cache(maxsize=None)
def _const_hyena(a_len):
    k1n = 2 * a_len
    n = k1n * LANES
    c, s = _cs(np.outer(np.arange(k1n), np.arange(a_len)), k1n)
    w1 = np.block([[c, s], [-s, c]])
    w1f = np.concatenate([c, -s], 0)
    tc, ts = _cs(np.outer(np.arange(k1n), np.arange(LANES)), n)
    tw = np.stack([tc, -ts])
    c2, s2 = _cs(np.outer(np.arange(LANES), np.arange(LANES)), LANES)
    w2 = np.block([[c2, -s2], [s2, c2]])
    w2s = np.concatenate([c2, s2], 0)
    w2d = np.concatenate([-s2, c2], 0)
    w2inv = np.block([[c2, s2], [-s2, c2]])
    v1 = np.concatenate([c.T, s.T], 0) / n
    f = lambda m: m.astype(np.float32)
    return f(w1), f(w1f), f(tw), f(w2), f(w2s), f(w2d), f(w2inv), f(v1)


@functools.lru_cache(maxsize=None)
def _const_dense(l_len):
    n = 2 * l_len
    cl, sl = _cs(np.outer(np.arange(l_len), np.arange(l_len)), l_len)
    four_c, four_s = cl / math.sqrt(l_len), sl / math.sqrt(l_len)
    cf, sf = _cs(np.outer(np.arange(l_len), np.arange(n)), n)
    fwd = np.block([[cf, -sf], [sf, cf]])
    inv = np.block([[cf.T, sf.T], [-sf.T, cf.T]]) / n
    f = lambda m: m.astype(np.float32)
    return f(four_c), f(four_s), f(cf), f(-sf), f(fwd), f(inv)


@functools.lru_cache(maxsize=None)
def _const_filter_features(l_len):
    t01 = np.linspace(0.0, 1.0, l_len)
    bands = np.linspace(1e-4, HY_BANDS - 1, HY_BANDS)
    ang = (2.0 * np.pi / l_len) * np.arange(l_len)[:, None] * bands[None, :]
    z = np.concatenate([t01[:, None], np.cos(ang), -np.sin(ang)], -1)
    zt = np.zeros((LANES, l_len), np.float64)
    zt[:HY_EMB] = z.T
    max_decay = math.log(HY_DECAY_TARGET) / HY_FAST_DECAY
    min_decay = math.log(HY_DECAY_TARGET) / HY_SLOW_DECAY
    deltas = np.abs(np.linspace(min_decay, max_decay, HY_CH))
    return zt.astype(np.float32), t01[None, :].astype(np.float32), deltas[:, None].astype(np.float32)


def _chunk_time(m):
    rows, l_len = m.shape
    return np.ascontiguousarray(m.reshape(rows, l_len // LANES, LANES).transpose(1, 0, 2))


def _ada_kernel(c_ref, w_ref, b_ref, o_ref):
    s = _silu(c_ref[...]).astype(BF16)
    o_ref[...] = _dot(s, w_ref[...].astype(BF16)) + b_ref[...]


def _ada(cond8, ada_w, ada_b):
    depth, d, n6 = ada_w.shape
    tn = 1536
    return pl.pallas_call(
        _ada_kernel,
        grid=(depth, n6 // tn),
        in_specs=[pl.BlockSpec((8, d), lambda i, j: (0, 0)),
                  pl.BlockSpec((None, d, tn), lambda i, j: (i, 0, j)),
                  pl.BlockSpec((None, 1, tn), lambda i, j: (i, 0, j))],
        out_specs=pl.BlockSpec((None, 8, tn), lambda i, j: (i, 0, j)),
        out_shape=jax.ShapeDtypeStruct((depth, 8, n6), F32),
        compiler_params=_params("parallel", "parallel"),
        name="ada_mod",
    )(cond8, ada_w, ada_b.reshape(depth, 1, n6))


def _inproj_kernel(x_ref, g_ref, mod_ref, wt_ref, cdft_ref, zf_ref, ph_ref):
    d = D_MODEL
    u = _norm_mod(x_ref[...], g_ref[...], mod_ref[:, 0:d], mod_ref[:, d:2 * d]).astype(BF16)
    cdft = cdft_ref[...].astype(BF16)
    for g in range(FOURIER_GROUPS):
        rows = slice(g * GROUP_DIM, (g + 1) * GROUP_DIM)
        pt = lax.dot_general(wt_ref[rows, :], u, _NT, preferred_element_type=F32)
        z = _dot(cdft, pt.astype(BF16))
        zf_ref[0, rows, :] = z[0:GROUP_DIM]
        zf_ref[1, rows, :] = z[GROUP_DIM:]
    rb = 512
    for r0 in range(FOURIER_CH, wt_ref.shape[0], rb):
        ph_ref[r0 - FOURIER_CH:r0 - FOURIER_CH + rb, :] = lax.dot_general(
            wt_ref[r0:r0 + rb, :], u, _NT, preferred_element_type=F32)


def _inproj(x, g, mod3, w_in_t, tl=1024):
    b, l, d = x.shape
    tl = min(tl, l)
    nh = w_in_t.shape[0] - FOURIER_CH
    return pl.pallas_call(
        _inproj_kernel,
        grid=(b, l // tl),
        in_specs=[pl.BlockSpec((None, tl, d), lambda i, t: (i, t, 0)),
                  pl.BlockSpec((1, d), lambda i, t: (0, 0)),
                  pl.BlockSpec((None, 1, 6 * d), lambda i, t: (i, 0, 0)),
                  _resident(w_in_t.shape),
                  _resident((2 * GROUP_DIM, GROUP_DIM))],
        out_specs=[pl.BlockSpec((None, 2, FOURIER_CH, tl), lambda i, t: (i, 0, 0, t)),
                   pl.BlockSpec((None, nh, tl), lambda i, t: (i, 0, t))],
        out_shape=[jax.ShapeDtypeStruct((b, 2, FOURIER_CH, l), F32),
                   jax.ShapeDtypeStruct((b, nh, l), F32)],
        compiler_params=_params("parallel", "parallel"),
        name="inproj",
    )(x, g, mod3, w_in_t, jnp.asarray(_const_channel_dft()))


def _twiddle_store(ys_ref, c, yr, yi, tr, ti):
    ys_ref[c, :, 0:LANES] = (yr * tr - yi * ti).astype(ys_ref.dtype)
    ys_ref[c, :, LANES:] = (yr * ti + yi * tr).astype(ys_ref.dtype)


def _fourier_kernel(z_ref, w1_ref, tw_ref, w2_ref, o_ref, zs_ref, ys_ref):
    _, ct, a_len, _ = zs_ref.shape
    w1 = w1_ref[...].astype(BF16)
    tr, ti = tw_ref[0], tw_ref[1]
    for ri in range(2):
        zs_ref[ri] = z_ref[ri].reshape(ct, a_len, LANES)

    def stage1(p, carry):
        c = 2 * p
        x = jnp.concatenate([jnp.concatenate([zs_ref[0, c + h], zs_ref[1, c + h]], 0) for h in range(2)], 1)
        y = _dot(w1, x.astype(BF16))
        for h in range(2):
            yh = y[:, h * LANES:(h + 1) * LANES]
            _twiddle_store(ys_ref, c + h, yh[0:a_len], yh[a_len:], tr, ti)
        return carry

    lax.fori_loop(0, ct // 2, stage1, 0, unroll=8)
    y = ys_ref[...].reshape(ct * a_len, 2 * LANES)
    zr = _dot(y, w2_ref[...].astype(BF16))
    tiles = []
    for c in range(ct):
        t = zr[c * a_len:(c + 1) * a_len].T
        tiles.append(jnp.concatenate([t[h * a_len:(h + 1) * a_len] for h in range(LANES // a_len)], 1))
    o_ref[...] = jnp.stack(tiles, 0).reshape(ct, a_len * LANES)


def _fourier_latent(zf, ct=32):
    b, _, ch, l = zf.shape
    a_len = l // LANES
    w1, tw, w2 = (jnp.asarray(m) for m in _const_fourier(a_len))
    return pl.pallas_call(
        _fourier_kernel,
        grid=(b, ch // ct),
        in_specs=[pl.BlockSpec((None, 2, ct, l), lambda i, c: (i, 0, c, 0)),
                  _resident(w1.shape), _resident(tw.shape), _resident(w2.shape)],
        out_specs=pl.BlockSpec((None, ct, l), lambda i, c: (i, c, 0)),
        out_shape=jax.ShapeDtypeStruct((b, ch, l), F32),
        scratch_shapes=[pltpu.VMEM((2, ct, a_len, LANES), F32),
                        pltpu.VMEM((ct, a_len, 2 * LANES), BF16)],
        compiler_params=_params("parallel", "parallel"),
        name="fourier_latent",
    )(zf, w1, tw, w2)


def _fourier_ctx_kernel(z_ref, c_ref, s_ref, o_ref):
    zr = z_ref[0].astype(BF16)
    zi = z_ref[1].astype(BF16)
    o_ref[...] = _dot(zr, c_ref[...].astype(BF16)) + _dot(zi, s_ref[...].astype(BF16))


def _fourier_ctx(zf):
    b, _, ch, l = zf.shape
    four_c, four_s = (jnp.asarray(m) for m in _const_dense(l)[:2])
    return pl.pallas_call(
        _fourier_ctx_kernel,
        grid=(b,),
        in_specs=[pl.BlockSpec((None, 2, ch, l), lambda i: (i, 0, 0, 0)),
                  _resident((l, l)), _resident((l, l))],
        out_specs=pl.BlockSpec((None, ch, l), lambda i: (i, 0, 0)),
        out_shape=jax.ShapeDtypeStruct((b, ch, l), F32),
        compiler_params=_params("parallel"),
        name="fourier_ctx",
    )(zf, four_c, four_s)


def _filter_mlp(zt, w1t, b1, freq, w2t, b2):
    h = jnp.sin(freq * (_dot_hi(w1t, zt) + b1))
    return jnp.sin(freq * (_dot_hi(w2t, h) + b2))


@functools.lru_cache(maxsize=None)
def _const_filter_fft(a_len):
    k1n = 2 * a_len
    c, s = _cs(np.outer(np.arange(k1n), np.arange(k1n)), k1n)
    w1r = np.concatenate([c, -s], 0)
    perm = np.zeros((2, LANES, LANES))
    for b in range(1, LANES):
        perm[0, LANES - b, b] = 1.0
    perm[1, 0, 0] = 1.0
    return w1r.astype(np.float32), perm.astype(np.float32)


def _hfilter_kernel(zt_ref, t_ref, trev_ref, dl_ref, w1t_ref, b1_ref, fr_ref, w2t_ref, b2_ref, w3t_ref,
                    perm_ref, w1r_ref, tw_ref, w2_ref, o_ref, h2_ref, h2r_ref, xs_ref, ys_ref):
    cf = o_ref.shape[0]
    k1n = xs_ref.shape[1]
    a_len = k1n // 2
    l = a_len * LANES

    @pl.when((pl.program_id(0) == 0) & (pl.program_id(1) == 0))
    def _():
        step = min(l, 1024)
        for s in range(0, l, step):
            h2_ref[:, s:s + step] = _filter_mlp(zt_ref[:, s:s + step], w1t_ref[...], b1_ref[...], fr_ref[...],
                                                w2t_ref[...], b2_ref[...]).astype(h2_ref.dtype)
        flip, first = perm_ref[0].astype(BF16), perm_ref[1].astype(BF16)
        for a in range(a_len):
            src = a_len - 1 - a
            r = _dot(h2_ref[:, src * LANES:(src + 1) * LANES], flip)
            if a >= 1:
                r = r + _dot(h2_ref[:, (src + 1) * LANES:(src + 2) * LANES], first)
            h2r_ref[:, a * LANES:(a + 1) * LANES] = r.astype(h2r_ref.dtype)

    w3f = w3t_ref[0].astype(BF16)
    w3b = w3t_ref[1].astype(BF16)
    dl = dl_ref[...]
    f = _dot(w3f, h2_ref[...]) * jnp.exp(-(dl * t_ref[...]))
    lane = lax.broadcasted_iota(jnp.int32, (cf, l), 1)
    g = jnp.where(lane == 0, 0.0, _dot(w3b, h2r_ref[...]) * jnp.exp(-(dl * trev_ref[...])))
    b0 = _dot(w3b, h2_ref[:, 0:LANES])[:, 0:1]
    tot = (jnp.sum(jnp.abs(f), axis=1, keepdims=True) + jnp.sum(jnp.abs(g), axis=1, keepdims=True) + jnp.abs(b0))
    inv = 1.0 / tot
    xs_ref[...] = (jnp.concatenate([f, g], 1) * inv).reshape(cf, k1n, LANES).astype(xs_ref.dtype)

    w1r = w1r_ref[...].astype(BF16)
    tr, ti = tw_ref[0], tw_ref[1]

    def stage1(p, carry):
        c = 2 * p
        y = _dot(w1r, jnp.concatenate([xs_ref[c], xs_ref[c + 1]], 1))
        for h in range(2):
            yh = y[:, h * LANES:(h + 1) * LANES]
            _twiddle_store(ys_ref, c + h, yh[0:k1n], yh[k1n:], tr, ti)
        return carry

    lax.fori_loop(0, cf // 2, stage1, 0, unroll=8)
    o_ref[...] = _dot(ys_ref[...].reshape(cf * k1n, 2 * LANES), w2_ref[...].astype(BF16)).reshape(cf, k1n, 2 * LANES)


def _hfilter_latent(l, w1t, b1, freq, w2t, b2, w3t, cf=64):
    a_len = l // LANES
    k1n = 2 * a_len
    zt, t01, deltas = _const_filter_features(l)
    trev = np.concatenate([np.zeros((1, 1), np.float32), t01[:, :0:-1]], 1)
    zt, t01, trev, deltas = (jnp.asarray(m) for m in (zt, t01, trev, deltas))
    _, _, tw, w2, _, _, _, _ = (jnp.asarray(m) for m in _const_hyena(a_len))
    w1r, perm = (jnp.asarray(m) for m in _const_filter_fft(a_len))
    orders = w3t.shape[1]
    return pl.pallas_call(
        _hfilter_kernel,
        grid=(orders, HY_CH // cf),
        in_specs=[_resident(zt.shape), _resident(t01.shape), _resident(trev.shape),
                  pl.BlockSpec((cf, 1), lambda o, c: (c, 0)),
                  _resident(w1t.shape), _resident(b1.shape), _resident(freq.shape),
                  _resident(w2t.shape), _resident(b2.shape),
                  pl.BlockSpec((2, None, cf, HY_HID), lambda o, c: (0, o, c, 0)),
                  _resident(perm.shape), _resident(w1r.shape), _resident(tw.shape), _resident(w2.shape)],
        out_specs=pl.BlockSpec((None, cf, k1n, 2 * LANES), lambda o, c: (o, c, 0, 0)),
        out_shape=jax.ShapeDtypeStruct((orders, HY_CH, k1n, 2 * LANES), F32),
        scratch_shapes=[pltpu.VMEM((HY_HID, l), BF16),
                        pltpu.VMEM((HY_HID, l), BF16),
                        pltpu.VMEM((cf, k1n, LANES), BF16),
                        pltpu.VMEM((cf, k1n, 2 * LANES), BF16)],
        compiler_params=_params("arbitrary", "arbitrary"),
        name="hyena_filter_latent",
    )(zt, t01, trev, deltas, w1t, b1, freq, w2t, b2, w3t, perm, w1r, tw, w2)


def _hfilter_ctx_kernel(zt_ref, t_ref, dl_ref, w1t_ref, b1_ref, fr_ref, w2t_ref, b2_ref, w3t_ref,
                        cf_ref, sf_ref, o_ref):
    l = zt_ref.shape[1]
    h2 = _filter_mlp(zt_ref[...], w1t_ref[...], b1_ref[...], fr_ref[...], w2t_ref[...], b2_ref[...])
    ch = w3t_ref.shape[1]
    w3 = w3t_ref[...].reshape(2 * ch, HY_HID)
    dl = dl_ref[...]
    hv = _dot_hi(w3, h2) * jnp.exp(-(jnp.concatenate([dl, dl], 0) * t_ref[...]))
    tot = jnp.sum(jnp.abs(hv), axis=1, keepdims=True)
    inv = 1.0 / (tot[0:ch] + tot[ch:])
    f = hv[0:ch] * inv
    lane = lax.broadcasted_iota(jnp.int32, (ch, l), 1)
    bk = jnp.where(lane == 0, 0.0, hv[ch:] * inv)
    o_ref[:, 0:2 * l] = _dot((f + bk).astype(BF16), cf_ref[...].astype(BF16))
    o_ref[:, 2 * l:] = _dot((f - bk).astype(BF16), sf_ref[...].astype(BF16))


def _hfilter_ctx(l, w1t, b1, freq, w2t, b2, w3t):
    zt, t01, deltas = (jnp.asarray(m) for m in _const_filter_features(l))
    cfm, nsf = (jnp.asarray(m) for m in _const_dense(l)[2:4])
    orders = w3t.shape[1]
    return pl.pallas_call(
        _hfilter_ctx_kernel,
        grid=(orders,),
        in_specs=[_resident(zt.shape), _resident(t01.shape), _resident(deltas.shape),
                  _resident(w1t.shape), _resident(b1.shape), _resident(freq.shape),
                  _resident(w2t.shape), _resident(b2.shape),
                  pl.BlockSpec((2, None, HY_CH, HY_HID), lambda o: (0, o, 0, 0)),
                  _resident(cfm.shape), _resident(nsf.shape)],
        out_specs=pl.BlockSpec((None, HY_CH, 4 * l), lambda o: (o, 0, 0)),
        out_shape=jax.ShapeDtypeStruct((orders, HY_CH, 4 * l), F32),
        compiler_params=_params("parallel"),
        name="hyena_filter_ctx",
    )(zt, t01, deltas, w1t, b1, freq, w2t, b2, w3t, cfm, nsf)


def _short_conv(x, cw, cb):
    l = x.shape[1]
    lane = lax.broadcasted_iota(jnp.int32, x.shape, 1)
    prev = jnp.where(lane == 0, 0.0, pltpu.roll(x, 1, 1))
    nxt = jnp.where(lane == l - 1, 0.0, pltpu.roll(x, l - 1, 1))
    return cw[:, 0:1] * prev + cw[:, 1:2] * x + cw[:, 2:3] * nxt + cb


def _short_conv_tile(x, w0, w1, w2, bias):
    a_len = x.shape[0]
    lane = lax.broadcasted_iota(jnp.int32, x.shape, 1)
    row = lax.broadcasted_iota(jnp.int32, x.shape, 0)
    xl = pltpu.roll(x, 1, 1)
    prev = jnp.where(lane == 0, jnp.where(row == 0, 0.0, pltpu.roll(xl, 1, 0)), xl)
    xr = pltpu.roll(x, LANES - 1, 1)
    nxt = jnp.where(lane == LANES - 1, jnp.where(row == a_len - 1, 0.0, pltpu.roll(xr, a_len - 1, 0)), xr)
    return w0 * prev + w1 * x + w2 * nxt + bias


def _hconv_kernel(cw_ref, cb_ref, sk_ref, x1_ref, x2_ref, v_ref, hf_ref,
                  w1_ref, tw_ref, w2_ref, w2i_ref, v1_ref, o_ref, g1_ref, g2_ref, vb_ref, ys_ref, us_ref, *, group):
    nb, ct, a_len, _ = vb_ref.shape
    k1n = 2 * a_len
    nch3 = cb_ref.shape[0]
    c0 = pl.program_id(0) * ct
    w1 = w1_ref[...].astype(BF16)
    w2 = w2_ref[...].astype(BF16)
    w2i = w2i_ref[...].astype(BF16)
    v1 = v1_ref[...].astype(BF16)
    tr, ti = tw_ref[0], tw_ref[1]

    for src, dst in ((x1_ref, g1_ref), (x2_ref, g2_ref), (v_ref, vb_ref)):
        for bi in range(nb):
            dst[bi] = src[bi].reshape(ct, a_len, LANES)

    def sconv(x, ch):
        return _short_conv_tile(x, cw_ref[ch], cw_ref[nch3 + ch], cw_ref[2 * nch3 + ch], cb_ref[ch])

    for order, gate_ref in enumerate((g1_ref, g2_ref)):
        def stage1(p, carry):
            c = 2 * p
            if order == 0:
                for h in range(2):
                    for bi in range(nb):
                        vb_ref[bi, c + h] = sconv(vb_ref[bi, c + h], 2 * HY_CH + c0 + c + h)
            x = jnp.concatenate([jnp.concatenate([vb_ref[0, c + h], vb_ref[1, c + h]], 0) for h in range(2)], 1)
            y = _dot(w1, x.astype(BF16))
            for h in range(2):
                yh = y[:, h * LANES:(h + 1) * LANES]
                _twiddle_store(ys_ref, c + h, yh[0:k1n], yh[k1n:], tr, ti)
            return carry

        lax.fori_loop(0, ct // 2, stage1, 0, unroll=8)

        def middle(g, carry):
            cs = pl.ds(g * group, group)
            z = _dot(ys_ref[cs].reshape(group * k1n, 2 * LANES), w2)
            h = hf_ref[order, cs].reshape(group * k1n, 2 * LANES)
            zr, zi = z[:, 0:LANES], z[:, LANES:]
            hr, hi = h[:, 0:LANES], h[:, LANES:]
            p = jnp.concatenate([zr * hr - zi * hi, zr * hi + zi * hr], 1).astype(BF16)
            u = _dot(p, w2i).reshape(group, k1n, 2 * LANES)
            ur, ui = u[:, :, 0:LANES], u[:, :, LANES:]
            us_ref[cs, :, 0:LANES] = (ur * tr + ui * ti).astype(BF16)
            us_ref[cs, :, LANES:] = (ui * tr - ur * ti).astype(BF16)
            return carry

        lax.fori_loop(0, ct // group, middle, 0, unroll=2)

        def stage1_inv(c, carry):
            q = _dot(v1, us_ref[c])
            conv = (q[0:a_len, 0:LANES] - q[a_len:, LANES:], q[a_len:, 0:LANES] + q[0:a_len, LANES:])
            skip = sk_ref[order * HY_CH + c0 + c]
            for bi in range(nb):
                gate = sconv(gate_ref[bi, c], order * HY_CH + c0 + c)
                vb_ref[bi, c] = gate * (conv[bi] + vb_ref[bi, c] * skip)
            return carry

        lax.fori_loop(0, ct, stage1_inv, 0, unroll=8)

    for bi in range(nb):
        o_ref[bi] = vb_ref[bi].reshape(ct, a_len * LANES)


def _hconv_latent(ph, hf, conv_w, conv_b, skip, ct=32, group=16):
    b, nch3, l = ph.shape
    assert b == 2, "the two batch entries ride as real/imaginary parts"
    a_len = l // LANES
    k1n = 2 * a_len
    w1, _, tw, w2, _, _, w2i, v1 = (jnp.asarray(m) for m in _const_hyena(a_len))
    nblk = HY_CH // ct
    smem = pl.BlockSpec(memory_space=pltpu.SMEM)
    part = lambda off: pl.BlockSpec((b, ct, l), lambda c: (0, c + off * nblk, 0))
    scratch = pltpu.VMEM((b, ct, a_len, LANES), F32)
    return pl.pallas_call(
        functools.partial(_hconv_kernel, group=group),
        grid=(nblk,),
        in_specs=[smem, smem, smem, part(0), part(1), part(2),
                  pl.BlockSpec((2, ct, k1n, 2 * LANES), lambda c: (0, c, 0, 0)),
                  _resident(w1.shape), _resident(tw.shape), _resident(w2.shape),
                  _resident(w2i.shape), _resident(v1.shape)],
        out_specs=pl.BlockSpec((b, ct, l), lambda c: (0, c, 0)),
        out_shape=jax.ShapeDtypeStruct((b, HY_CH, l), F32),
        scratch_shapes=[scratch, scratch, scratch,
                        pltpu.VMEM((ct, k1n, 2 * LANES), BF16),
                        pltpu.VMEM((ct, k1n, 2 * LANES), BF16)],
        compiler_params=_params("parallel"),
        name="hyena_conv_latent",
    )(conv_w.reshape(-1), conv_b, skip.reshape(-1), ph, ph, ph, hf, w1, tw, w2, w2i, v1)


def _hconv_ctx_kernel(cw_ref, cb_ref, sk_ref, p_ref, hf_ref, fwd_ref, inv_ref, o_ref):
    nb, _, l = p_ref.shape
    ch = HY_CH
    z = [_short_conv(p_ref[bi], cw_ref[...], cb_ref[...]) for bi in range(nb)]
    fwd = fwd_ref[...].astype(BF16)
    inv = inv_ref[...].astype(BF16)
    cur = [zz[2 * ch:] for zz in z]
    for order in range(2):
        spec = _dot(jnp.concatenate(cur, 1).astype(BF16), fwd)
        h = hf_ref[order]
        zr, zi, hr, hi = spec[:, 0:2 * l], spec[:, 2 * l:], h[:, 0:2 * l], h[:, 2 * l:]
        prod = jnp.concatenate([zr * hr - zi * hi, zr * hi + zi * hr], 1).astype(BF16)
        y = _dot(prod, inv)
        sk = sk_ref[:, order:order + 1]
        cur = [z[bi][order * ch:(order + 1) * ch] * (y[:, bi * l:(bi + 1) * l] + cur[bi] * sk) for bi in range(nb)]
    for bi in range(nb):
        o_ref[bi] = cur[bi]


def _hconv_ctx(ph, hf, conv_w, conv_b, skip):
    b, nch3, l = ph.shape
    assert b == 2
    fwd, inv = (jnp.asarray(m) for m in _const_dense(l)[4:6])
    cw = jnp.pad(conv_w.T, ((0, 0), (0, 5)))
    return pl.pallas_call(
        _hconv_ctx_kernel,
        in_specs=[pl.BlockSpec(cw.shape, lambda: (0, 0)),
                  pl.BlockSpec((nch3, 1), lambda: (0, 0)),
                  pl.BlockSpec((HY_CH, 2), lambda: (0, 0)),
                  pl.BlockSpec(ph.shape, lambda: (0, 0, 0)),
                  pl.BlockSpec(hf.shape, lambda: (0, 0, 0)),
                  pl.BlockSpec(fwd.shape, lambda: (0, 0)),
                  pl.BlockSpec(inv.shape, lambda: (0, 0))],
        out_specs=pl.BlockSpec((b, HY_CH, l), lambda: (0, 0, 0)),
        out_shape=jax.ShapeDtypeStruct((b, HY_CH, l), F32),
        compiler_params=pltpu.CompilerParams(vmem_limit_bytes=VMEM_LIMIT_BYTES),
        name="hyena_conv_ctx",
    )(cw, conv_b.reshape(-1, 1), skip.T, ph, hf, fwd, inv)


def _mix_ffn_kernel(*refs, n_mix, final, ff_chunk):
    d = D_MODEL
    h_ref, y_refs = refs[0], refs[1:1 + n_mix]
    wo_ref, mod_ref, g2_ref, wg_ref, wu_ref, wd_ref = refs[1 + n_mix:7 + n_mix]
    rest = refs[7 + n_mix:]
    y, row = None, 0
    for y_ref in y_refs:
        rows = y_ref.shape[0]
        part = lax.dot_general(y_ref[...].astype(BF16), wo_ref[row:row + rows, :], _TN, preferred_element_type=F32)
        y = part if y is None else y + part
        row += rows
    if final:
        fg_ref, o_ref = rest
    else:
        (o_ref,) = rest
    h1 = h_ref[...] + mod_ref[:, 2 * d:3 * d] * y
    u = _norm_mod(h1, g2_ref[...], mod_ref[:, 3 * d:4 * d], mod_ref[:, 4 * d:5 * d]).astype(BF16)
    acc = jnp.zeros(h1.shape, F32)
    for j in range(D_FF // ff_chunk):
        cols = slice(j * ff_chunk, (j + 1) * ff_chunk)
        act = (_silu(_dot(u, wg_ref[:, cols])) * _dot(u, wu_ref[:, cols])).astype(BF16)
        acc = acc + _dot(act, wd_ref[cols, :])
    h2 = h1 + mod_ref[:, 5 * d:6 * d] * acc
    if final:
        h2 = _rms(h2, fg_ref[...])
    o_ref[...] = h2


def _layer_resident(stacked, layer):
    nd = stacked.ndim - 1
    return pl.BlockSpec((None,) + stacked.shape[1:], lambda *_: (layer,) + (0,) * nd, pipeline_mode=pl.Buffered(1))


def _mix_ffn(h, ys, wo, mod3, g2, wg, wu, wd, layer, final_g=None, tl=512, ff_chunk=256):
    b, l, d = h.shape
    tl = min(tl, l)
    tok = pl.BlockSpec((None, tl, d), lambda i, t: (i, t, 0))
    y_specs = [pl.BlockSpec((None, y.shape[1], tl), lambda i, t: (i, 0, t)) for y in ys]
    in_specs = ([tok] + y_specs + [_resident(wo.shape), pl.BlockSpec((None, 1, 6 * d), lambda i, t: (i, 0, 0)),
                                    pl.BlockSpec((1, d), lambda i, t: (0, 0)),
                                    _layer_resident(wg, layer), _layer_resident(wu, layer),
                                    _layer_resident(wd, layer)])
    args = [h, *ys, wo, mod3, g2, wg, wu, wd]
    if final_g is not None:
        in_specs.append(pl.BlockSpec((1, d), lambda i, t: (0, 0)))
        args.append(final_g)
    return pl.pallas_call(
        functools.partial(_mix_ffn_kernel, n_mix=len(ys), final=final_g is not None, ff_chunk=ff_chunk),
        grid=(b, l // tl),
        in_specs=in_specs,
        out_specs=tok,
        out_shape=jax.ShapeDtypeStruct((b, l, d), F32),
        compiler_params=_params("parallel", "parallel"),
        name="mix_ffn_final" if final_g is not None else "mix_ffn",
    )(*args)


def _qkv_kernel(x_ref, g_ref, mod_ref, w_ref, q_ref, k_ref, v_ref):
    d = D_MODEL
    u = _norm_mod(x_ref[...], g_ref[...], mod_ref[:, 0:d], mod_ref[:, d:2 * d]).astype(BF16)
    q_ref[...] = (_dot(u, w_ref[:, 0:d]) * (NA_HEAD_DIM ** -0.5)).astype(BF16)
    k_ref[...] = _dot(u, w_ref[:, d:2 * d]).astype(BF16)
    v_ref[...] = _dot(u, w_ref[:, 2 * d:]).astype(BF16)


def _qkv(h, g, mod3, w, tl=1024):
    b, l, d = h.shape
    tl = min(tl, l)
    tok = pl.BlockSpec((None, tl, d), lambda i, t: (i, t, 0))
    return pl.pallas_call(
        _qkv_kernel,
        grid=(b, l // tl),
        in_specs=[tok, pl.BlockSpec((1, d), lambda i, t: (0, 0)),
                  pl.BlockSpec((None, 1, 6 * d), lambda i, t: (i, 0, 0)), _resident(w.shape)],
        out_specs=[tok, tok, tok],
        out_shape=[jax.ShapeDtypeStruct((b, l, d), BF16)] * 3,
        compiler_params=_params("parallel", "parallel"),
        name="qkv_proj",
    )(h, g, mod3, w)


NA_QROWS = 4
NA_KROWS = 12
NA_PATTERNS = 3


def _na_key_start(r0, grid_rows):
    return jnp.minimum(jnp.clip(r0 - NA_KH // 2, 0, grid_rows - NA_KH), grid_rows - NA_KROWS)


def _na_pattern(pat, i, j):
    if pat == 0:
        return j - i + NA_KH - 1, j < NA_KH
    if pat == 1:
        return j - i + NA_KH // 2 - 1, i <= j < i + NA_KH
    return j - i - 1, j >= NA_KROWS - NA_KH


def _bias_blocks_kernel(rpb_ref, o_ref):
    pair = pl.program_id(0)
    ndr, ndc = 2 * NA_KH - 1, 2 * NA_KW - 1
    kcol = lax.broadcasted_iota(jnp.int32, (GRID_W, LANES), 0)
    lane = lax.broadcasted_iota(jnp.int32, (GRID_W, LANES), 1)
    q = lane % GRID_W
    upper = lane >= GRID_W
    wstart = jnp.clip(q - NA_KW // 2, 0, GRID_W - NA_KW)
    ok = (kcol >= wstart) & (kcol < wstart + NA_KW)
    dc = kcol - q + (NA_KW - 1)
    neg = jnp.full((GRID_W, LANES), NEG_BIG, F32)
    for e in range(2):
        base = (2 * pair + e) * ndr * ndc
        tiles = []
        for dr in range(ndr):
            acc = jnp.zeros((GRID_W, LANES), F32)
            for dd in range(ndc):
                acc = jnp.where(dc == dd, rpb_ref[base + dr * ndc + dd], acc)
            tiles.append(jnp.where(ok, acc, NEG_BIG))
        for pat in range(NA_PATTERNS):
            for j in range(NA_KROWS):
                for p in range(NA_QROWS // 2):
                    halves = []
                    for i in (2 * p, 2 * p + 1):
                        dr, visible = _na_pattern(pat, i, j)
                        halves.append(tiles[dr] if visible else neg)
                    o_ref[e, pat, j * GRID_W:(j + 1) * GRID_W, p * LANES:(p + 1) * LANES] = jnp.where(
                        upper, halves[1], halves[0])


def _bias_blocks(rpb):
    heads = rpb.shape[0]
    nq, nk = NA_QROWS * GRID_W, NA_KROWS * GRID_W
    return pl.pallas_call(
        _bias_blocks_kernel,
        grid=(heads // 2,),
        in_specs=[pl.BlockSpec(memory_space=pltpu.SMEM)],
        out_specs=pl.BlockSpec((None, 2, NA_PATTERNS, nk, nq), lambda p: (p, 0, 0, 0, 0)),
        out_shape=jax.ShapeDtypeStruct((heads // 2, 2, NA_PATTERNS, nk, nq), F32),
        compiler_params=_params("parallel"),
        name="na_bias_blocks",
    )(rpb.reshape(-1))


def _na_kernel(q_ref, k_ref, v_ref, kc_ref, vc_ref, bias_ref, o_ref, *, blocks_per_step, grid_rows):
    nq, nk = NA_QROWS * GRID_W, NA_KROWS * GRID_W
    lane = lax.broadcasted_iota(jnp.int32, (nq, LANES), 1)
    row = lax.broadcasted_iota(jnp.int32, (LANES, nq), 0)
    kc = kc_ref[...]
    vc = vc_ref[...]
    step = pl.program_id(2)
    for i in range(blocks_per_step):
        r0 = (step * blocks_per_step + i) * NA_QROWS
        pat = jnp.where(r0 == 0, 0, jnp.where(r0 == grid_rows - NA_QROWS, 2, 1))
        koff = pl.multiple_of(_na_key_start(r0, grid_rows) * GRID_W, GRID_W)
        q = q_ref[i * nq:(i + 1) * nq, :]
        ku = k_ref[pl.ds(koff, nk), :]
        vt = jnp.concatenate([v_ref[pl.ds(koff, nk), :], vc], 0).T
        vt = jnp.concatenate([vt, jnp.ones((16, vt.shape[1]), vt.dtype)], 0)
        outs = []
        for e in range(2):
            head = (lane >= NA_HEAD_DIM) if e else (lane < NA_HEAD_DIM)
            qh = jnp.where(head, q, jnp.zeros_like(q))
            sl = lax.dot_general(ku, qh, _NT, preferred_element_type=F32) + bias_ref[e, pat]
            sc = lax.dot_general(kc, qh, _NT, preferred_element_type=F32)
            s = jnp.concatenate([sl, sc], 0)
            p = jnp.exp((s - jnp.max(s, axis=0, keepdims=True)).astype(BF16))
            o = _dot(vt, p)
            outs.append(o[0:LANES] / o[LANES:LANES + 1])
        o_ref[:, i * nq:(i + 1) * nq] = jnp.where(row < NA_HEAD_DIM, outs[0], outs[1]).astype(o_ref.dtype)


def _na(q, k, v, kc, vc, bias_blocks, blocks_per_step=16):
    b, l, d = q.shape
    lc = kc.shape[1]
    grid_rows = l // GRID_W
    blocks_per_step = min(blocks_per_step, grid_rows // NA_QROWS)
    assert grid_rows % (NA_QROWS * blocks_per_step) == 0 and grid_rows >= NA_KROWS
    npairs = d // LANES
    tq = blocks_per_step * NA_QROWS * GRID_W
    return pl.pallas_call(
        functools.partial(_na_kernel, blocks_per_step=blocks_per_step, grid_rows=grid_rows),
        grid=(npairs, b, l // tq),
        in_specs=[pl.BlockSpec((None, tq, LANES), lambda p, i, t: (i, t, p)),
                  pl.BlockSpec((None, l, LANES), lambda p, i, t: (i, 0, p)),
                  pl.BlockSpec((None, l, LANES), lambda p, i, t: (i, 0, p)),
                  pl.BlockSpec((None, lc, LANES), lambda p, i, t: (i, 0, p)),
                  pl.BlockSpec((None, lc, LANES), lambda p, i, t: (i, 0, p)),
                  pl.BlockSpec((None,) + bias_blocks.shape[1:], lambda p, i, t: (p, 0, 0, 0, 0))],
        out_specs=pl.BlockSpec((None, LANES, tq), lambda p, i, t: (i, p, t)),
        out_shape=jax.ShapeDtypeStruct((b, d, l), BF16),
        compiler_params=_params("parallel", "parallel", "parallel"),
        name="neighbourhood_attention",
    )(q, k, v, kc, vc, bias_blocks)


def kernel(x, c, ctx, c_ctx, ada_w, ada_b, norm1_g, norm2_g, ffn_w_gate, ffn_w_up, ffn_w_down, ab_w_in, ab_w_out, hy_conv_w, hy_conv_b, hy_f_w1, hy_f_b1, hy_f_freq, hy_f_w2, hy_f_b2, hy_f_w3, hy_bias, na_w_qkv, na_w_out, na_rpb, final_g):
    b, l, d = x.shape
    lc = ctx.shape[1]
    assert b == 2 and d == D_MODEL and l % 1024 == 0 and ada_w.shape[0] == 2

    cond8 = jnp.zeros((8, d), F32).at[0:b].set(c).at[b].set(c_ctx)
    mod = _ada(cond8, ada_w, ada_b)
    mod_lat = [mod[i, 0:b].reshape(b, 1, 6 * d) for i in range(2)]
    mod_ctx = [jnp.broadcast_to(mod[i, b:b + 1], (b, 6 * d)).reshape(b, 1, 6 * d) for i in range(2)]
    n1 = [norm1_g[i].reshape(1, d) for i in range(2)]
    n2 = [norm2_g[i].reshape(1, d) for i in range(2)]
    ffn = (ffn_w_gate.astype(BF16), ffn_w_up.astype(BF16), ffn_w_down.astype(BF16))

    w_in_t = ab_w_in[0].T.astype(BF16)
    w_out = ab_w_out[0].astype(BF16)
    w1t = jnp.pad(hy_f_w1[0].T, ((0, 0), (0, LANES - HY_EMB)))
    col = lambda vec: vec.reshape(-1, 1)
    mlp = (w1t, col(hy_f_b1[0]), col(hy_f_freq[0]), hy_f_w2[0].T, col(hy_f_b2[0]),
           hy_f_w3[0].T.reshape(2, 2, HY_CH, HY_HID))

    zf, ph = _inproj(x, n1[0], mod_lat[0], w_in_t)
    y_f = _fourier_latent(zf)
    hf = _hfilter_latent(l, *mlp)
    y_h = _hconv_latent(ph, hf, hy_conv_w[0], hy_conv_b[0], hy_bias[0]).reshape(b, HY_CH, l)
    h_lat = _mix_ffn(x, (y_f, y_h), w_out, mod_lat[0], n2[0], *ffn, layer=0)

    zfc, phc = _inproj(ctx, n1[0], mod_ctx[0], w_in_t)
    y_fc = _fourier_ctx(zfc)
    hfc = _hfilter_ctx(lc, *mlp)
    y_hc = _hconv_ctx(phc, hfc, hy_conv_w[0], hy_conv_b[0], hy_bias[0])
    h_ctx = _mix_ffn(ctx, (y_fc, y_hc), w_out, mod_ctx[0], n2[0], *ffn, layer=0)

    w_qkv = na_w_qkv[0].astype(BF16)
    q, k, v = _qkv(h_lat, n1[1], mod_lat[1], w_qkv)
    _, kc, vc = _qkv(h_ctx, n1[1], mod_ctx[1], w_qkv)
    attn = _na(q, k, v, kc, vc, _bias_blocks(na_rpb[0]))
    return _mix_ffn(h_lat, (attn,), na_w_out[0].astype(BF16), mod_lat[1], n2[1], *ffn, layer=1,
                    final_g=final_g.reshape(1, d))
```

```python
import functools
import math

import numpy as np
import jax
import jax.numpy as jnp
from jax import lax
from jax.experimental import pallas as pl
from jax.experimental.pallas import tpu as pltpu

F32 = jnp.float32
BF16 = jnp.bfloat16

D_MODEL = 1024
FOURIER_CH = 512
FOURIER_GROUPS = 4
GROUP_DIM = 128
HY_CH = 512
HY_BANDS = 16
HY_EMB = 33
HY_HID = 64
HY_FAST_DECAY = 0.3
HY_SLOW_DECAY = 1.5
HY_DECAY_TARGET = 1e-2
D_FF = 2816
NA_HEADS = 16
NA_HEAD_DIM = 64
NA_KH = 8
NA_KW = 16
GRID_W = 64
RMS_EPS = 1e-6
NEG_BIG = -1e30

LANES = 128
VMEM_LIMIT_BYTES = 56 * 1024 * 1024

_NT = (((1,), (1,)), ((), ()))
_TN = (((0,), (0,)), ((), ()))


def _params(*sem):
    return pltpu.CompilerParams(dimension_semantics=sem, vmem_limit_bytes=VMEM_LIMIT_BYTES)


def _dot(a, b):
    return jnp.dot(a, b, preferred_element_type=F32)


def _dot_hi(a, b):
    return jnp.dot(a, b, preferred_element_type=F32, precision=lax.Precision.HIGHEST)


def _resident(shape):
    nd = len(shape)
    return pl.BlockSpec(shape, lambda *_: (0,) * nd, pipeline_mode=pl.Buffered(1))


def _rms(x, g):
    ms = jnp.mean(x * x, axis=-1, keepdims=True)
    return x * lax.rsqrt(ms + RMS_EPS) * g


def _norm_mod(x, g, shift, scale):
    return _rms(x, g) * (1.0 + scale) + shift


def _silu(x):
    return x * jax.nn.sigmoid(x)


def _cs(num, den):
    ang = 2.0 * np.pi * (np.asarray(num, np.float64) % den) / den
    return np.cos(ang), np.sin(ang)


@functools.lru_cache(maxsize=None)
def _const_channel_dft():
    c, s = _cs(np.outer(np.arange(GROUP_DIM), np.arange(GROUP_DIM)), GROUP_DIM)
    return (np.concatenate([c, -s], 0) / math.sqrt(GROUP_DIM)).astype(np.float32)


@functools.lru_cache(maxsize=None)
def _const_fourier(a_len):
    n = a_len * LANES
    c, s = _cs(np.outer(np.arange(a_len), np.arange(a_len)), a_len)
    w1 = np.block([[c, s], [-s, c]])
    tc, ts = _cs(np.outer(np.arange(a_len), np.arange(LANES)), n)
    tw = np.stack([tc, -ts])
    c2, s2 = _cs(np.outer(np.arange(LANES), np.arange(LANES)), LANES)
    w2 = np.concatenate([c2, s2], 0) / math.sqrt(n)
    r = LANES // a_len
    perm = np.array([r * m + h for h in range(r) for m in range(a_len)])
    w2 = w2[:, perm]
    return w1.astype(np.float32), tw.astype(np.float32), w2.astype(np.float32)


@functools.lru_cache(maxsize=None)
def _const_hyena(a_len):
    k1n = 2 * a_len
    n = k1n * LANES
    c, s = _cs(np.outer(np.arange(k1n), np.arange(a_len)), k1n)
    w1 = np.block([[c, s], [-s, c]])
    tc, ts = _cs(np.outer(np.arange(k1n), np.arange(LANES)), n)
    tw = np.stack([tc, -ts])
    c2, s2 = _cs(np.outer(np.arange(LANES), np.arange(LANES)), LANES)
    w2 = np.block([[c2, -s2], [s2, c2]])
    w2inv = np.block([[c2, s2], [-s2, c2]])
    v1 = np.concatenate([c.T, s.T], 0) / n
    f = lambda m: m.astype(np.float32)
    return f(w1), f(tw), f(w2), f(w2inv), f(v1)


@functools.lru_cache(maxsize=None)
def _const_dense(l_len):
    n = 2 * l_len
    cl, sl = _cs(np.outer(np.arange(l_len), np.arange(l_len)), l_len)
    four_c, four_s = cl / math.sqrt(l_len), sl / math.sqrt(l_len)
    cf, sf = _cs(np.outer(np.arange(l_len), np.arange(n)), n)
    fwd = np.block([[cf, -sf], [sf, cf]])
    inv = np.block([[cf.T, sf.T], [-sf.T, cf.T]]) / n
    f = lambda m: m.astype(np.float32)
    return f(four_c), f(four_s), f(cf), f(-sf), f(fwd), f(inv)


@functools.lru_cache(maxsize=None)
def _const_filter_features(l_len):
    t01 = np.linspace(0.0, 1.0, l_len)
    bands = np.linspace(1e-4, HY_BANDS - 1, HY_BANDS)
    ang = (2.0 * np.pi / l_len) * np.arange(l_len)[:, None] * bands[None, :]
    z = np.concatenate([t01[:, None], np.cos(ang), -np.sin(ang)], -1)
    zt = np.zeros((LANES, l_len), np.float64)
    zt[:HY_EMB] = z.T
    max_decay = math.log(HY_DECAY_TARGET) / HY_FAST_DECAY
    min_decay = math.log(HY_DECAY_TARGET) / HY_SLOW_DECAY
    deltas = np.abs(np.linspace(min_decay, max_decay, HY_CH))
    return zt.astype(np.float32), t01[None, :].astype(np.float32), deltas[:, None].astype(np.float32)


def _ada_kernel(c_ref, w_ref, b_ref, o_ref):
    s = _silu(c_ref[...]).astype(BF16)
    o_ref[...] = _dot(s, w_ref[...].astype(BF16)) + b_ref[...]


def _ada(cond8, ada_w, ada_b):
    depth, d, n6 = ada_w.shape
    tn = 1536
    return pl.pallas_call(
        _ada_kernel,
        grid=(depth, n6 // tn),
        in_specs=[pl.BlockSpec((8, d), lambda i, j: (0, 0)),
                  pl.BlockSpec((None, d, tn), lambda i, j: (i, 0, j)),
                  pl.BlockSpec((None, 1, tn), lambda i, j: (i, 0, j))],
        out_specs=pl.BlockSpec((None, 8, tn), lambda i, j: (i, 0, j)),
        out_shape=jax.ShapeDtypeStruct((depth, 8, n6), F32),
        compiler_params=_params("parallel", "parallel"),
        name="ada_mod",
    )(cond8, ada_w, ada_b.reshape(depth, 1, n6))


def _inproj_kernel(x_ref, g_ref, mod_ref, wt_ref, cdft_ref, zf_ref, ph_ref):
    d = D_MODEL
    u = _norm_mod(x_ref[...], g_ref[...], mod_ref[:, 0:d], mod_ref[:, d:2 * d]).astype(BF16)
    cdft = cdft_ref[...].astype(BF16)
    for g in range(FOURIER_GROUPS):
        rows = slice(g * GROUP_DIM, (g + 1) * GROUP_DIM)
        pt = lax.dot_general(wt_ref[rows, :], u, _NT, preferred_element_type=F32)
        z = _dot(cdft, pt.astype(BF16))
        zf_ref[0, rows, :] = z[0:GROUP_DIM]
        zf_ref[1, rows, :] = z[GROUP_DIM:]
    rb = 512
    for r0 in range(FOURIER_CH, wt_ref.shape[0], rb):
        ph_ref[r0 - FOURIER_CH:r0 - FOURIER_CH + rb, :] = lax.dot_general(
            wt_ref[r0:r0 + rb, :], u, _NT, preferred_element_type=F32)


def _inproj(x, g, mod3, w_in_t, tl=1024):
    b, l, d = x.shape
    tl = min(tl, l)
    nh = w_in_t.shape[0] - FOURIER_CH
    return pl.pallas_call(
        _inproj_kernel,
        grid=(b, l // tl),
        in_specs=[pl.BlockSpec((None, tl, d), lambda i, t: (i, t, 0)),
                  pl.BlockSpec((1, d), lambda i, t: (0, 0)),
                  pl.BlockSpec((None, 1, 6 * d), lambda i, t: (i, 0, 0)),
                  _resident(w_in_t.shape),
                  _resident((2 * GROUP_DIM, GROUP_DIM))],
        out_specs=[pl.BlockSpec((None, 2, FOURIER_CH, tl), lambda i, t: (i, 0, 0, t)),
                   pl.BlockSpec((None, nh, tl), lambda i, t: (i, 0, t))],
        out_shape=[jax.ShapeDtypeStruct((b, 2, FOURIER_CH, l), F32),
                   jax.ShapeDtypeStruct((b, nh, l), F32)],
        compiler_params=_params("parallel", "parallel"),
        name="inproj",
    )(x, g, mod3, w_in_t, jnp.asarray(_const_channel_dft()))


def _twiddle_store(ys_ref, c, yr, yi, tr, ti):
    ys_ref[c, :, 0:LANES] = (yr * tr - yi * ti).astype(ys_ref.dtype)
    ys_ref[c, :, LANES:] = (yr * ti + yi * tr).astype(ys_ref.dtype)


def _fourier_kernel(z_ref, w1_ref, tw_ref, w2_ref, o_ref, zs_ref, ys_ref):
    _, ct, a_len, _ = zs_ref.shape
    w1 = w1_ref[...].astype(BF16)
    tr, ti = tw_ref[0], tw_ref[1]
    for ri in range(2):
        zs_ref[ri] = z_ref[ri].reshape(ct, a_len, LANES)

    def stage1(p, carry):
        c = 2 * p
        x = jnp.concatenate([jnp.concatenate([zs_ref[0, c + h], zs_ref[1, c + h]], 0) for h in range(2)], 1)
        y = _dot(w1, x.astype(BF16))
        for h in range(2):
            yh = y[:, h * LANES:(h + 1) * LANES]
            _twiddle_store(ys_ref, c + h, yh[0:a_len], yh[a_len:], tr, ti)
        return carry

    lax.fori_loop(0, ct // 2, stage1, 0, unroll=8)
    y = ys_ref[...].reshape(ct * a_len, 2 * LANES)
    zr = _dot(y, w2_ref[...].astype(BF16))
    tiles = []
    for c in range(ct):
        t = zr[c * a_len:(c + 1) * a_len].T
        tiles.append(jnp.concatenate([t[h * a_len:(h + 1) * a_len] for h in range(LANES // a_len)], 1))
    o_ref[...] = jnp.stack(tiles, 0).reshape(ct, a_len * LANES)


def _fourier_latent(zf, ct=32):
    b, _, ch, l = zf.shape
    a_len = l // LANES
    w1, tw, w2 = (jnp.asarray(m) for m in _const_fourier(a_len))
    return pl.pallas_call(
        _fourier_kernel,
        grid=(b, ch // ct),
        in_specs=[pl.BlockSpec((None, 2, ct, l), lambda i, c: (i, 0, c, 0)),
                  _resident(w1.shape), _resident(tw.shape), _resident(w2.shape)],
        out_specs=pl.BlockSpec((None, ct, l), lambda i, c: (i, c, 0)),
        out_shape=jax.ShapeDtypeStruct((b, ch, l), F32),
        scratch_shapes=[pltpu.VMEM((2, ct, a_len, LANES), F32),
                        pltpu.VMEM((ct, a_len, 2 * LANES), BF16)],
        compiler_params=_params("parallel", "parallel"),
        name="fourier_latent",
    )(zf, w1, tw, w2)


def _fourier_ctx_kernel(z_ref, c_ref, s_ref, o_ref):
    zr = z_ref[0].astype(BF16)
    zi = z_ref[1].astype(BF16)
    o_ref[...] = _dot(zr, c_ref[...].astype(BF16)) + _dot(zi, s_ref[...].astype(BF16))


def _fourier_ctx(zf):
    b, _, ch, l = zf.shape
    four_c, four_s = (jnp.asarray(m) for m in _const_dense(l)[:2])
    return pl.pallas_call(
        _fourier_ctx_kernel,
        grid=(b,),
        in_specs=[pl.BlockSpec((None, 2, ch, l), lambda i: (i, 0, 0, 0)),
                  _resident((l, l)), _resident((l, l))],
        out_specs=pl.BlockSpec((None, ch, l), lambda i: (i, 0, 0)),
        out_shape=jax.ShapeDtypeStruct((b, ch, l), F32),
        compiler_params=_params("parallel"),
        name="fourier_ctx",
    )(zf, four_c, four_s)


def _filter_mlp(zt, w1t, b1, freq, w2t, b2):
    h = jnp.sin(freq * (_dot_hi(w1t, zt) + b1))
    return jnp.sin(freq * (_dot_hi(w2t, h) + b2))


@functools.lru_cache(maxsize=None)
def _const_filter_fft(a_len):
    k1n = 2 * a_len
    c, s = _cs(np.outer(np.arange(k1n), np.arange(k1n)), k1n)
    w1r = np.concatenate([c, -s], 0)
    perm = np.zeros((2, LANES, LANES))
    for b in range(1, LANES):
        perm[0, LANES - b, b] = 1.0
    perm[1, 0, 0] = 1.0
    return w1r.astype(np.float32), perm.astype(np.float32)


def _hfilter_kernel(zt_ref, t_ref, trev_ref, dl_ref, w1t_ref, b1_ref, fr_ref, w2t_ref, b2_ref, w3t_ref,
                    perm_ref, w1r_ref, tw_ref, w2_ref, o_ref, h2_ref, h2r_ref, xs_ref, ys_ref):
    cf = o_ref.shape[0]
    k1n = xs_ref.shape[1]
    a_len = k1n // 2
    l = a_len * LANES

    @pl.when((pl.program_id(0) == 0) & (pl.program_id(1) == 0))
    def _():
        step = min(l, 1024)
        for s in range(0, l, step):
            h2_ref[:, s:s + step] = _filter_mlp(zt_ref[:, s:s + step], w1t_ref[...], b1_ref[...], fr_ref[...],
                                                w2t_ref[...], b2_ref[...]).astype(h2_ref.dtype)
        flip, first = perm_ref[0].astype(BF16), perm_ref[1].astype(BF16)
        for a in range(a_len):
            src = a_len - 1 - a
            r = _dot(h2_ref[:, src * LANES:(src + 1) * LANES], flip)
            if a >= 1:
                r = r + _dot(h2_ref[:, (src + 1) * LANES:(src + 2) * LANES], first)
            h2r_ref[:, a * LANES:(a + 1) * LANES] = r.astype(h2r_ref.dtype)

    w3f = w3t_ref[0].astype(BF16)
    w3b = w3t_ref[1].astype(BF16)
    dl = dl_ref[...]
    f = _dot(w3f, h2_ref[...]) * jnp.exp(-(dl * t_ref[...]))
    lane = lax.broadcasted_iota(jnp.int32, (cf, l), 1)
    g = jnp.where(lane == 0, 0.0, _dot(w3b, h2r_ref[...]) * jnp.exp(-(dl * trev_ref[...])))
    b0 = _dot(w3b, h2_ref[:, 0:LANES])[:, 0:1]
    tot = (jnp.sum(jnp.abs(f), axis=1, keepdims=True) + jnp.sum(jnp.abs(g), axis=1, keepdims=True) + jnp.abs(b0))
    inv = 1.0 / tot
    xs_ref[...] = (jnp.concatenate([f, g], 1) * inv).reshape(cf, k1n, LANES).astype(xs_ref.dtype)

    w1r = w1r_ref[...].astype(BF16)
    tr, ti = tw_ref[0], tw_ref[1]

    def stage1(p, carry):
        c = 2 * p
        y = _dot(w1r, jnp.concatenate([xs_ref[c], xs_ref[c + 1]], 1))
        for h in range(2):
            yh = y[:, h * LANES:(h + 1) * LANES]
            _twiddle_store(ys_ref, c + h, yh[0:k1n], yh[k1n:], tr, ti)
        return carry

    lax.fori_loop(0, cf // 2, stage1, 0, unroll=8)
    o_ref[...] = _dot(ys_ref[...].reshape(cf * k1n, 2 * LANES), w2_ref[...].astype(BF16)).reshape(cf, k1n, 2 * LANES)


def _hfilter_latent(l, w1t, b1, freq, w2t, b2, w3t, cf=64):
    a_len = l // LANES
    k1n = 2 * a_len
    zt, t01, deltas = _const_filter_features(l)
    trev = np.concatenate([np.zeros((1, 1), np.float32), t01[:, :0:-1]], 1)
    zt, t01, trev, deltas = (jnp.asarray(m) for m in (zt, t01, trev, deltas))
    _, tw, w2, _, _ = (jnp.asarray(m) for m in _const_hyena(a_len))
    w1r, perm = (jnp.asarray(m) for m in _const_filter_fft(a_len))
    orders = w3t.shape[1]
    return pl.pallas_call(
        _hfilter_kernel,
        grid=(orders, HY_CH // cf),
        in_specs=[_resident(zt.shape), _resident(t01.shape), _resident(trev.shape),
                  pl.BlockSpec((cf, 1), lambda o, c: (c, 0)),
                  _resident(w1t.shape), _resident(b1.shape), _resident(freq.shape),
                  _resident(w2t.shape), _resident(b2.shape),
                  pl.BlockSpec((2, None, cf, HY_HID), lambda o, c: (0, o, c, 0)),
                  _resident(perm.shape), _resident(w1r.shape), _resident(tw.shape), _resident(w2.shape)],
        out_specs=pl.BlockSpec((None, cf, k1n, 2 * LANES), lambda o, c: (o, c, 0, 0)),
        out_shape=jax.ShapeDtypeStruct((orders, HY_CH, k1n, 2 * LANES), F32),
        scratch_shapes=[pltpu.VMEM((HY_HID, l), BF16),
                        pltpu.VMEM((HY_HID, l), BF16),
                        pltpu.VMEM((cf, k1n, LANES), BF16),
                        pltpu.VMEM((cf, k1n, 2 * LANES), BF16)],
        compiler_params=_params("arbitrary", "arbitrary"),
        name="hyena_filter_latent",
    )(zt, t01, trev, deltas, w1t, b1, freq, w2t, b2, w3t, perm, w1r, tw, w2)


def _hfilter_ctx_kernel(zt_ref, t_ref, dl_ref, w1t_ref, b1_ref, fr_ref, w2t_ref, b2_ref, w3t_ref,
                        cf_ref, sf_ref, o_ref):
    l = zt_ref.shape[1]
    h2 = _filter_mlp(zt_ref[...], w1t_ref[...], b1_ref[...], fr_ref[...], w2t_ref[...], b2_ref[...])
    ch = w3t_ref.shape[1]
    w3 = w3t_ref[...].reshape(2 * ch, HY_HID)
    dl = dl_ref[...]
    hv = _dot_hi(w3, h2) * jnp.exp(-(jnp.concatenate([dl, dl], 0) * t_ref[...]))
    tot = jnp.sum(jnp.abs(hv), axis=1, keepdims=True)
    inv = 1.0 / (tot[0:ch] + tot[ch:])
    f = hv[0:ch] * inv
    lane = lax.broadcasted_iota(jnp.int32, (ch, l), 1)
    bk = jnp.where(lane == 0, 0.0, hv[ch:] * inv)
    o_ref[:, 0:2 * l] = _dot((f + bk).astype(BF16), cf_ref[...].astype(BF16))
    o_ref[:, 2 * l:] = _dot((f - bk).astype(BF16), sf_ref[...].astype(BF16))


def _hfilter_ctx(l, w1t, b1, freq, w2t, b2, w3t):
    zt, t01, deltas = (jnp.asarray(m) for m in _const_filter_features(l))
    cfm, nsf = (jnp.asarray(m) for m in _const_dense(l)[2:4])
    orders = w3t.shape[1]
    return pl.pallas_call(
        _hfilter_ctx_kernel,
        grid=(orders,),
        in_specs=[_resident(zt.shape), _resident(t01.shape), _resident(deltas.shape),
                  _resident(w1t.shape), _resident(b1.shape), _resident(freq.shape),
                  _resident(w2t.shape), _resident(b2.shape),
                  pl.BlockSpec((2, None, HY_CH, HY_HID), lambda o: (0, o, 0, 0)),
                  _resident(cfm.shape), _resident(nsf.shape)],
        out_specs=pl.BlockSpec((None, HY_CH, 4 * l), lambda o: (o, 0, 0)),
        out_shape=jax.ShapeDtypeStruct((orders, HY_CH, 4 * l), F32),
        compiler_params=_params("parallel"),
        name="hyena_filter_ctx",
    )(zt, t01, deltas, w1t, b1, freq, w2t, b2, w3t, cfm, nsf)


def _short_conv(x, cw, cb):
    l = x.shape[1]
    lane = lax.broadcasted_iota(jnp.int32, x.shape, 1)
    prev = jnp.where(lane == 0, 0.0, pltpu.roll(x, 1, 1))
    nxt = jnp.where(lane == l - 1, 0.0, pltpu.roll(x, l - 1, 1))
    return cw[:, 0:1] * prev + cw[:, 1:2] * x + cw[:, 2:3] * nxt + cb


def _short_conv_tile(x, w0, w1, w2, bias):
    a_len = x.shape[0]
    lane = lax.broadcasted_iota(jnp.int32, x.shape, 1)
    row = lax.broadcasted_iota(jnp.int32, x.shape, 0)
    xl = pltpu.roll(x, 1, 1)
    prev = jnp.where(lane == 0, jnp.where(row == 0, 0.0, pltpu.roll(xl, 1, 0)), xl)
    xr = pltpu.roll(x, LANES - 1, 1)
    nxt = jnp.where(lane == LANES - 1, jnp.where(row == a_len - 1, 0.0, pltpu.roll(xr, a_len - 1, 0)), xr)
    return w0 * prev + w1 * x + w2 * nxt + bias


def _hconv_kernel(cw_ref, cb_ref, sk_ref, x1_ref, x2_ref, v_ref, hf_ref,
                  w1_ref, tw_ref, w2_ref, w2i_ref, v1_ref, o_ref, g1_ref, g2_ref, vb_ref, ys_ref, us_ref, *, group):
    nb, ct, a_len, _ = vb_ref.shape
    k1n = 2 * a_len
    nch3 = cb_ref.shape[0]
    c0 = pl.program_id(0) * ct
    w1 = w1_ref[...].astype(BF16)
    w2 = w2_ref[...].astype(BF16)
    w2i = w2i_ref[...].astype(BF16)
    v1 = v1_ref[...].astype(BF16)
    tr, ti = tw_ref[0], tw_ref[1]

    for src, dst in ((x1_ref, g1_ref), (x2_ref, g2_ref), (v_ref, vb_ref)):
        for bi in range(nb):
            dst[bi] = src[bi].reshape(ct, a_len, LANES)

    def sconv(x, ch):
        return _short_conv_tile(x, cw_ref[ch], cw_ref[nch3 + ch], cw_ref[2 * nch3 + ch], cb_ref[ch])

    for order, gate_ref in enumerate((g1_ref, g2_ref)):
        def stage1(p, carry):
            c = 2 * p
            if order == 0:
                for h in range(2):
                    for bi in range(nb):
                        vb_ref[bi, c + h] = sconv(vb_ref[bi, c + h], 2 * HY_CH + c0 + c + h)
            x = jnp.concatenate([jnp.concatenate([vb_ref[0, c + h], vb_ref[1, c + h]], 0) for h in range(2)], 1)
            y = _dot(w1, x.astype(BF16))
            for h in range(2):
                yh = y[:, h * LANES:(h + 1) * LANES]
                _twiddle_store(ys_ref, c + h, yh[0:k1n], yh[k1n:], tr, ti)
            return carry

        lax.fori_loop(0, ct // 2, stage1, 0, unroll=8)

        def middle(g, carry):
            cs = pl.ds(g * group, group)
            z = _dot(ys_ref[cs].reshape(group * k1n, 2 * LANES), w2)
            h = hf_ref[order, cs].reshape(group * k1n, 2 * LANES)
            zr, zi = z[:, 0:LANES], z[:, LANES:]
            hr, hi = h[:, 0:LANES], h[:, LANES:]
            p = jnp.concatenate([zr * hr - zi * hi, zr * hi + zi * hr], 1).astype(BF16)
            u = _dot(p, w2i).reshape(group, k1n, 2 * LANES)
            ur, ui = u[:, :, 0:LANES], u[:, :, LANES:]
            us_ref[cs, :, 0:LANES] = (ur * tr + ui * ti).astype(BF16)
            us_ref[cs, :, LANES:] = (ui * tr - ur * ti).astype(BF16)
            return carry

        lax.fori_loop(0, ct // group, middle, 0, unroll=2)

        def stage1_inv(c, carry):
            q = _dot(v1, us_ref[c])
            conv = (q[0:a_len, 0:LANES] - q[a_len:, LANES:], q[a_len:, 0:LANES] + q[0:a_len, LANES:])
            skip = sk_ref[order * HY_CH + c0 + c]
            for bi in range(nb):
                gate = sconv(gate_ref[bi, c], order * HY_CH + c0 + c)
                vb_ref[bi, c] = gate * (conv[bi] + vb_ref[bi, c] * skip)
            return carry

        lax.fori_loop(0, ct, stage1_inv, 0, unroll=8)

    for bi in range(nb):
        o_ref[bi] = vb_ref[bi].reshape(ct, a_len * LANES)


def _hconv_latent(ph, hf, conv_w, conv_b, skip, ct=32, group=16):
    b, nch3, l = ph.shape
    assert b == 2, "the two batch entries ride as real/imaginary parts"
    a_len = l // LANES
    k1n = 2 * a_len
    w1, tw, w2, w2i, v1 = (jnp.asarray(m) for m in _const_hyena(a_len))
    nblk = HY_CH // ct
    smem = pl.BlockSpec(memory_space=pltpu.SMEM)
    part = lambda off: pl.BlockSpec((b, ct, l), lambda c: (0, c + off * nblk, 0))
    scratch = pltpu.VMEM((b, ct, a_len, LANES), F32)
    return pl.pallas_call(
        functools.partial(_hconv_kernel, group=group),
        grid=(nblk,),
        in_specs=[smem, smem, smem, part(0), part(1), part(2),
                  pl.BlockSpec((2, ct, k1n, 2 * LANES), lambda c: (0, c, 0, 0)),
                  _resident(w1.shape), _resident(tw.shape), _resident(w2.shape),
                  _resident(w2i.shape), _resident(v1.shape)],
        out_specs=pl.BlockSpec((b, ct, l), lambda c: (0, c, 0)),
        out_shape=jax.ShapeDtypeStruct((b, HY_CH, l), F32),
        scratch_shapes=[scratch, scratch, scratch,
                        pltpu.VMEM((ct, k1n, 2 * LANES), BF16),
                        pltpu.VMEM((ct, k1n, 2 * LANES), BF16)],
        compiler_params=_params("parallel"),
        name="hyena_conv_latent",
    )(conv_w.reshape(-1), conv_b, skip.reshape(-1), ph, ph, ph, hf, w1, tw, w2, w2i, v1)


def _hconv_ctx_kernel(cw_ref, cb_ref, sk_ref, p_ref, hf_ref, fwd_ref, inv_ref, o_ref):
    nb, _, l = p_ref.shape
    ch = HY_CH
    z = [_short_conv(p_ref[bi], cw_ref[...], cb_ref[...]) for bi in range(nb)]
    fwd = fwd_ref[...].astype(BF16)
    inv = inv_ref[...].astype(BF16)
    cur = [zz[2 * ch:] for zz in z]
    for order in range(2):
        spec = _dot(jnp.concatenate(cur, 1).astype(BF16), fwd)
        h = hf_ref[order]
        zr, zi, hr, hi = spec[:, 0:2 * l], spec[:, 2 * l:], h[:, 0:2 * l], h[:, 2 * l:]
        prod = jnp.concatenate([zr * hr - zi * hi, zr * hi + zi * hr], 1).astype(BF16)
        y = _dot(prod, inv)
        sk = sk_ref[:, order:order + 1]
        cur = [z[bi][order * ch:(order + 1) * ch] * (y[:, bi * l:(bi + 1) * l] + cur[bi] * sk) for bi in range(nb)]
    for bi in range(nb):
        o_ref[bi] = cur[bi]


def _hconv_ctx(ph, hf, conv_w, conv_b, skip):
    b, nch3, l = ph.shape
    assert b == 2
    fwd, inv = (jnp.asarray(m) for m in _const_dense(l)[4:6])
    cw = jnp.pad(conv_w.T, ((0, 0), (0, 5)))
    return pl.pallas_call(
        _hconv_ctx_kernel,
        in_specs=[pl.BlockSpec(cw.shape, lambda: (0, 0)),
                  pl.BlockSpec((nch3, 1), lambda: (0, 0)),
                  pl.BlockSpec((HY_CH, 2), lambda: (0, 0)),
                  pl.BlockSpec(ph.shape, lambda: (0, 0, 0)),
                  pl.BlockSpec(hf.shape, lambda: (0, 0, 0)),
                  pl.BlockSpec(fwd.shape, lambda: (0, 0)),
                  pl.BlockSpec(inv.shape, lambda: (0, 0))],
        out_specs=pl.BlockSpec((b, HY_CH, l), lambda: (0, 0, 0)),
        out_shape=jax.ShapeDtypeStruct((b, HY_CH, l), F32),
        compiler_params=pltpu.CompilerParams(vmem_limit_bytes=VMEM_LIMIT_BYTES),
        name="hyena_conv_ctx",
    )(cw, conv_b.reshape(-1, 1), skip.T, ph, hf, fwd, inv)


def _mix_ffn_kernel(*refs, n_mix, final, ff_chunk):
    d = D_MODEL
    h_ref, y_refs = refs[0], refs[1:1 + n_mix]
    wo_ref, mod_ref, g2_ref, wg_ref, wu_ref, wd_ref = refs[1 + n_mix:7 + n_mix]
    rest = refs[7 + n_mix:]
    y, row = None, 0
    for y_ref in y_refs:
        rows = y_ref.shape[0]
        part = lax.dot_general(y_ref[...].astype(BF16), wo_ref[row:row + rows, :], _TN, preferred_element_type=F32)
        y = part if y is None else y + part
        row += rows
    if final:
        fg_ref, o_ref = rest
    else:
        (o_ref,) = rest
    h1 = h_ref[...] + mod_ref[:, 2 * d:3 * d] * y
    u = _norm_mod(h1, g2_ref[...], mod_ref[:, 3 * d:4 * d], mod_ref[:, 4 * d:5 * d]).astype(BF16)
    acc = jnp.zeros(h1.shape, F32)
    for j in range(D_FF // ff_chunk):
        cols = slice(j * ff_chunk, (j + 1) * ff_chunk)
        act = (_silu(_dot(u, wg_ref[:, cols])) * _dot(u, wu_ref[:, cols])).astype(BF16)
        acc = acc + _dot(act, wd_ref[cols, :])
    h2 = h1 + mod_ref[:, 5 * d:6 * d] * acc
    if final:
        h2 = _rms(h2, fg_ref[...])
    o_ref[...] = h2


def _layer_resident(stacked, layer):
    nd = stacked.ndim - 1
    return pl.BlockSpec((None,) + stacked.shape[1:], lambda *_: (layer,) + (0,) * nd, pipeline_mode=pl.Buffered(1))


def _mix_ffn(h, ys, wo, mod3, g2, wg, wu, wd, layer, final_g=None, tl=512, ff_chunk=256):
    b, l, d = h.shape
    tl = min(tl, l)
    tok = pl.BlockSpec((None, tl, d), lambda i, t: (i, t, 0))
    y_specs = [pl.BlockSpec((None, y.shape[1], tl), lambda i, t: (i, 0, t)) for y in ys]
    in_specs = ([tok] + y_specs + [_resident(wo.shape), pl.BlockSpec((None, 1, 6 * d), lambda i, t: (i, 0, 0)),
                                    pl.BlockSpec((1, d), lambda i, t: (0, 0)),
                                    _layer_resident(wg, layer), _layer_resident(wu, layer),
                                    _layer_resident(wd, layer)])
    args = [h, *ys, wo, mod3, g2, wg, wu, wd]
    if final_g is not None:
        in_specs.append(pl.BlockSpec((1, d), lambda i, t: (0, 0)))
        args.append(final_g)
    return pl.pallas_call(
        functools.partial(_mix_ffn_kernel, n_mix=len(ys), final=final_g is not None, ff_chunk=ff_chunk),
        grid=(b, l // tl),
        in_specs=in_specs,
        out_specs=tok,
        out_shape=jax.ShapeDtypeStruct((b, l, d), F32),
        compiler_params=_params("parallel", "parallel"),
        name="mix_ffn_final" if final_g is not None else "mix_ffn",
    )(*args)


def _qkv_kernel(x_ref, g_ref, mod_ref, w_ref, q_ref, k_ref, v_ref):
    d = D_MODEL
    u = _norm_mod(x_ref[...], g_ref[...], mod_ref[:, 0:d], mod_ref[:, d:2 * d]).astype(BF16)
    q_ref[...] = (_dot(u, w_ref[:, 0:d]) * (NA_HEAD_DIM ** -0.5)).astype(BF16)
    k_ref[...] = _dot(u, w_ref[:, d:2 * d]).astype(BF16)
    v_ref[...] = _dot(u, w_ref[:, 2 * d:]).astype(BF16)


def _qkv(h, g, mod3, w, tl=1024):
    b, l, d = h.shape
    tl = min(tl, l)
    tok = pl.BlockSpec((None, tl, d), lambda i, t: (i, t, 0))
    return pl.pallas_call(
        _qkv_kernel,
        grid=(b, l // tl),
        in_specs=[tok, pl.BlockSpec((1, d), lambda i, t: (0, 0)),
                  pl.BlockSpec((None, 1, 6 * d), lambda i, t: (i, 0, 0)), _resident(w.shape)],
        out_specs=[tok, tok, tok],
        out_shape=[jax.ShapeDtypeStruct((b, l, d), BF16)] * 3,
        compiler_params=_params("parallel", "parallel"),
        name="qkv_proj",
    )(h, g, mod3, w)


NA_QROWS = 4
NA_KROWS = 12
NA_PATTERNS = 3


def _na_key_start(r0, grid_rows):
    return jnp.minimum(jnp.clip(r0 - NA_KH // 2, 0, grid_rows - NA_KH), grid_rows - NA_KROWS)


def _na_pattern(pat, i, j):
    if pat == 0:
        return j - i + NA_KH - 1, j < NA_KH
    if pat == 1:
        return j - i + NA_KH // 2 - 1, i <= j < i + NA_KH
    return j - i - 1, j >= NA_KROWS - NA_KH


def _bias_blocks_kernel(rpb_ref, o_ref):
    pair = pl.program_id(0)
    ndr, ndc = 2 * NA_KH - 1, 2 * NA_KW - 1
    kcol = lax.broadcasted_iota(jnp.int32, (GRID_W, LANES), 0)
    lane = lax.broadcasted_iota(jnp.int32, (GRID_W, LANES), 1)
    q = lane % GRID_W
    upper = lane >= GRID_W
    wstart = jnp.clip(q - NA_KW // 2, 0, GRID_W - NA_KW)
    ok = (kcol >= wstart) & (kcol < wstart + NA_KW)
    dc = kcol - q + (NA_KW - 1)
    neg = jnp.full((GRID_W, LANES), NEG_BIG, F32)
    for e in range(2):
        base = (2 * pair + e) * ndr * ndc
        tiles = []
        for dr in range(ndr):
            acc = jnp.zeros((GRID_W, LANES), F32)
            for dd in range(ndc):
                acc = jnp.where(dc == dd, rpb_ref[base + dr * ndc + dd], acc)
            tiles.append(jnp.where(ok, acc, NEG_BIG))
        for pat in range(NA_PATTERNS):
            for j in range(NA_KROWS):
                for p in range(NA_QROWS // 2):
                    halves = []
                    for i in (2 * p, 2 * p + 1):
                        dr, visible = _na_pattern(pat, i, j)
                        halves.append(tiles[dr] if visible else neg)
                    o_ref[e, pat, j * GRID_W:(j + 1) * GRID_W, p * LANES:(p + 1) * LANES] = jnp.where(
                        upper, halves[1], halves[0])


def _bias_blocks(rpb):
    heads = rpb.shape[0]
    nq, nk = NA_QROWS * GRID_W, NA_KROWS * GRID_W
    return pl.pallas_call(
        _bias_blocks_kernel,
        grid=(heads // 2,),
        in_specs=[pl.BlockSpec(memory_space=pltpu.SMEM)],
        out_specs=pl.BlockSpec((None, 2, NA_PATTERNS, nk, nq), lambda p: (p, 0, 0, 0, 0)),
        out_shape=jax.ShapeDtypeStruct((heads // 2, 2, NA_PATTERNS, nk, nq), F32),
        compiler_params=_params("parallel"),
        name="na_bias_blocks",
    )(rpb.reshape(-1))


def _na_kernel(q_ref, k_ref, v_ref, kc_ref, vc_ref, bias_ref, o_ref, *, blocks_per_step, grid_rows):
    nq, nk = NA_QROWS * GRID_W, NA_KROWS * GRID_W
    lane = lax.broadcasted_iota(jnp.int32, (nq, LANES), 1)
    row = lax.broadcasted_iota(jnp.int32, (LANES, nq), 0)
    kc = kc_ref[...]
    vc = vc_ref[...]
    step = pl.program_id(2)
    for i in range(blocks_per_step):
        r0 = (step * blocks_per_step + i) * NA_QROWS
        pat = jnp.where(r0 == 0, 0, jnp.where(r0 == grid_rows - NA_QROWS, 2, 1))
        koff = pl.multiple_of(_na_key_start(r0, grid_rows) * GRID_W, GRID_W)
        q = q_ref[i * nq:(i + 1) * nq, :]
        ku = k_ref[pl.ds(koff, nk), :]
        vt = jnp.concatenate([v_ref[pl.ds(koff, nk), :], vc], 0).T
        vt = jnp.concatenate([vt, jnp.ones((16, vt.shape[1]), vt.dtype)], 0)
        outs = []
        for e in range(2):
            head = (lane >= NA_HEAD_DIM) if e else (lane < NA_HEAD_DIM)
            qh = jnp.where(head, q, jnp.zeros_like(q))
            sl = lax.dot_general(ku, qh, _NT, preferred_element_type=F32) + bias_ref[e, pat]
            sc = lax.dot_general(kc, qh, _NT, preferred_element_type=F32)
            s = jnp.concatenate([sl, sc], 0)
            p = jnp.exp((s - jnp.max(s, axis=0, keepdims=True)).astype(BF16))
            o = _dot(vt, p)
            outs.append(o[0:LANES] / o[LANES:LANES + 1])
        o_ref[:, i * nq:(i + 1) * nq] = jnp.where(row < NA_HEAD_DIM, outs[0], outs[1]).astype(o_ref.dtype)


def _na(q, k, v, kc, vc, bias_blocks, blocks_per_step=32):
    b, l, d = q.shape
    lc = kc.shape[1]
    grid_rows = l // GRID_W
    blocks_per_step = min(blocks_per_step, grid_rows // NA_QROWS)
    assert grid_rows % (NA_QROWS * blocks_per_step) == 0 and grid_rows >= NA_KROWS
    npairs = d // LANES
    tq = blocks_per_step * NA_QROWS * GRID_W
    return pl.pallas_call(
        functools.partial(_na_kernel, blocks_per_step=blocks_per_step, grid_rows=grid_rows),
        grid=(npairs, b, l // tq),
        in_specs=[pl.BlockSpec((None, tq, LANES), lambda p, i, t: (i, t, p)),
                  pl.BlockSpec((None, l, LANES), lambda p, i, t: (i, 0, p)),
                  pl.BlockSpec((None, l, LANES), lambda p, i, t: (i, 0, p)),
                  pl.BlockSpec((None, lc, LANES), lambda p, i, t: (i, 0, p)),
                  pl.BlockSpec((None, lc, LANES), lambda p, i, t: (i, 0, p)),
                  pl.BlockSpec((None,) + bias_blocks.shape[1:], lambda p, i, t: (p, 0, 0, 0, 0))],
        out_specs=pl.BlockSpec((None, LANES, tq), lambda p, i, t: (i, p, t)),
        out_shape=jax.ShapeDtypeStruct((b, d, l), BF16),
        compiler_params=_params("parallel", "parallel", "parallel"),
        name="neighbourhood_attention",
    )(q, k, v, kc, vc, bias_blocks)


def kernel(x, c, ctx, c_ctx, ada_w, ada_b, norm1_g, norm2_g, ffn_w_gate, ffn_w_up, ffn_w_down, ab_w_in, ab_w_out, hy_conv_w, hy_conv_b, hy_f_w1, hy_f_b1, hy_f_freq, hy_f_w2, hy_f_b2, hy_f_w3, hy_bias, na_w_qkv, na_w_out, na_rpb, final_g):
    b, l, d = x.shape
    lc = ctx.shape[1]
    assert b == 2 and d == D_MODEL and l % 1024 == 0 and ada_w.shape[0] == 2

    cond8 = jnp.zeros((8, d), F32).at[0:b].set(c).at[b].set(c_ctx)
    mod = _ada(cond8, ada_w, ada_b)
    mod_lat = [mod[i, 0:b].reshape(b, 1, 6 * d) for i in range(2)]
    mod_ctx = [jnp.broadcast_to(mod[i, b:b + 1], (b, 6 * d)).reshape(b, 1, 6 * d) for i in range(2)]
    n1 = [norm1_g[i].reshape(1, d) for i in range(2)]
    n2 = [norm2_g[i].reshape(1, d) for i in range(2)]
    ffn = (ffn_w_gate.astype(BF16), ffn_w_up.astype(BF16), ffn_w_down.astype(BF16))

    w_in_t = ab_w_in[0].T.astype(BF16)
    w_out = ab_w_out[0].astype(BF16)
    w1t = jnp.pad(hy_f_w1[0].T, ((0, 0), (0, LANES - HY_EMB)))
    col = lambda vec: vec.reshape(-1, 1)
    mlp = (w1t, col(hy_f_b1[0]), col(hy_f_freq[0]), hy_f_w2[0].T, col(hy_f_b2[0]),
           hy_f_w3[0].T.reshape(2, 2, HY_CH, HY_HID))

    zf, ph = _inproj(x, n1[0], mod_lat[0], w_in_t)
    y_f = _fourier_latent(zf)
    hf = _hfilter_latent(l, *mlp)
    y_h = _hconv_latent(ph, hf, hy_conv_w[0], hy_conv_b[0], hy_bias[0])
    h_lat = _mix_ffn(x, (y_f, y_h), w_out, mod_lat[0], n2[0], *ffn, layer=0)

    zfc, phc = _inproj(ctx, n1[0], mod_ctx[0], w_in_t)
    y_fc = _fourier_ctx(zfc)
    hfc = _hfilter_ctx(lc, *mlp)
    y_hc = _hconv_ctx(phc, hfc, hy_conv_w[0], hy_conv_b[0], hy_bias[0])
    h_ctx = _mix_ffn(ctx, (y_fc, y_hc), w_out, mod_ctx[0], n2[0], *ffn, layer=0)

    w_qkv = na_w_qkv[0].astype(BF16)
    q, k, v = _qkv(h_lat, n1[1], mod_lat[1], w_qkv)
    _, kc, vc = _qkv(h_ctx, n1[1], mod_ctx[1], w_qkv)
    attn = _na(q, k, v, kc, vc, _bias_blocks(na_rpb[0]))
    return _mix_ffn(h_lat, (attn,), na_w_out[0].astype(BF16), mod_lat[1], n2[1], *ffn, layer=1,
                    final_g=final_g.reshape(1, d))
```

```python
import functools
import math

import numpy as np
import jax
import jax.numpy as jnp
from jax import lax
from jax.experimental import pallas as pl
from jax.experimental.pallas import tpu as pltpu

F32 = jnp.float32
BF16 = jnp.bfloat16

D_MODEL = 1024
FOURIER_CH = 512
FOURIER_GROUPS = 4
GROUP_DIM = 128
HY_CH = 512
HY_BANDS = 16
HY_EMB = 33
HY_HID = 64
HY_FAST_DECAY = 0.3
HY_SLOW_DECAY = 1.5
HY_DECAY_TARGET = 1e-2
D_FF = 2816
NA_HEADS = 16
NA_HEAD_DIM = 64
NA_KH = 8
NA_KW = 16
GRID_W = 64
RMS_EPS = 1e-6
NEG_BIG = -1e30

LANES = 128
VMEM_LIMIT_BYTES = 56 * 1024 * 1024

_NT = (((1,), (1,)), ((), ()))
_TN = (((0,), (0,)), ((), ()))


def _params(*sem):
    return pltpu.CompilerParams(dimension_semantics=sem, vmem_limit_bytes=VMEM_LIMIT_BYTES)


def _dot(a, b):
    return jnp.dot(a, b, preferred_element_type=F32)


def _dot_hi(a, b):
    return jnp.dot(a, b, preferred_element_type=F32, precision=lax.Precision.HIGHEST)


def _resident(shape):
    nd = len(shape)
    return pl.BlockSpec(shape, lambda *_: (0,) * nd, pipeline_mode=pl.Buffered(1))


def _rms(x, g):
    ms = jnp.mean(x * x, axis=-1, keepdims=True)
    return x * lax.rsqrt(ms + RMS_EPS) * g


def _norm_mod(x, g, shift, scale):
    return _rms(x, g) * (1.0 + scale) + shift


def _silu(x):
    return x * jax.nn.sigmoid(x)


def _cs(num, den):
    ang = 2.0 * np.pi * (np.asarray(num, np.float64) % den) / den
    return np.cos(ang), np.sin(ang)


@functools.lru_cache(maxsize=None)
def _const_channel_dft():
    c, s = _cs(np.outer(np.arange(GROUP_DIM), np.arange(GROUP_DIM)), GROUP_DIM)
    return (np.concatenate([c, -s], 0) / math.sqrt(GROUP_DIM)).astype(np.float32)


@functools.lru_cache(maxsize=None)
def _const_fourier(a_len):
    n = a_len * LANES
    c, s = _cs(np.outer(np.arange(a_len), np.arange(a_len)), a_len)
    w1 = np.block([[c, s], [-s, c]])
    tc, ts = _cs(np.outer(np.arange(a_len), np.arange(LANES)), n)
    tw = np.stack([tc, -ts])
    c2, s2 = _cs(np.outer(np.arange(LANES), np.arange(LANES)), LANES)
    w2 = np.concatenate([c2, s2], 0) / math.sqrt(n)
    r = LANES // a_len
    perm = np.array([r * m + h for h in range(r) for m in range(a_len)])
    w2 = w2[:, perm]
    return w1.astype(np.float32), tw.astype(np.float32), w2.astype(np.float32)


@functools.lru_cache(maxsize=None)
def _const_hyena(a_len):
    k1n = 2 * a_len
    n = k1n * LANES
    c, s = _cs(np.outer(np.arange(k1n), np.arange(a_len)), k1n)
    w1 = np.block([[c, s], [-s, c]])
    tc, ts = _cs(np.outer(np.arange(k1n), np.arange(LANES)), n)
    tw = np.stack([tc, -ts])
    c2, s2 = _cs(np.outer(np.arange(LANES), np.arange(LANES)), LANES)
    w2 = np.block([[c2, -s2], [s2, c2]])
    w2inv = np.block([[c2, s2], [-s2, c2]])
    v1 = np.concatenate([c.T, s.T], 0) / n
    f = lambda m: m.astype(np.float32)
    return f(w1), f(tw), f(w2), f(w2inv), f(v1)


@functools.lru_cache(maxsize=None)
def _const_dense(l_len):
    n = 2 * l_len
    cl, sl = _cs(np.outer(np.arange(l_len), np.arange(l_len)), l_len)
    four_c, four_s = cl / math.sqrt(l_len), sl / math.sqrt(l_len)
    cf, sf = _cs(np.outer(np.arange(l_len), np.arange(n)), n)
    fwd = np.block([[cf, -sf], [sf, cf]])
    inv = np.block([[cf.T, sf.T], [-sf.T, cf.T]]) / n
    f = lambda m: m.astype(np.float32)
    return f(four_c), f(four_s), f(cf), f(-sf), f(fwd), f(inv)


@functools.lru_cache(maxsize=None)
def _const_filter_features(l_len):
    t01 = np.linspace(0.0, 1.0, l_len)
    bands = np.linspace(1e-4, HY_BANDS - 1, HY_BANDS)
    ang = (2.0 * np.pi / l_len) * np.arange(l_len)[:, None] * bands[None, :]
    z = np.concatenate([t01[:, None], np.cos(ang), -np.sin(ang)], -1)
    zt = np.zeros((LANES, l_len), np.float64)
    zt[:HY_EMB] = z.T
    max_decay = math.log(HY_DECAY_TARGET) / HY_FAST_DECAY
    min_decay = math.log(HY_DECAY_TARGET) / HY_SLOW_DECAY
    deltas = np.abs(np.linspace(min_decay, max_decay, HY_CH))
    return zt.astype(np.float32), t01[None, :].astype(np.float32), deltas[:, None].astype(np.float32)


def _ada_kernel(c_ref, w_ref, b_ref, o_ref):
    s = _silu(c_ref[...]).astype(BF16)
    o_ref[...] = _dot(s, w_ref[...].astype(BF16)) + b_ref[...]


def _ada(cond8, ada_w, ada_b):
    depth, d, n6 = ada_w.shape
    tn = 1536
    return pl.pallas_call(
        _ada_kernel,
        grid=(depth, n6 // tn),
        in_specs=[pl.BlockSpec((8, d), lambda i, j: (0, 0)),
                  pl.BlockSpec((None, d, tn), lambda i, j: (i, 0, j)),
                  pl.BlockSpec((None, 1, tn), lambda i, j: (i, 0, j))],
        out_specs=pl.BlockSpec((None, 8, tn), lambda i, j: (i, 0, j)),
        out_shape=jax.ShapeDtypeStruct((depth, 8, n6), F32),
        compiler_params=_params("parallel", "parallel"),
        name="ada_mod",
    )(cond8, ada_w, ada_b.reshape(depth, 1, n6))


def _inproj_kernel(x_ref, g_ref, mod_ref, wt_ref, cdft_ref, zf_ref, ph_ref):
    d = D_MODEL
    u = _norm_mod(x_ref[...], g_ref[...], mod_ref[:, 0:d], mod_ref[:, d:2 * d]).astype(BF16)
    cdft = cdft_ref[...].astype(BF16)
    for g in range(FOURIER_GROUPS):
        rows = slice(g * GROUP_DIM, (g + 1) * GROUP_DIM)
        pt = lax.dot_general(wt_ref[rows, :], u, _NT, preferred_element_type=F32)
        z = _dot(cdft, pt.astype(BF16))
        zf_ref[0, rows, :] = z[0:GROUP_DIM]
        zf_ref[1, rows, :] = z[GROUP_DIM:]
    rb = 512
    for r0 in range(FOURIER_CH, wt_ref.shape[0], rb):
        ph_ref[r0 - FOURIER_CH:r0 - FOURIER_CH + rb, :] = lax.dot_general(
            wt_ref[r0:r0 + rb, :], u, _NT, preferred_element_type=F32)


def _inproj(x, g, mod3, w_in_t, tl=1024):
    b, l, d = x.shape
    tl = min(tl, l)
    nh = w_in_t.shape[0] - FOURIER_CH
    return pl.pallas_call(
        _inproj_kernel,
        grid=(b, l // tl),
        in_specs=[pl.BlockSpec((None, tl, d), lambda i, t: (i, t, 0)),
                  pl.BlockSpec((1, d), lambda i, t: (0, 0)),
                  pl.BlockSpec((None, 1, 6 * d), lambda i, t: (i, 0, 0)),
                  _resident(w_in_t.shape),
                  _resident((2 * GROUP_DIM, GROUP_DIM))],
        out_specs=[pl.BlockSpec((None, 2, FOURIER_CH, tl), lambda i, t: (i, 0, 0, t)),
                   pl.BlockSpec((None, nh, tl), lambda i, t: (i, 0, t))],
        out_shape=[jax.ShapeDtypeStruct((b, 2, FOURIER_CH, l), F32),
                   jax.ShapeDtypeStruct((b, nh, l), F32)],
        compiler_params=_params("parallel", "parallel"),
        name="inproj",
    )(x, g, mod3, w_in_t, jnp.asarray(_const_channel_dft()))


def _twiddle_store(ys_ref, c, yr, yi, tr, ti):
    ys_ref[c, :, 0:LANES] = (yr * tr - yi * ti).astype(ys_ref.dtype)
    ys_ref[c, :, LANES:] = (yr * ti + yi * tr).astype(ys_ref.dtype)


def _fourier_kernel(z_ref, w1_ref, tw_ref, w2_ref, o_ref, zs_ref, ys_ref):
    _, ct, a_len, _ = zs_ref.shape
    w1 = w1_ref[...].astype(BF16)
    tr, ti = tw_ref[0], tw_ref[1]
    for ri in range(2):
        zs_ref[ri] = z_ref[ri].reshape(ct, a_len, LANES)

    def stage1(p, carry):
        c = 2 * p
        x = jnp.concatenate([jnp.concatenate([zs_ref[0, c + h], zs_ref[1, c + h]], 0) for h in range(2)], 1)
        y = _dot(w1, x.astype(BF16))
        for h in range(2):
            yh = y[:, h * LANES:(h + 1) * LANES]
            _twiddle_store(ys_ref, c + h, yh[0:a_len], yh[a_len:], tr, ti)
        return carry

    lax.fori_loop(0, ct // 2, stage1, 0, unroll=8)
    y = ys_ref[...].reshape(ct * a_len, 2 * LANES)
    zr = _dot(y, w2_ref[...].astype(BF16))
    tiles = []
    for c in range(ct):
        t = zr[c * a_len:(c + 1) * a_len].T
        tiles.append(jnp.concatenate([t[h * a_len:(h + 1) * a_len] for h in range(LANES // a_len)], 1))
    o_ref[...] = jnp.stack(tiles, 0).reshape(ct, a_len * LANES)


def _fourier_latent(zf, ct=32):
    b, _, ch, l = zf.shape
    a_len = l // LANES
    w1, tw, w2 = (jnp.asarray(m) for m in _const_fourier(a_len))
    return pl.pallas_call(
        _fourier_kernel,
        grid=(b, ch // ct),
        in_specs=[pl.BlockSpec((None, 2, ct, l), lambda i, c: (i, 0, c, 0)),
                  _resident(w1.shape), _resident(tw.shape), _resident(w2.shape)],
        out_specs=pl.BlockSpec((None, ct, l), lambda i, c: (i, c, 0)),
        out_shape=jax.ShapeDtypeStruct((b, ch, l), F32),
        scratch_shapes=[pltpu.VMEM((2, ct, a_len, LANES), F32),
                        pltpu.VMEM((ct, a_len, 2 * LANES), BF16)],
        compiler_params=_params("parallel", "parallel"),
        name="fourier_latent",
    )(zf, w1, tw, w2)


def _fourier_ctx_kernel(z_ref, c_ref, s_ref, o_ref):
    zr = z_ref[0].astype(BF16)
    zi = z_ref[1].astype(BF16)
    o_ref[...] = _dot(zr, c_ref[...].astype(BF16)) + _dot(zi, s_ref[...].astype(BF16))


def _fourier_ctx(zf):
    b, _, ch, l = zf.shape
    four_c, four_s = (jnp.asarray(m) for m in _const_dense(l)[:2])
    return pl.pallas_call(
        _fourier_ctx_kernel,
        grid=(b,),
        in_specs=[pl.BlockSpec((None, 2, ch, l), lambda i: (i, 0, 0, 0)),
                  _resident((l, l)), _resident((l, l))],
        out_specs=pl.BlockSpec((None, ch, l), lambda i: (i, 0, 0)),
        out_shape=jax.ShapeDtypeStruct((b, ch, l), F32),
        compiler_params=_params("parallel"),
        name="fourier_ctx",
    )(zf, four_c, four_s)


def _filter_mlp(zt, w1t, b1, freq, w2t, b2):
    h = jnp.sin(freq * (_dot_hi(w1t, zt) + b1))
    return jnp.sin(freq * (_dot_hi(w2t, h) + b2))


@functools.lru_cache(maxsize=None)
def _const_filter_fft(a_len):
    k1n = 2 * a_len
    c, s = _cs(np.outer(np.arange(k1n), np.arange(k1n)), k1n)
    w1r = np.concatenate([c, -s], 0)
    perm = np.zeros((2, LANES, LANES))
    for b in range(1, LANES):
        perm[0, LANES - b, b] = 1.0
    perm[1, 0, 0] = 1.0
    return w1r.astype(np.float32), perm.astype(np.float32)


def _hfilter_kernel(zt_ref, t_ref, trev_ref, dl_ref, w1t_ref, b1_ref, fr_ref, w2t_ref, b2_ref, w3t_ref,
                    perm_ref, w1r_ref, tw_ref, w2_ref, o_ref, h2_ref, h2r_ref, xs_ref, ys_ref):
    cf = o_ref.shape[0]
    k1n = xs_ref.shape[1]
    a_len = k1n // 2
    l = a_len * LANES

    @pl.when((pl.program_id(0) == 0) & (pl.program_id(1) == 0))
    def _():
        step = min(l, 1024)
        for s in range(0, l, step):
            h2_ref[:, s:s + step] = _filter_mlp(zt_ref[:, s:s + step], w1t_ref[...], b1_ref[...], fr_ref[...],
                                                w2t_ref[...], b2_ref[...]).astype(h2_ref.dtype)
        flip, first = perm_ref[0].astype(BF16), perm_ref[1].astype(BF16)
        for a in range(a_len):
            src = a_len - 1 - a
            r = _dot(h2_ref[:, src * LANES:(src + 1) * LANES], flip)
            if a >= 1:
                r = r + _dot(h2_ref[:, (src + 1) * LANES:(src + 2) * LANES], first)
            h2r_ref[:, a * LANES:(a + 1) * LANES] = r.astype(h2r_ref.dtype)

    w3f = w3t_ref[0].astype(BF16)
    w3b = w3t_ref[1].astype(BF16)
    dl = dl_ref[...]
    f = _dot(w3f, h2_ref[...]) * jnp.exp(-(dl * t_ref[...]))
    lane = lax.broadcasted_iota(jnp.int32, (cf, l), 1)
    g = jnp.where(lane == 0, 0.0, _dot(w3b, h2r_ref[...]) * jnp.exp(-(dl * trev_ref[...])))
    b0 = _dot(w3b, h2_ref[:, 0:LANES])[:, 0:1]
    tot = (jnp.sum(jnp.abs(f), axis=1, keepdims=True) + jnp.sum(jnp.abs(g), axis=1, keepdims=True) + jnp.abs(b0))
    inv = 1.0 / tot
    xs_ref[...] = (jnp.concatenate([f, g], 1) * inv).reshape(cf, k1n, LANES).astype(xs_ref.dtype)

    w1r = w1r_ref[...].astype(BF16)
    tr, ti = tw_ref[0], tw_ref[1]

    def stage1(p, carry):
        c = 2 * p
        y = _dot(w1r, jnp.concatenate([xs_ref[c], xs_ref[c + 1]], 1))
        for h in range(2):
            yh = y[:, h * LANES:(h + 1) * LANES]
            _twiddle_store(ys_ref, c + h, yh[0:k1n], yh[k1n:], tr, ti)
        return carry

    lax.fori_loop(0, cf // 2, stage1, 0, unroll=8)
    o_ref[...] = _dot(ys_ref[...].reshape(cf * k1n, 2 * LANES), w2_ref[...].astype(BF16)).reshape(cf, k1n, 2 * LANES)


def _hfilter_latent(l, w1t, b1, freq, w2t, b2, w3t, cf=64):
    a_len = l // LANES
    k1n = 2 * a_len
    zt, t01, deltas = _const_filter_features(l)
    trev = np.concatenate([np.zeros((1, 1), np.float32), t01[:, :0:-1]], 1)
    zt, t01, trev, deltas = (jnp.asarray(m) for m in (zt, t01, trev, deltas))
    _, tw, w2, _, _ = (jnp.asarray(m) for m in _const_hyena(a_len))
    w1r, perm = (jnp.asarray(m) for m in _const_filter_fft(a_len))
    orders = w3t.shape[1]
    return pl.pallas_call(
        _hfilter_kernel,
        grid=(orders, HY_CH // cf),
        in_specs=[_resident(zt.shape), _resident(t01.shape), _resident(trev.shape),
                  pl.BlockSpec((cf, 1), lambda o, c: (c, 0)),
                  _resident(w1t.shape), _resident(b1.shape), _resident(freq.shape),
                  _resident(w2t.shape), _resident(b2.shape),
                  pl.BlockSpec((2, None, cf, HY_HID), lambda o, c: (0, o, c, 0)),
                  _resident(perm.shape), _resident(w1r.shape), _resident(tw.shape), _resident(w2.shape)],
        out_specs=pl.BlockSpec((None, cf, k1n, 2 * LANES), lambda o, c: (o, c, 0, 0)),
        out_shape=jax.ShapeDtypeStruct((orders, HY_CH, k1n, 2 * LANES), F32),
        scratch_shapes=[pltpu.VMEM((HY_HID, l), BF16),
                        pltpu.VMEM((HY_HID, l), BF16),
                        pltpu.VMEM((cf, k1n, LANES), BF16),
                        pltpu.VMEM((cf, k1n, 2 * LANES), BF16)],
        compiler_params=_params("arbitrary", "arbitrary"),
        name="hyena_filter_latent",
    )(zt, t01, trev, deltas, w1t, b1, freq, w2t, b2, w3t, perm, w1r, tw, w2)


def _hfilter_ctx_kernel(zt_ref, t_ref, dl_ref, w1t_ref, b1_ref, fr_ref, w2t_ref, b2_ref, w3t_ref,
                        cf_ref, sf_ref, o_ref):
    l = zt_ref.shape[1]
    h2 = _filter_mlp(zt_ref[...], w1t_ref[...], b1_ref[...], fr_ref[...], w2t_ref[...], b2_ref[...])
    ch = w3t_ref.shape[1]
    w3 = w3t_ref[...].reshape(2 * ch, HY_HID)
    dl = dl_ref[...]
    hv = _dot_hi(w3, h2) * jnp.exp(-(jnp.concatenate([dl, dl], 0) * t_ref[...]))
    tot = jnp.sum(jnp.abs(hv), axis=1, keepdims=True)
    inv = 1.0 / (tot[0:ch] + tot[ch:])
    f = hv[0:ch] * inv
    lane = lax.broadcasted_iota(jnp.int32, (ch, l), 1)
    bk = jnp.where(lane == 0, 0.0, hv[ch:] * inv)
    o_ref[:, 0:2 * l] = _dot((f + bk).astype(BF16), cf_ref[...].astype(BF16))
    o_ref[:, 2 * l:] = _dot((f - bk).astype(BF16), sf_ref[...].astype(BF16))


def _hfilter_ctx(l, w1t, b1, freq, w2t, b2, w3t):
    zt, t01, deltas = (jnp.asarray(m) for m in _const_filter_features(l))
    cfm, nsf = (jnp.asarray(m) for m in _const_dense(l)[2:4])
    orders = w3t.shape[1]
    return pl.pallas_call(
        _hfilter_ctx_kernel,
        grid=(orders,),
        in_specs=[_resident(zt.shape), _resident(t01.shape), _resident(deltas.shape),
                  _resident(w1t.shape), _resident(b1.shape), _resident(freq.shape),
                  _resident(w2t.shape), _resident(b2.shape),
                  pl.BlockSpec((2, None, HY_CH, HY_HID), lambda o: (0, o, 0, 0)),
                  _resident(cfm.shape), _resident(nsf.shape)],
        out_specs=pl.BlockSpec((None, HY_CH, 4 * l), lambda o: (o, 0, 0)),
        out_shape=jax.ShapeDtypeStruct((orders, HY_CH, 4 * l), F32),
        compiler_params=_params("parallel"),
        name="hyena_filter_ctx",
    )(zt, t01, deltas, w1t, b1, freq, w2t, b2, w3t, cfm, nsf)


def _short_conv(x, cw, cb):
    l = x.shape[1]
    lane = lax.broadcasted_iota(jnp.int32, x.shape, 1)
    prev = jnp.where(lane == 0, 0.0, pltpu.roll(x, 1, 1))
    nxt = jnp.where(lane == l - 1, 0.0, pltpu.roll(x, l - 1, 1))
    return cw[:, 0:1] * prev + cw[:, 1:2] * x + cw[:, 2:3] * nxt + cb


def _short_conv_tile(x, w0, w1, w2, bias):
    a_len = x.shape[0]
    lane = lax.broadcasted_iota(jnp.int32, x.shape, 1)
    row = lax.broadcasted_iota(jnp.int32, x.shape, 0)
    xl = pltpu.roll(x, 1, 1)
    prev = jnp.where(lane == 0, jnp.where(row == 0, 0.0, pltpu.roll(xl, 1, 0)), xl)
    xr = pltpu.roll(x, LANES - 1, 1)
    nxt = jnp.where(lane == LANES - 1, jnp.where(row == a_len - 1, 0.0, pltpu.roll(xr, a_len - 1, 0)), xr)
    return w0 * prev + w1 * x + w2 * nxt + bias


def _hconv_kernel(cw_ref, cb_ref, sk_ref, x1_ref, x2_ref, v_ref, hf_ref,
                  w1_ref, tw_ref, w2_ref, w2i_ref, v1_ref, o_ref, g1_ref, g2_ref, vb_ref, ys_ref, us_ref, *, group):
    nb, ct, a_len, _ = vb_ref.shape
    k1n = 2 * a_len
    nch3 = cb_ref.shape[0]
    c0 = pl.program_id(0) * ct
    w1 = w1_ref[...].astype(BF16)
    w2 = w2_ref[...].astype(BF16)
    w2i = w2i_ref[...].astype(BF16)
    v1 = v1_ref[...].astype(BF16)
    tr, ti = tw_ref[0], tw_ref[1]

    for src, dst in ((x1_ref, g1_ref), (x2_ref, g2_ref), (v_ref, vb_ref)):
        for bi in range(nb):
            dst[bi] = src[bi].reshape(ct, a_len, LANES)

    def sconv(x, ch):
        return _short_conv_tile(x, cw_ref[ch], cw_ref[nch3 + ch], cw_ref[2 * nch3 + ch], cb_ref[ch])

    for order, gate_ref in enumerate((g1_ref, g2_ref)):
        def stage1(p, carry):
            c = 2 * p
            if order == 0:
                for h in range(2):
                    for bi in range(nb):
                        vb_ref[bi, c + h] = sconv(vb_ref[bi, c + h], 2 * HY_CH + c0 + c + h)
            x = jnp.concatenate([jnp.concatenate([vb_ref[0, c + h], vb_ref[1, c + h]], 0) for h in range(2)], 1)
            y = _dot(w1, x.astype(BF16))
            for h in range(2):
                yh = y[:, h * LANES:(h + 1) * LANES]
                _twiddle_store(ys_ref, c + h, yh[0:k1n], yh[k1n:], tr, ti)
            return carry

        lax.fori_loop(0, ct // 2, stage1, 0, unroll=8)

        def middle(g, carry):
            cs = pl.ds(g * group, group)
            z = _dot(ys_ref[cs].reshape(group * k1n, 2 * LANES), w2)
            h = hf_ref[order, cs].reshape(group * k1n, 2 * LANES)
            zr, zi = z[:, 0:LANES], z[:, LANES:]
            hr, hi = h[:, 0:LANES], h[:, LANES:]
            p = jnp.concatenate([zr * hr - zi * hi, zr * hi + zi * hr], 1).astype(BF16)
            u = _dot(p, w2i).reshape(group, k1n, 2 * LANES)
            ur, ui = u[:, :, 0:LANES], u[:, :, LANES:]
            us_ref[cs, :, 0:LANES] = (ur * tr + ui * ti).astype(BF16)
            us_ref[cs, :, LANES:] = (ui * tr - ur * ti).astype(BF16)
            return carry

        lax.fori_loop(0, ct // group, middle, 0, unroll=2)

        def stage1_inv(c, carry):
            q = _dot(v1, us_ref[c])
            conv = (q[0:a_len, 0:LANES] - q[a_len:, LANES:], q[a_len:, 0:LANES] + q[0:a_len, LANES:])
            skip = sk_ref[order * HY_CH + c0 + c]
            for bi in range(nb):
                gate = sconv(gate_ref[bi, c], order * HY_CH + c0 + c)
                vb_ref[bi, c] = gate * (conv[bi] + vb_ref[bi, c] * skip)
            return carry

        lax.fori_loop(0, ct, stage1_inv, 0, unroll=8)

    for bi in range(nb):
        o_ref[bi] = vb_ref[bi].reshape(ct, a_len * LANES)


def _hconv_latent(ph, hf, conv_w, conv_b, skip, ct=32, group=16):
    b, nch3, l = ph.shape
    assert b == 2, "the two batch entries ride as real/imaginary parts"
    a_len = l // LANES
    k1n = 2 * a_len
    w1, tw, w2, w2i, v1 = (jnp.asarray(m) for m in _const_hyena(a_len))
    nblk = HY_CH // ct
    smem = pl.BlockSpec(memory_space=pltpu.SMEM)
    part = lambda off: pl.BlockSpec((b, ct, l), lambda c: (0, c + off * nblk, 0))
    scratch = pltpu.VMEM((b, ct, a_len, LANES), F32)
    return pl.pallas_call(
        functools.partial(_hconv_kernel, group=group),
        grid=(nblk,),
        in_specs=[smem, smem, smem, part(0), part(1), part(2),
                  pl.BlockSpec((2, ct, k1n, 2 * LANES), lambda c: (0, c, 0, 0)),
                  _resident(w1.shape), _resident(tw.shape), _resident(w2.shape),
                  _resident(w2i.shape), _resident(v1.shape)],
        out_specs=pl.BlockSpec((b, ct, l), lambda c: (0, c, 0)),
        out_shape=jax.ShapeDtypeStruct((b, HY_CH, l), F32),
        scratch_shapes=[scratch, scratch, scratch,
                        pltpu.VMEM((ct, k1n, 2 * LANES), BF16),
                        pltpu.VMEM((ct, k1n, 2 * LANES), BF16)],
        compiler_params=_params("parallel"),
        name="hyena_conv_latent",
    )(conv_w.reshape(-1), conv_b, skip.reshape(-1), ph, ph, ph, hf, w1, tw, w2, w2i, v1)


def _hconv_ctx_kernel(cw_ref, cb_ref, sk_ref, p_ref, hf_ref, fwd_ref, inv_ref, o_ref):
    nb, _, l = p_ref.shape
    ch = HY_CH
    z = [_short_conv(p_ref[bi], cw_ref[...], cb_ref[...]) for bi in range(nb)]
    fwd = fwd_ref[...].astype(BF16)
    inv = inv_ref[...].astype(BF16)
    cur = [zz[2 * ch:] for zz in z]
    for order in range(2):
        spec = _dot(jnp.concatenate(cur, 1).astype(BF16), fwd)
        h = hf_ref[order]
        zr, zi, hr, hi = spec[:, 0:2 * l], spec[:, 2 * l:], h[:, 0:2 * l], h[:, 2 * l:]
        prod = jnp.concatenate([zr * hr - zi * hi, zr * hi + zi * hr], 1).astype(BF16)
        y = _dot(prod, inv)
        sk = sk_ref[:, order:order + 1]
        cur = [z[bi][order * ch:(order + 1) * ch] * (y[:, bi * l:(bi + 1) * l] + cur[bi] * sk) for bi in range(nb)]
    for bi in range(nb):
        o_ref[bi] = cur[bi]


def _hconv_ctx(ph, hf, conv_w, conv_b, skip):
    b, nch3, l = ph.shape
    assert b == 2
    fwd, inv = (jnp.asarray(m) for m in _const_dense(l)[4:6])
    cw = jnp.pad(conv_w.T, ((0, 0), (0, 5)))
    return pl.pallas_call(
        _hconv_ctx_kernel,
        in_specs=[pl.BlockSpec(cw.shape, lambda: (0, 0)),
                  pl.BlockSpec((nch3, 1), lambda: (0, 0)),
                  pl.BlockSpec((HY_CH, 2), lambda: (0, 0)),
                  pl.BlockSpec(ph.shape, lambda: (0, 0, 0)),
                  pl.BlockSpec(hf.shape, lambda: (0, 0, 0)),
                  pl.BlockSpec(fwd.shape, lambda: (0, 0)),
                  pl.BlockSpec(inv.shape, lambda: (0, 0))],
        out_specs=pl.BlockSpec((b, HY_CH, l), lambda: (0, 0, 0)),
        out_shape=jax.ShapeDtypeStruct((b, HY_CH, l), F32),
        compiler_params=pltpu.CompilerParams(vmem_limit_bytes=VMEM_LIMIT_BYTES),
        name="hyena_conv_ctx",
    )(cw, conv_b.reshape(-1, 1), skip.T, ph, hf, fwd, inv)


def _mix_ffn_kernel(*refs, n_mix, final, ff_chunk):
    d = D_MODEL
    h_ref, y_refs = refs[0], refs[1:1 + n_mix]
    wo_ref, mod_ref, g2_ref, wg_ref, wu_ref, wd_ref = refs[1 + n_mix:7 + n_mix]
    rest = refs[7 + n_mix:]
    y, row = None, 0
    for y_ref in y_refs:
        rows = y_ref.shape[0]
        part = lax.dot_general(y_ref[...].astype(BF16), wo_ref[row:row + rows, :], _TN, preferred_element_type=F32)
        y = part if y is None else y + part
        row += rows
    if final:
        fg_ref, o_ref = rest
    else:
        (o_ref,) = rest
    h1 = h_ref[...] + mod_ref[:, 2 * d:3 * d] * y
    u = _norm_mod(h1, g2_ref[...], mod_ref[:, 3 * d:4 * d], mod_ref[:, 4 * d:5 * d]).astype(BF16)
    acc = jnp.zeros(h1.shape, F32)
    for j in range(D_FF // ff_chunk):
        cols = slice(j * ff_chunk, (j + 1) * ff_chunk)
        act = (_silu(_dot(u, wg_ref[:, cols])) * _dot(u, wu_ref[:, cols])).astype(BF16)
        acc = acc + _dot(act, wd_ref[cols, :])
    h2 = h1 + mod_ref[:, 5 * d:6 * d] * acc
    if final:
        h2 = _rms(h2, fg_ref[...])
    o_ref[...] = h2


def _layer_resident(stacked, layer):
    nd = stacked.ndim - 1
    return pl.BlockSpec((None,) + stacked.shape[1:], lambda *_: (layer,) + (0,) * nd, pipeline_mode=pl.Buffered(1))


def _mix_ffn(h, ys, wo, mod3, g2, wg, wu, wd, layer, final_g=None, tl=512, ff_chunk=256):
    b, l, d = h.shape
    tl = min(tl, l)
    tok = pl.BlockSpec((None, tl, d), lambda i, t: (i, t, 0))
    y_specs = [pl.BlockSpec((None, y.shape[1], tl), lambda i, t: (i, 0, t)) for y in ys]
    in_specs = ([tok] + y_specs + [_resident(wo.shape), pl.BlockSpec((None, 1, 6 * d), lambda i, t: (i, 0, 0)),
                                    pl.BlockSpec((1, d), lambda i, t: (0, 0)),
                                    _layer_resident(wg, layer), _layer_resident(wu, layer),
                                    _layer_resident(wd, layer)])
    args = [h, *ys, wo, mod3, g2, wg, wu, wd]
    if final_g is not None:
        in_specs.append(pl.BlockSpec((1, d), lambda i, t: (0, 0)))
        args.append(final_g)
    return pl.pallas_call(
        functools.partial(_mix_ffn_kernel, n_mix=len(ys), final=final_g is not None, ff_chunk=ff_chunk),
        grid=(b, l // tl),
        in_specs=in_specs,
        out_specs=tok,
        out_shape=jax.ShapeDtypeStruct((b, l, d), F32),
        compiler_params=_params("parallel", "parallel"),
        name="mix_ffn_final" if final_g is not None else "mix_ffn",
    )(*args)


def _qkv_kernel(x_ref, g_ref, mod_ref, w_ref, q_ref, k_ref, v_ref):
    d = D_MODEL
    u = _norm_mod(x_ref[...], g_ref[...], mod_ref[:, 0:d], mod_ref[:, d:2 * d]).astype(BF16)
    q_ref[...] = (_dot(u, w_ref[:, 0:d]) * (NA_HEAD_DIM ** -0.5)).astype(BF16)
    k_ref[...] = _dot(u, w_ref[:, d:2 * d]).astype(BF16)
    v_ref[...] = _dot(u, w_ref[:, 2 * d:]).astype(BF16)


def _qkv(h, g, mod3, w, tl=1024):
    b, l, d = h.shape
    tl = min(tl, l)
    tok = pl.BlockSpec((None, tl, d), lambda i, t: (i, t, 0))
    return pl.pallas_call(
        _qkv_kernel,
        grid=(b, l // tl),
        in_specs=[tok, pl.BlockSpec((1, d), lambda i, t: (0, 0)),
                  pl.BlockSpec((None, 1, 6 * d), lambda i, t: (i, 0, 0)), _resident(w.shape)],
        out_specs=[tok, tok, tok],
        out_shape=[jax.ShapeDtypeStruct((b, l, d), BF16)] * 3,
        compiler_params=_params("parallel", "parallel"),
        name="qkv_proj",
    )(h, g, mod3, w)


NA_QROWS = 4
NA_KROWS = 12
NA_PATTERNS = 3
NA_GROUP = 4


def _na_key_start(r0, grid_rows):
    return jnp.minimum(jnp.clip(r0 - NA_KH // 2, 0, grid_rows - NA_KH), grid_rows - NA_KROWS)


def _na_pattern(pat, i, j):
    if pat == 0:
        return j - i + NA_KH - 1, j < NA_KH
    if pat == 1:
        return j - i + NA_KH // 2 - 1, i <= j < i + NA_KH
    return j - i - 1, j >= NA_KROWS - NA_KH


def _bias_blocks_kernel(rpb_ref, o_ref):
    pair = pl.program_id(0)
    ndr, ndc = 2 * NA_KH - 1, 2 * NA_KW - 1
    kcol = lax.broadcasted_iota(jnp.int32, (GRID_W, LANES), 0)
    lane = lax.broadcasted_iota(jnp.int32, (GRID_W, LANES), 1)
    q = lane % GRID_W
    upper = lane >= GRID_W
    wstart = jnp.clip(q - NA_KW // 2, 0, GRID_W - NA_KW)
    ok = (kcol >= wstart) & (kcol < wstart + NA_KW)
    dc = kcol - q + (NA_KW - 1)
    neg = jnp.full((GRID_W, LANES), NEG_BIG, F32)
    for e in range(2):
        base = (2 * pair + e) * ndr * ndc
        tiles = []
        for dr in range(ndr):
            acc = jnp.zeros((GRID_W, LANES), F32)
            for dd in range(ndc):
                acc = jnp.where(dc == dd, rpb_ref[base + dr * ndc + dd], acc)
            tiles.append(jnp.where(ok, acc, NEG_BIG))
        for pat in range(NA_PATTERNS):
            for j in range(NA_KROWS):
                for p in range(NA_QROWS // 2):
                    halves = []
                    for i in (2 * p, 2 * p + 1):
                        dr, visible = _na_pattern(pat, i, j)
                        halves.append(tiles[dr] if visible else neg)
                    o_ref[e, pat, j * GRID_W:(j + 1) * GRID_W, p * LANES:(p + 1) * LANES] = jnp.where(
                        upper, halves[1], halves[0])


def _bias_blocks(rpb):
    heads = rpb.shape[0]
    nq, nk = NA_QROWS * GRID_W, NA_KROWS * GRID_W
    return pl.pallas_call(
        _bias_blocks_kernel,
        grid=(heads // 2,),
        in_specs=[pl.BlockSpec(memory_space=pltpu.SMEM)],
        out_specs=pl.BlockSpec((None, 2, NA_PATTERNS, nk, nq), lambda p: (p, 0, 0, 0, 0)),
        out_shape=jax.ShapeDtypeStruct((heads // 2, 2, NA_PATTERNS, nk, nq), F32),
        compiler_params=_params("parallel"),
        name="na_bias_blocks",
    )(rpb.reshape(-1))


def _na_kernel(q_ref, k_ref, v_ref, kc_ref, vc_ref, bias_ref, o_ref, *, blocks_per_step, grid_rows):
    nq, nk = NA_QROWS * GRID_W, NA_KROWS * GRID_W
    lane = lax.broadcasted_iota(jnp.int32, (nq, LANES), 1)
    row = lax.broadcasted_iota(jnp.int32, (LANES, nq), 0)
    kc = kc_ref[...]
    vc = vc_ref[...]
    step = pl.program_id(2)
    for i in range(0, blocks_per_step, NA_GROUP):
        qs, kus, vts, pats = [], [], [], []
        for j in range(i, i + NA_GROUP):
            r0 = (step * blocks_per_step + j) * NA_QROWS
            pats.append(jnp.where(r0 == 0, 0, jnp.where(r0 == grid_rows - NA_QROWS, 2, 1)))
            koff = pl.multiple_of(_na_key_start(r0, grid_rows) * GRID_W, GRID_W)
            qs.append(q_ref[j * nq:(j + 1) * nq, :])
            kus.append(k_ref[pl.ds(koff, nk), :])
            vt = jnp.concatenate([v_ref[pl.ds(koff, nk), :], vc], 0).T
            vts.append(jnp.concatenate([vt, jnp.ones((16, vt.shape[1]), vt.dtype)], 0))
        ss = {}
        for jj in range(NA_GROUP):
            for e in range(2):
                head = (lane >= NA_HEAD_DIM) if e else (lane < NA_HEAD_DIM)
                qh = jnp.where(head, qs[jj], jnp.zeros_like(qs[jj]))
                sl = lax.dot_general(kus[jj], qh, _NT, preferred_element_type=F32) + bias_ref[e, pats[jj]]
                sc = lax.dot_general(kc, qh, _NT, preferred_element_type=F32)
                ss[jj, e] = jnp.concatenate([sl, sc], 0)
        ps = {key: jnp.exp((s - jnp.max(s, axis=0, keepdims=True)).astype(BF16)) for key, s in ss.items()}
        for jj in range(NA_GROUP):
            outs = []
            for e in range(2):
                o = _dot(vts[jj], ps[jj, e])
                outs.append(o[0:LANES] / o[LANES:LANES + 1])
            j = i + jj
            o_ref[:, j * nq:(j + 1) * nq] = jnp.where(row < NA_HEAD_DIM, outs[0], outs[1]).astype(o_ref.dtype)


def _na(q, k, v, kc, vc, bias_blocks, blocks_per_step=16):
    b, l, d = q.shape
    lc = kc.shape[1]
    grid_rows = l // GRID_W
    blocks_per_step = min(blocks_per_step, grid_rows // NA_QROWS)
    assert grid_rows % (NA_QROWS * blocks_per_step) == 0 and grid_rows >= NA_KROWS
    assert blocks_per_step % NA_GROUP == 0
    npairs = d // LANES
    tq = blocks_per_step * NA_QROWS * GRID_W
    return pl.pallas_call(
        functools.partial(_na_kernel, blocks_per_step=blocks_per_step, grid_rows=grid_rows),
        grid=(npairs, b, l // tq),
        in_specs=[pl.BlockSpec((None, tq, LANES), lambda p, i, t: (i, t, p)),
                  pl.BlockSpec((None, l, LANES), lambda p, i, t: (i, 0, p)),
                  pl.BlockSpec((None, l, LANES), lambda p, i, t: (i, 0, p)),
                  pl.BlockSpec((None, lc, LANES), lambda p, i, t: (i, 0, p)),
                  pl.BlockSpec((None, lc, LANES), lambda p, i, t: (i, 0, p)),
                  pl.BlockSpec((None,) + bias_blocks.shape[1:], lambda p, i, t: (p, 0, 0, 0, 0))],
        out_specs=pl.BlockSpec((None, LANES, tq), lambda p, i, t: (i, p, t)),
        out_shape=jax.ShapeDtypeStruct((b, d, l), BF16),
        compiler_params=_params("parallel", "parallel", "parallel"),
        name="neighbourhood_attention",
    )(q, k, v, kc, vc, bias_blocks)


def kernel(x, c, ctx, c_ctx, ada_w, ada_b, norm1_g, norm2_g, ffn_w_gate, ffn_w_up, ffn_w_down, ab_w_in, ab_w_out, hy_conv_w, hy_conv_b, hy_f_w1, hy_f_b1, hy_f_freq, hy_f_w2, hy_f_b2, hy_f_w3, hy_bias, na_w_qkv, na_w_out, na_rpb, final_g):
    b, l, d = x.shape
    lc = ctx.shape[1]
    assert b == 2 and d == D_MODEL and l % 1024 == 0 and ada_w.shape[0] == 2

    cond8 = jnp.zeros((8, d), F32).at[0:b].set(c).at[b].set(c_ctx)
    mod = _ada(cond8, ada_w, ada_b)
    mod_lat = [mod[i, 0:b].reshape(b, 1, 6 * d) for i in range(2)]
    mod_ctx = [jnp.broadcast_to(mod[i, b:b + 1], (b, 6 * d)).reshape(b, 1, 6 * d) for i in range(2)]
    n1 = [norm1_g[i].reshape(1, d) for i in range(2)]
    n2 = [norm2_g[i].reshape(1, d) for i in range(2)]
    ffn = (ffn_w_gate.astype(BF16), ffn_w_up.astype(BF16), ffn_w_down.astype(BF16))

    w_in_t = ab_w_in[0].T.astype(BF16)
    w_out = ab_w_out[0].astype(BF16)
    w1t = jnp.pad(hy_f_w1[0].T, ((0, 0), (0, LANES - HY_EMB)))
    col = lambda vec: vec.reshape(-1, 1)
    mlp = (w1t, col(hy_f_b1[0]), col(hy_f_freq[0]), hy_f_w2[0].T, col(hy_f_b2[0]),
           hy_f_w3[0].T.reshape(2, 2, HY_CH, HY_HID))

    zf, ph = _inproj(x, n1[0], mod_lat[0], w_in_t)
    y_f = _fourier_latent(zf)
    hf = _hfilter_latent(l, *mlp)
    y_h = _hconv_latent(ph, hf, hy_conv_w[0], hy_conv_b[0], hy_bias[0])
    h_lat = _mix_ffn(x, (y_f, y_h), w_out, mod_lat[0], n2[0], *ffn, layer=0)

    zfc, phc = _inproj(ctx, n1[0], mod_ctx[0], w_in_t)
    y_fc = _fourier_ctx(zfc)
    hfc = _hfilter_ctx(lc, *mlp)
    y_hc = _hconv_ctx(phc, hfc, hy_conv_w[0], hy_conv_b[0], hy_bias[0])
    h_ctx = _mix_ffn(ctx, (y_fc, y_hc), w_out, mod_ctx[0], n2[0], *ffn, layer=0)

    w_qkv = na_w_qkv[0].astype(BF16)
    q, k, v = _qkv(h_lat, n1[1], mod_lat[1], w_qkv)
    _, kc, vc = _qkv(h_ctx, n1[1], mod_ctx[1], w_qkv)
    attn = _na(q, k, v, kc, vc, _bias_blocks(na_rpb[0]))
    return _mix_ffn(h_lat, (attn,), na_w_out[0].astype(BF16), mod_lat[1], n2[1], *ffn, layer=1,
                    final_g=final_g.reshape(1, d))
```

```python
import functools
import math

import numpy as np
import jax
import jax.numpy as jnp
from jax import lax
from jax.experimental import pallas as pl
from jax.experimental.pallas import tpu as pltpu

F32 = jnp.float32
BF16 = jnp.bfloat16

D_MODEL = 1024
FOURIER_CH = 512
FOURIER_GROUPS = 4
GROUP_DIM = 128
HY_CH = 512
HY_BANDS = 16
HY_EMB = 33
HY_HID = 64
HY_FAST_DECAY = 0.3
HY_SLOW_DECAY = 1.5
HY_DECAY_TARGET = 1e-2
D_FF = 2816
NA_HEADS = 16
NA_HEAD_DIM = 64
NA_KH = 8
NA_KW = 16
GRID_W = 64
RMS_EPS = 1e-6
NEG_BIG = -1e30

LANES = 128
VMEM_LIMIT_BYTES = 56 * 1024 * 1024

_NT = (((1,), (1,)), ((), ()))
_TN = (((0,), (0,)), ((), ()))


def _params(*sem):
    return pltpu.CompilerParams(dimension_semantics=sem, vmem_limit_bytes=VMEM_LIMIT_BYTES)


def _dot(a, b):
    return jnp.dot(a, b, preferred_element_type=F32)


def _dot_hi(a, b):
    return jnp.dot(a, b, preferred_element_type=F32, precision=lax.Precision.HIGHEST)


def _resident(shape):
    nd = len(shape)
    return pl.BlockSpec(shape, lambda *_: (0,) * nd, pipeline_mode=pl.Buffered(1))


def _rms(x, g):
    ms = jnp.mean(x * x, axis=-1, keepdims=True)
    return x * lax.rsqrt(ms + RMS_EPS) * g


def _norm_mod(x, g, shift, scale):
    return _rms(x, g) * (1.0 + scale) + shift


def _silu(x):
    return x * jax.nn.sigmoid(x)


def _cs(num, den):
    ang = 2.0 * np.pi * (np.asarray(num, np.float64) % den) / den
    return np.cos(ang), np.sin(ang)


@functools.lru_cache(maxsize=None)
def _const_channel_dft():
    c, s = _cs(np.outer(np.arange(GROUP_DIM), np.arange(GROUP_DIM)), GROUP_DIM)
    return (np.concatenate([c, -s], 0) / math.sqrt(GROUP_DIM)).astype(np.float32)


@functools.lru_cache(maxsize=None)
def _const_fourier(a_len):
    n = a_len * LANES
    c, s = _cs(np.outer(np.arange(a_len), np.arange(a_len)), a_len)
    w1 = np.block([[c, s], [-s, c]])
    tc, ts = _cs(np.outer(np.arange(a_len), np.arange(LANES)), n)
    tw = np.stack([tc, -ts])
    c2, s2 = _cs(np.outer(np.arange(LANES), np.arange(LANES)), LANES)
    w2 = np.concatenate([c2, s2], 0) / math.sqrt(n)
    r = LANES // a_len
    perm = np.array([r * m + h for h in range(r) for m in range(a_len)])
    w2 = w2[:, perm]
    return w1.astype(np.float32), tw.astype(np.float32), w2.astype(np.float32)


@functools.lru_cache(maxsize=None)
def _const_hyena(a_len):
    k1n = 2 * a_len
    n = k1n * LANES
    c, s = _cs(np.outer(np.arange(k1n), np.arange(a_len)), k1n)
    w1 = np.block([[c, s], [-s, c]])
    tc, ts = _cs(np.outer(np.arange(k1n), np.arange(LANES)), n)
    tw = np.stack([tc, -ts])
    c2, s2 = _cs(np.outer(np.arange(LANES), np.arange(LANES)), LANES)
    w2 = np.block([[c2, -s2], [s2, c2]])
    w2inv = np.block([[c2, s2], [-s2, c2]])
    v1 = np.concatenate([c.T, s.T], 0) / n
    f = lambda m: m.astype(np.float32)
    return f(w1), f(tw), f(w2), f(w2inv), f(v1)


@functools.lru_cache(maxsize=None)
def _const_dense(l_len):
    n = 2 * l_len
    cl, sl = _cs(np.outer(np.arange(l_len), np.arange(l_len)), l_len)
    four_c, four_s = cl / math.sqrt(l_len), sl / math.sqrt(l_len)
    cf, sf = _cs(np.outer(np.arange(l_len), np.arange(n)), n)
    fwd = np.block([[cf, -sf], [sf, cf]])
    inv = np.block([[cf.T, sf.T], [-sf.T, cf.T]]) / n
    f = lambda m: m.astype(np.float32)
    return f(four_c), f(four_s), f(cf), f(-sf), f(fwd), f(inv)


@functools.lru_cache(maxsize=None)
def _const_filter_features(l_len):
    t01 = np.linspace(0.0, 1.0, l_len)
    bands = np.linspace(1e-4, HY_BANDS - 1, HY_BANDS)
    ang = (2.0 * np.pi / l_len) * np.arange(l_len)[:, None] * bands[None, :]
    z = np.concatenate([t01[:, None], np.cos(ang), -np.sin(ang)], -1)
    zt = np.zeros((LANES, l_len), np.float64)
    zt[:HY_EMB] = z.T
    max_decay = math.log(HY_DECAY_TARGET) / HY_FAST_DECAY
    min_decay = math.log(HY_DECAY_TARGET) / HY_SLOW_DECAY
    deltas = np.abs(np.linspace(min_decay, max_decay, HY_CH))
    return zt.astype(np.float32), t01[None, :].astype(np.float32), deltas[:, None].astype(np.float32)


def _ada_kernel(c_ref, w_ref, b_ref, o_ref):
    s = _silu(c_ref[...]).astype(BF16)
    o_ref[...] = _dot(s, w_ref[...].astype(BF16)) + b_ref[...]


def _ada(cond8, ada_w, ada_b):
    depth, d, n6 = ada_w.shape
    tn = 1536
    return pl.pallas_call(
        _ada_kernel,
        grid=(depth, n6 // tn),
        in_specs=[pl.BlockSpec((8, d), lambda i, j: (0, 0)),
                  pl.BlockSpec((None, d, tn), lambda i, j: (i, 0, j)),
                  pl.BlockSpec((None, 1, tn), lambda i, j: (i, 0, j))],
        out_specs=pl.BlockSpec((None, 8, tn), lambda i, j: (i, 0, j)),
        out_shape=jax.ShapeDtypeStruct((depth, 8, n6), F32),
        compiler_params=_params("parallel", "parallel"),
        name="ada_mod",
    )(cond8, ada_w, ada_b.reshape(depth, 1, n6))


def _inproj_kernel(x_ref, g_ref, mod_ref, wt_ref, cdft_ref, zf_ref, ph_ref):
    d = D_MODEL
    u = _norm_mod(x_ref[...], g_ref[...], mod_ref[:, 0:d], mod_ref[:, d:2 * d]).astype(BF16)
    cdft = cdft_ref[...].astype(BF16)
    pf = lax.dot_general(wt_ref[0:FOURIER_CH, :], u, _NT, preferred_element_type=F32)
    rb = 512
    for r0 in range(FOURIER_CH, wt_ref.shape[0], rb):
        ph_ref[r0 - FOURIER_CH:r0 - FOURIER_CH + rb, :] = lax.dot_general(
            wt_ref[r0:r0 + rb, :], u, _NT, preferred_element_type=F32)
    for g in range(FOURIER_GROUPS):
        rows = slice(g * GROUP_DIM, (g + 1) * GROUP_DIM)
        z = _dot(cdft, pf[rows].astype(BF16))
        zf_ref[0, rows, :] = z[0:GROUP_DIM]
        zf_ref[1, rows, :] = z[GROUP_DIM:]


def _inproj(x, g, mod3, w_in_t, tl=1024):
    b, l, d = x.shape
    tl = min(tl, l)
    nh = w_in_t.shape[0] - FOURIER_CH
    return pl.pallas_call(
        _inproj_kernel,
        grid=(b, l // tl),
        in_specs=[pl.BlockSpec((None, tl, d), lambda i, t: (i, t, 0)),
                  pl.BlockSpec((1, d), lambda i, t: (0, 0)),
                  pl.BlockSpec((None, 1, 6 * d), lambda i, t: (i, 0, 0)),
                  _resident(w_in_t.shape),
                  _resident((2 * GROUP_DIM, GROUP_DIM))],
        out_specs=[pl.BlockSpec((None, 2, FOURIER_CH, tl), lambda i, t: (i, 0, 0, t)),
                   pl.BlockSpec((None, nh, tl), lambda i, t: (i, 0, t))],
        out_shape=[jax.ShapeDtypeStruct((b, 2, FOURIER_CH, l), F32),
                   jax.ShapeDtypeStruct((b, nh, l), F32)],
        compiler_params=_params("parallel", "parallel"),
        name="inproj",
    )(x, g, mod3, w_in_t, jnp.asarray(_const_channel_dft()))


def _twiddle_store(ys_ref, c, yr, yi, tr, ti):
    ys_ref[c, :, 0:LANES] = (yr * tr - yi * ti).astype(ys_ref.dtype)
    ys_ref[c, :, LANES:] = (yr * ti + yi * tr).astype(ys_ref.dtype)


def _fourier_kernel(z_ref, w1_ref, tw_ref, w2_ref, o_ref, zs_ref, ys_ref):
    _, ct, a_len, _ = zs_ref.shape
    w1 = w1_ref[...].astype(BF16)
    tr, ti = tw_ref[0], tw_ref[1]
    for ri in range(2):
        zs_ref[ri] = z_ref[ri].reshape(ct, a_len, LANES)

    def stage1(p, carry):
        c = 2 * p
        x = jnp.concatenate([jnp.concatenate([zs_ref[0, c + h], zs_ref[1, c + h]], 0) for h in range(2)], 1)
        y = _dot(w1, x.astype(BF16))
        for h in range(2):
            yh = y[:, h * LANES:(h + 1) * LANES]
            _twiddle_store(ys_ref, c + h, yh[0:a_len], yh[a_len:], tr, ti)
        return carry

    lax.fori_loop(0, ct // 2, stage1, 0, unroll=8)
    y = ys_ref[...].reshape(ct * a_len, 2 * LANES)
    zr = _dot(y, w2_ref[...].astype(BF16))
    tiles = []
    for c in range(ct):
        t = zr[c * a_len:(c + 1) * a_len].T
        tiles.append(jnp.concatenate([t[h * a_len:(h + 1) * a_len] for h in range(LANES // a_len)], 1))
    o_ref[...] = jnp.stack(tiles, 0).reshape(ct, a_len * LANES)


def _fourier_latent(zf, ct=32):
    b, _, ch, l = zf.shape
    a_len = l // LANES
    w1, tw, w2 = (jnp.asarray(m) for m in _const_fourier(a_len))
    return pl.pallas_call(
        _fourier_kernel,
        grid=(b, ch // ct),
        in_specs=[pl.BlockSpec((None, 2, ct, l), lambda i, c: (i, 0, c, 0)),
                  _resident(w1.shape), _resident(tw.shape), _resident(w2.shape)],
        out_specs=pl.BlockSpec((None, ct, l), lambda i, c: (i, c, 0)),
        out_shape=jax.ShapeDtypeStruct((b, ch, l), F32),
        scratch_shapes=[pltpu.VMEM((2, ct, a_len, LANES), F32),
                        pltpu.VMEM((ct, a_len, 2 * LANES), BF16)],
        compiler_params=_params("parallel", "parallel"),
        name="fourier_latent",
    )(zf, w1, tw, w2)


def _fourier_ctx_kernel(z_ref, c_ref, s_ref, o_ref):
    zr = z_ref[0].astype(BF16)
    zi = z_ref[1].astype(BF16)
    o_ref[...] = _dot(zr, c_ref[...].astype(BF16)) + _dot(zi, s_ref[...].astype(BF16))


def _fourier_ctx(zf):
    b, _, ch, l = zf.shape
    four_c, four_s = (jnp.asarray(m) for m in _const_dense(l)[:2])
    return pl.pallas_call(
        _fourier_ctx_kernel,
        grid=(b,),
        in_specs=[pl.BlockSpec((None, 2, ch, l), lambda i: (i, 0, 0, 0)),
                  _resident((l, l)), _resident((l, l))],
        out_specs=pl.BlockSpec((None, ch, l), lambda i: (i, 0, 0)),
        out_shape=jax.ShapeDtypeStruct((b, ch, l), F32),
        compiler_params=_params("parallel"),
        name="fourier_ctx",
    )(zf, four_c, four_s)


def _filter_mlp(zt, w1t, b1, freq, w2t, b2):
    h = jnp.sin(freq * (_dot_hi(w1t, zt) + b1))
    return jnp.sin(freq * (_dot_hi(w2t, h) + b2))


@functools.lru_cache(maxsize=None)
def _const_filter_fft(a_len):
    k1n = 2 * a_len
    c, s = _cs(np.outer(np.arange(k1n), np.arange(k1n)), k1n)
    w1r = np.concatenate([c, -s], 0)
    perm = np.zeros((2, LANES, LANES))
    for b in range(1, LANES):
        perm[0, LANES - b, b] = 1.0
    perm[1, 0, 0] = 1.0
    return w1r.astype(np.float32), perm.astype(np.float32)


def _hfilter_kernel(zt_ref, t_ref, trev_ref, dl_ref, w1t_ref, b1_ref, fr_ref, w2t_ref, b2_ref, w3t_ref,
                    perm_ref, w1r_ref, tw_ref, w2_ref, o_ref, h2_ref, h2r_ref, xs_ref, ys_ref):
    cf = o_ref.shape[0]
    k1n = xs_ref.shape[1]
    a_len = k1n // 2
    l = a_len * LANES

    @pl.when((pl.program_id(0) == 0) & (pl.program_id(1) == 0))
    def _():
        step = min(l, 1024)
        for s in range(0, l, step):
            h2_ref[:, s:s + step] = _filter_mlp(zt_ref[:, s:s + step], w1t_ref[...], b1_ref[...], fr_ref[...],
                                                w2t_ref[...], b2_ref[...]).astype(h2_ref.dtype)
        flip, first = perm_ref[0].astype(BF16), perm_ref[1].astype(BF16)
        for a in range(a_len):
            src = a_len - 1 - a
            r = _dot(h2_ref[:, src * LANES:(src + 1) * LANES], flip)
            if a >= 1:
                r = r + _dot(h2_ref[:, (src + 1) * LANES:(src + 2) * LANES], first)
            h2r_ref[:, a * LANES:(a + 1) * LANES] = r.astype(h2r_ref.dtype)

    w3f = w3t_ref[0].astype(BF16)
    w3b = w3t_ref[1].astype(BF16)
    dl = dl_ref[...]
    f = _dot(w3f, h2_ref[...]) * jnp.exp(-(dl * t_ref[...]))
    lane = lax.broadcasted_iota(jnp.int32, (cf, l), 1)
    g = jnp.where(lane == 0, 0.0, _dot(w3b, h2r_ref[...]) * jnp.exp(-(dl * trev_ref[...])))
    b0 = _dot(w3b, h2_ref[:, 0:LANES])[:, 0:1]
    tot = (jnp.sum(jnp.abs(f), axis=1, keepdims=True) + jnp.sum(jnp.abs(g), axis=1, keepdims=True) + jnp.abs(b0))
    inv = 1.0 / tot
    xs_ref[...] = (jnp.concatenate([f, g], 1) * inv).reshape(cf, k1n, LANES).astype(xs_ref.dtype)

    w1r = w1r_ref[...].astype(BF16)
    tr, ti = tw_ref[0], tw_ref[1]

    def stage1(p, carry):
        c = 2 * p
        y = _dot(w1r, jnp.concatenate([xs_ref[c], xs_ref[c + 1]], 1))
        for h in range(2):
            yh = y[:, h * LANES:(h + 1) * LANES]
            _twiddle_store(ys_ref, c + h, yh[0:k1n], yh[k1n:], tr, ti)
        return carry

    lax.fori_loop(0, cf // 2, stage1, 0, unroll=8)
    o_ref[...] = _dot(ys_ref[...].reshape(cf * k1n, 2 * LANES), w2_ref[...].astype(BF16)).reshape(cf, k1n, 2 * LANES)


def _hfilter_latent(l, w1t, b1, freq, w2t, b2, w3t, cf=64):
    a_len = l // LANES
    k1n = 2 * a_len
    zt, t01, deltas = _const_filter_features(l)
    trev = np.concatenate([np.zeros((1, 1), np.float32), t01[:, :0:-1]], 1)
    zt, t01, trev, deltas = (jnp.asarray(m) for m in (zt, t01, trev, deltas))
    _, tw, w2, _, _ = (jnp.asarray(m) for m in _const_hyena(a_len))
    w1r, perm = (jnp.asarray(m) for m in _const_filter_fft(a_len))
    orders = w3t.shape[1]
    return pl.pallas_call(
        _hfilter_kernel,
        grid=(orders, HY_CH // cf),
        in_specs=[_resident(zt.shape), _resident(t01.shape), _resident(trev.shape),
                  pl.BlockSpec((cf, 1), lambda o, c: (c, 0)),
                  _resident(w1t.shape), _resident(b1.shape), _resident(freq.shape),
                  _resident(w2t.shape), _resident(b2.shape),
                  pl.BlockSpec((2, None, cf, HY_HID), lambda o, c: (0, o, c, 0)),
                  _resident(perm.shape), _resident(w1r.shape), _resident(tw.shape), _resident(w2.shape)],
        out_specs=pl.BlockSpec((None, cf, k1n, 2 * LANES), lambda o, c: (o, c, 0, 0)),
        out_shape=jax.ShapeDtypeStruct((orders, HY_CH, k1n, 2 * LANES), F32),
        scratch_shapes=[pltpu.VMEM((HY_HID, l), BF16),
                        pltpu.VMEM((HY_HID, l), BF16),
                        pltpu.VMEM((cf, k1n, LANES), BF16),
                        pltpu.VMEM((cf, k1n, 2 * LANES), BF16)],
        compiler_params=_params("arbitrary", "arbitrary"),
        name="hyena_filter_latent",
    )(zt, t01, trev, deltas, w1t, b1, freq, w2t, b2, w3t, perm, w1r, tw, w2)


def _hfilter_ctx_kernel(zt_ref, t_ref, dl_ref, w1t_ref, b1_ref, fr_ref, w2t_ref, b2_ref, w3t_ref,
                        cf_ref, sf_ref, o_ref):
    l = zt_ref.shape[1]
    h2 = _filter_mlp(zt_ref[...], w1t_ref[...], b1_ref[...], fr_ref[...], w2t_ref[...], b2_ref[...])
    ch = w3t_ref.shape[1]
    w3 = w3t_ref[...].reshape(2 * ch, HY_HID)
    dl = dl_ref[...]
    hv = _dot_hi(w3, h2) * jnp.exp(-(jnp.concatenate([dl, dl], 0) * t_ref[...]))
    tot = jnp.sum(jnp.abs(hv), axis=1, keepdims=True)
    inv = 1.0 / (tot[0:ch] + tot[ch:])
    f = hv[0:ch] * inv
    lane = lax.broadcasted_iota(jnp.int32, (ch, l), 1)
    bk = jnp.where(lane == 0, 0.0, hv[ch:] * inv)
    o_ref[:, 0:2 * l] = _dot((f + bk).astype(BF16), cf_ref[...].astype(BF16))
    o_ref[:, 2 * l:] = _dot((f - bk).astype(BF16), sf_ref[...].astype(BF16))


def _hfilter_ctx(l, w1t, b1, freq, w2t, b2, w3t):
    zt, t01, deltas = (jnp.asarray(m) for m in _const_filter_features(l))
    cfm, nsf = (jnp.asarray(m) for m in _const_dense(l)[2:4])
    orders = w3t.shape[1]
    return pl.pallas_call(
        _hfilter_ctx_kernel,
        grid=(orders,),
        in_specs=[_resident(zt.shape), _resident(t01.shape), _resident(deltas.shape),
                  _resident(w1t.shape), _resident(b1.shape), _resident(freq.shape),
                  _resident(w2t.shape), _resident(b2.shape),
                  pl.BlockSpec((2, None, HY_CH, HY_HID), lambda o: (0, o, 0, 0)),
                  _resident(cfm.shape), _resident(nsf.shape)],
        out_specs=pl.BlockSpec((None, HY_CH, 4 * l), lambda o: (o, 0, 0)),
        out_shape=jax.ShapeDtypeStruct((orders, HY_CH, 4 * l), F32),
        compiler_params=_params("parallel"),
        name="hyena_filter_ctx",
    )(zt, t01, deltas, w1t, b1, freq, w2t, b2, w3t, cfm, nsf)


def _short_conv(x, cw, cb):
    l = x.shape[1]
    lane = lax.broadcasted_iota(jnp.int32, x.shape, 1)
    prev = jnp.where(lane == 0, 0.0, pltpu.roll(x, 1, 1))
    nxt = jnp.where(lane == l - 1, 0.0, pltpu.roll(x, l - 1, 1))
    return cw[:, 0:1] * prev + cw[:, 1:2] * x + cw[:, 2:3] * nxt + cb


def _short_conv_tile(x, w0, w1, w2, bias):
    a_len = x.shape[0]
    lane = lax.broadcasted_iota(jnp.int32, x.shape, 1)
    row = lax.broadcasted_iota(jnp.int32, x.shape, 0)
    xl = pltpu.roll(x, 1, 1)
    prev = jnp.where(lane == 0, jnp.where(row == 0, 0.0, pltpu.roll(xl, 1, 0)), xl)
    xr = pltpu.roll(x, LANES - 1, 1)
    nxt = jnp.where(lane == LANES - 1, jnp.where(row == a_len - 1, 0.0, pltpu.roll(xr, a_len - 1, 0)), xr)
    return w0 * prev + w1 * x + w2 * nxt + bias


def _hconv_kernel(cw_ref, cb_ref, sk_ref, x1_ref, x2_ref, v_ref, hf_ref,
                  w1_ref, tw_ref, w2_ref, w2i_ref, v1_ref, o_ref, g1_ref, g2_ref, vb_ref, ys_ref, us_ref, *, group):
    nb, ct, a_len, _ = vb_ref.shape
    k1n = 2 * a_len
    nch3 = cb_ref.shape[0]
    c0 = pl.program_id(0) * ct
    w1 = w1_ref[...].astype(BF16)
    w2 = w2_ref[...].astype(BF16)
    w2i = w2i_ref[...].astype(BF16)
    v1 = v1_ref[...].astype(BF16)
    tr, ti = tw_ref[0], tw_ref[1]

    for src, dst in ((x1_ref, g1_ref), (x2_ref, g2_ref), (v_ref, vb_ref)):
        for bi in range(nb):
            dst[bi] = src[bi].reshape(ct, a_len, LANES)

    def sconv(x, ch):
        return _short_conv_tile(x, cw_ref[ch], cw_ref[nch3 + ch], cw_ref[2 * nch3 + ch], cb_ref[ch])

    for order, gate_ref in enumerate((g1_ref, g2_ref)):
        def stage1(p, carry):
            c = 2 * p
            if order == 0:
                for h in range(2):
                    for bi in range(nb):
                        vb_ref[bi, c + h] = sconv(vb_ref[bi, c + h], 2 * HY_CH + c0 + c + h)
            x = jnp.concatenate([jnp.concatenate([vb_ref[0, c + h], vb_ref[1, c + h]], 0) for h in range(2)], 1)
            y = _dot(w1, x.astype(BF16))
            for h in range(2):
                yh = y[:, h * LANES:(h + 1) * LANES]
                _twiddle_store(ys_ref, c + h, yh[0:k1n], yh[k1n:], tr, ti)
            return carry

        lax.fori_loop(0, ct // 2, stage1, 0, unroll=8)

        def middle(g, carry):
            cs = pl.ds(g * group, group)
            z = _dot(ys_ref[cs].reshape(group * k1n, 2 * LANES), w2)
            h = hf_ref[order, cs].reshape(group * k1n, 2 * LANES)
            zr, zi = z[:, 0:LANES], z[:, LANES:]
            hr, hi = h[:, 0:LANES], h[:, LANES:]
            p = jnp.concatenate([zr * hr - zi * hi, zr * hi + zi * hr], 1).astype(BF16)
            u = _dot(p, w2i).reshape(group, k1n, 2 * LANES)
            ur, ui = u[:, :, 0:LANES], u[:, :, LANES:]
            us_ref[cs, :, 0:LANES] = (ur * tr + ui * ti).astype(BF16)
            us_ref[cs, :, LANES:] = (ui * tr - ur * ti).astype(BF16)
            return carry

        lax.fori_loop(0, ct // group, middle, 0, unroll=2)

        def stage1_inv(c, carry):
            q = _dot(v1, us_ref[c])
            conv = (q[0:a_len, 0:LANES] - q[a_len:, LANES:], q[a_len:, 0:LANES] + q[0:a_len, LANES:])
            skip = sk_ref[order * HY_CH + c0 + c]
            for bi in range(nb):
                gate = sconv(gate_ref[bi, c], order * HY_CH + c0 + c)
                vb_ref[bi, c] = gate * (conv[bi] + vb_ref[bi, c] * skip)
            return carry

        lax.fori_loop(0, ct, stage1_inv, 0, unroll=8)

    for bi in range(nb):
        o_ref[bi] = vb_ref[bi].reshape(ct, a_len * LANES)


def _hconv_latent(ph, hf, conv_w, conv_b, skip, ct=32, group=16):
    b, nch3, l = ph.shape
    assert b == 2, "the two batch entries ride as real/imaginary parts"
    a_len = l // LANES
    k1n = 2 * a_len
    w1, tw, w2, w2i, v1 = (jnp.asarray(m) for m in _const_hyena(a_len))
    nblk = HY_CH // ct
    smem = pl.BlockSpec(memory_space=pltpu.SMEM)
    part = lambda off: pl.BlockSpec((b, ct, l), lambda c: (0, c + off * nblk, 0))
    scratch = pltpu.VMEM((b, ct, a_len, LANES), F32)
    return pl.pallas_call(
        functools.partial(_hconv_kernel, group=group),
        grid=(nblk,),
        in_specs=[smem, smem, smem, part(0), part(1), part(2),
                  pl.BlockSpec((2, ct, k1n, 2 * LANES), lambda c: (0, c, 0, 0)),
                  _resident(w1.shape), _resident(tw.shape), _resident(w2.shape),
                  _resident(w2i.shape), _resident(v1.shape)],
        out_specs=pl.BlockSpec((b, ct, l), lambda c: (0, c, 0)),
        out_shape=jax.ShapeDtypeStruct((b, HY_CH, l), F32),
        scratch_shapes=[scratch, scratch, scratch,
                        pltpu.VMEM((ct, k1n, 2 * LANES), BF16),
                        pltpu.VMEM((ct, k1n, 2 * LANES), BF16)],
        compiler_params=_params("parallel"),
        name="hyena_conv_latent",
    )(conv_w.reshape(-1), conv_b, skip.reshape(-1), ph, ph, ph, hf, w1, tw, w2, w2i, v1)


def _hconv_ctx_kernel(cw_ref, cb_ref, sk_ref, p_ref, hf_ref, fwd_ref, inv_ref, o_ref):
    nb, _, l = p_ref.shape
    ch = HY_CH
    z = [_short_conv(p_ref[bi], cw_ref[...], cb_ref[...]) for bi in range(nb)]
    fwd = fwd_ref[...].astype(BF16)
    inv = inv_ref[...].astype(BF16)
    cur = [zz[2 * ch:] for zz in z]
    for order in range(2):
        spec = _dot(jnp.concatenate(cur, 1).astype(BF16), fwd)
        h = hf_ref[order]
        zr, zi, hr, hi = spec[:, 0:2 * l], spec[:, 2 * l:], h[:, 0:2 * l], h[:, 2 * l:]
        prod = jnp.concatenate([zr * hr - zi * hi, zr * hi + zi * hr], 1).astype(BF16)
        y = _dot(prod, inv)
        sk = sk_ref[:, order:order + 1]
        cur = [z[bi][order * ch:(order + 1) * ch] * (y[:, bi * l:(bi + 1) * l] + cur[bi] * sk) for bi in range(nb)]
    for bi in range(nb):
        o_ref[bi] = cur[bi]


def _hconv_ctx(ph, hf, conv_w, conv_b, skip):
    b, nch3, l = ph.shape
    assert b == 2
    fwd, inv = (jnp.asarray(m) for m in _const_dense(l)[4:6])
    cw = jnp.pad(conv_w.T, ((0, 0), (0, 5)))
    return pl.pallas_call(
        _hconv_ctx_kernel,
        in_specs=[pl.BlockSpec(cw.shape, lambda: (0, 0)),
                  pl.BlockSpec((nch3, 1), lambda: (0, 0)),
                  pl.BlockSpec((HY_CH, 2), lambda: (0, 0)),
                  pl.BlockSpec(ph.shape, lambda: (0, 0, 0)),
                  pl.BlockSpec(hf.shape, lambda: (0, 0, 0)),
                  pl.BlockSpec(fwd.shape, lambda: (0, 0)),
                  pl.BlockSpec(inv.shape, lambda: (0, 0))],
        out_specs=pl.BlockSpec((b, HY_CH, l), lambda: (0, 0, 0)),
        out_shape=jax.ShapeDtypeStruct((b, HY_CH, l), F32),
        compiler_params=pltpu.CompilerParams(vmem_limit_bytes=VMEM_LIMIT_BYTES),
        name="hyena_conv_ctx",
    )(cw, conv_b.reshape(-1, 1), skip.T, ph, hf, fwd, inv)


def _mix_ffn_kernel(*refs, n_mix, final, ff_chunk):
    d = D_MODEL
    h_ref, y_refs = refs[0], refs[1:1 + n_mix]
    wo_ref, mod_ref, g2_ref, wg_ref, wu_ref, wd_ref = refs[1 + n_mix:7 + n_mix]
    rest = refs[7 + n_mix:]
    y, row = None, 0
    for y_ref in y_refs:
        rows = y_ref.shape[0]
        part = lax.dot_general(y_ref[...].astype(BF16), wo_ref[row:row + rows, :], _TN, preferred_element_type=F32)
        y = part if y is None else y + part
        row += rows
    if final:
        fg_ref, o_ref = rest
    else:
        (o_ref,) = rest
    h1 = h_ref[...] + mod_ref[:, 2 * d:3 * d] * y
    u = _norm_mod(h1, g2_ref[...], mod_ref[:, 3 * d:4 * d], mod_ref[:, 4 * d:5 * d]).astype(BF16)
    acc = jnp.zeros(h1.shape, F32)
    for j in range(D_FF // ff_chunk):
        cols = slice(j * ff_chunk, (j + 1) * ff_chunk)
        act = (_silu(_dot(u, wg_ref[:, cols])) * _dot(u, wu_ref[:, cols])).astype(BF16)
        acc = acc + _dot(act, wd_ref[cols, :])
    h2 = h1 + mod_ref[:, 5 * d:6 * d] * acc
    if final:
        h2 = _rms(h2, fg_ref[...])
    o_ref[...] = h2


def _layer_resident(stacked, layer):
    nd = stacked.ndim - 1
    return pl.BlockSpec((None,) + stacked.shape[1:], lambda *_: (layer,) + (0,) * nd, pipeline_mode=pl.Buffered(1))


def _mix_ffn(h, ys, wo, mod3, g2, wg, wu, wd, layer, final_g=None, tl=512, ff_chunk=256):
    b, l, d = h.shape
    tl = min(tl, l)
    tok = pl.BlockSpec((None, tl, d), lambda i, t: (i, t, 0))
    y_specs = [pl.BlockSpec((None, y.shape[1], tl), lambda i, t: (i, 0, t)) for y in ys]
    in_specs = ([tok] + y_specs + [_resident(wo.shape), pl.BlockSpec((None, 1, 6 * d), lambda i, t: (i, 0, 0)),
                                    pl.BlockSpec((1, d), lambda i, t: (0, 0)),
                                    _layer_resident(wg, layer), _layer_resident(wu, layer),
                                    _layer_resident(wd, layer)])
    args = [h, *ys, wo, mod3, g2, wg, wu, wd]
    if final_g is not None:
        in_specs.append(pl.BlockSpec((1, d), lambda i, t: (0, 0)))
        args.append(final_g)
    return pl.pallas_call(
        functools.partial(_mix_ffn_kernel, n_mix=len(ys), final=final_g is not None, ff_chunk=ff_chunk),
        grid=(b, l // tl),
        in_specs=in_specs,
        out_specs=tok,
        out_shape=jax.ShapeDtypeStruct((b, l, d), F32),
        compiler_params=_params("parallel", "parallel"),
        name="mix_ffn_final" if final_g is not None else "mix_ffn",
    )(*args)


def _qkv_kernel(x_ref, g_ref, mod_ref, w_ref, q_ref, k_ref, v_ref):
    d = D_MODEL
    u = _norm_mod(x_ref[...], g_ref[...], mod_ref[:, 0:d], mod_ref[:, d:2 * d]).astype(BF16)
    q_ref[...] = (_dot(u, w_ref[:, 0:d]) * (NA_HEAD_DIM ** -0.5)).astype(BF16)
    k_ref[...] = _dot(u, w_ref[:, d:2 * d]).astype(BF16)
    v_ref[...] = _dot(u, w_ref[:, 2 * d:]).astype(BF16)


def _qkv(h, g, mod3, w, tl=1024):
    b, l, d = h.shape
    tl = min(tl, l)
    tok = pl.BlockSpec((None, tl, d), lambda i, t: (i, t, 0))
    return pl.pallas_call(
        _qkv_kernel,
        grid=(b, l // tl),
        in_specs=[tok, pl.BlockSpec((1, d), lambda i, t: (0, 0)),
                  pl.BlockSpec((None, 1, 6 * d), lambda i, t: (i, 0, 0)), _resident(w.shape)],
        out_specs=[tok, tok, tok],
        out_shape=[jax.ShapeDtypeStruct((b, l, d), BF16)] * 3,
        compiler_params=_params("parallel", "parallel"),
        name="qkv_proj",
    )(h, g, mod3, w)


NA_QROWS = 4
NA_KROWS = 12
NA_PATTERNS = 3
NA_GROUP = 4


def _na_key_start(r0, grid_rows):
    return jnp.minimum(jnp.clip(r0 - NA_KH // 2, 0, grid_rows - NA_KH), grid_rows - NA_KROWS)


def _na_pattern(pat, i, j):
    if pat == 0:
        return j - i + NA_KH - 1, j < NA_KH
    if pat == 1:
        return j - i + NA_KH // 2 - 1, i <= j < i + NA_KH
    return j - i - 1, j >= NA_KROWS - NA_KH


def _bias_blocks_kernel(rpb_ref, o_ref):
    pair = pl.program_id(0)
    ndr, ndc = 2 * NA_KH - 1, 2 * NA_KW - 1
    kcol = lax.broadcasted_iota(jnp.int32, (GRID_W, LANES), 0)
    lane = lax.broadcasted_iota(jnp.int32, (GRID_W, LANES), 1)
    q = lane % GRID_W
    upper = lane >= GRID_W
    wstart = jnp.clip(q - NA_KW // 2, 0, GRID_W - NA_KW)
    ok = (kcol >= wstart) & (kcol < wstart + NA_KW)
    dc = kcol - q + (NA_KW - 1)
    neg = jnp.full((GRID_W, LANES), NEG_BIG, F32)
    for e in range(2):
        base = (2 * pair + e) * ndr * ndc
        tiles = []
        for dr in range(ndr):
            acc = jnp.zeros((GRID_W, LANES), F32)
            for dd in range(ndc):
                acc = jnp.where(dc == dd, rpb_ref[base + dr * ndc + dd], acc)
            tiles.append(jnp.where(ok, acc, NEG_BIG))
        for pat in range(NA_PATTERNS):
            for j in range(NA_KROWS):
                for p in range(NA_QROWS // 2):
                    halves = []
                    for i in (2 * p, 2 * p + 1):
                        dr, visible = _na_pattern(pat, i, j)
                        halves.append(tiles[dr] if visible else neg)
                    o_ref[e, pat, j * GRID_W:(j + 1) * GRID_W, p * LANES:(p + 1) * LANES] = jnp.where(
                        upper, halves[1], halves[0])


def _bias_blocks(rpb):
    heads = rpb.shape[0]
    nq, nk = NA_QROWS * GRID_W, NA_KROWS * GRID_W
    return pl.pallas_call(
        _bias_blocks_kernel,
        grid=(heads // 2,),
        in_specs=[pl.BlockSpec(memory_space=pltpu.SMEM)],
        out_specs=pl.BlockSpec((None, 2, NA_PATTERNS, nk, nq), lambda p: (p, 0, 0, 0, 0)),
        out_shape=jax.ShapeDtypeStruct((heads // 2, 2, NA_PATTERNS, nk, nq), F32),
        compiler_params=_params("parallel"),
        name="na_bias_blocks",
    )(rpb.reshape(-1))


def _na_kernel(q_ref, k_ref, v_ref, kc_ref, vc_ref, bias_ref, o_ref, *, blocks_per_step, grid_rows):
    nq, nk = NA_QROWS * GRID_W, NA_KROWS * GRID_W
    lane = lax.broadcasted_iota(jnp.int32, (nq, LANES), 1)
    row = lax.broadcasted_iota(jnp.int32, (LANES, nq), 0)
    kc = kc_ref[...]
    vc = vc_ref[...]
    step = pl.program_id(2)
    for i in range(0, blocks_per_step, NA_GROUP):
        qs, kus, vts, pats = [], [], [], []
        for j in range(i, i + NA_GROUP):
            r0 = (step * blocks_per_step + j) * NA_QROWS
            pats.append(jnp.where(r0 == 0, 0, jnp.where(r0 == grid_rows - NA_QROWS, 2, 1)))
            koff = pl.multiple_of(_na_key_start(r0, grid_rows) * GRID_W, GRID_W)
            qs.append(q_ref[j * nq:(j + 1) * nq, :])
            kus.append(k_ref[pl.ds(koff, nk), :])
            vt = jnp.concatenate([v_ref[pl.ds(koff, nk), :], vc], 0).T
            vts.append(jnp.concatenate([vt, jnp.ones((16, vt.shape[1]), vt.dtype)], 0))
        ss = {}
        for jj in range(NA_GROUP):
            for e in range(2):
                head = (lane >= NA_HEAD_DIM) if e else (lane < NA_HEAD_DIM)
                qh = jnp.where(head, qs[jj], jnp.zeros_like(qs[jj]))
                sl = lax.dot_general(kus[jj], qh, _NT, preferred_element_type=F32) + bias_ref[e, pats[jj]]
                sc = lax.dot_general(kc, qh, _NT, preferred_element_type=F32)
                ss[jj, e] = jnp.concatenate([sl, sc], 0)
        ps = {key: jnp.exp((s - jnp.max(s, axis=0, keepdims=True)).astype(BF16)) for key, s in ss.items()}
        for jj in range(NA_GROUP):
            outs = []
            for e in range(2):
                o = _dot(vts[jj], ps[jj, e])
                outs.append(o[0:LANES] / o[LANES:LANES + 1])
            j = i + jj
            o_ref[:, j * nq:(j + 1) * nq] = jnp.where(row < NA_HEAD_DIM, outs[0], outs[1]).astype(o_ref.dtype)


def _na(q, k, v, kc, vc, bias_blocks, blocks_per_step=16):
    b, l, d = q.shape
    lc = kc.shape[1]
    grid_rows = l // GRID_W
    blocks_per_step = min(blocks_per_step, grid_rows // NA_QROWS)
    assert grid_rows % (NA_QROWS * blocks_per_step) == 0 and grid_rows >= NA_KROWS
    assert blocks_per_step % NA_GROUP == 0
    npairs = d // LANES
    tq = blocks_per_step * NA_QROWS * GRID_W
    return pl.pallas_call(
        functools.partial(_na_kernel, blocks_per_step=blocks_per_step, grid_rows=grid_rows),
        grid=(npairs, b, l // tq),
        in_specs=[pl.BlockSpec((None, tq, LANES), lambda p, i, t: (i, t, p)),
                  pl.BlockSpec((None, l, LANES), lambda p, i, t: (i, 0, p)),
                  pl.BlockSpec((None, l, LANES), lambda p, i, t: (i, 0, p)),
                  pl.BlockSpec((None, lc, LANES), lambda p, i, t: (i, 0, p)),
                  pl.BlockSpec((None, lc, LANES), lambda p, i, t: (i, 0, p)),
                  pl.BlockSpec((None,) + bias_blocks.shape[1:], lambda p, i, t: (p, 0, 0, 0, 0))],
        out_specs=pl.BlockSpec((None, LANES, tq), lambda p, i, t: (i, p, t)),
        out_shape=jax.ShapeDtypeStruct((b, d, l), BF16),
        compiler_params=_params("parallel", "parallel", "parallel"),
        name="neighbourhood_attention",
    )(q, k, v, kc, vc, bias_blocks)


def kernel(x, c, ctx, c_ctx, ada_w, ada_b, norm1_g, norm2_g, ffn_w_gate, ffn_w_up, ffn_w_down, ab_w_in, ab_w_out, hy_conv_w, hy_conv_b, hy_f_w1, hy_f_b1, hy_f_freq, hy_f_w2, hy_f_b2, hy_f_w3, hy_bias, na_w_qkv, na_w_out, na_rpb, final_g):
    b, l, d = x.shape
    lc = ctx.shape[1]
    assert b == 2 and d == D_MODEL and l % 1024 == 0 and ada_w.shape[0] == 2

    cond8 = jnp.zeros((8, d), F32).at[0:b].set(c).at[b].set(c_ctx)
    mod = _ada(cond8, ada_w, ada_b)
    mod_lat = [mod[i, 0:b].reshape(b, 1, 6 * d) for i in range(2)]
    mod_ctx = [jnp.broadcast_to(mod[i, b:b + 1], (b, 6 * d)).reshape(b, 1, 6 * d) for i in range(2)]
    n1 = [norm1_g[i].reshape(1, d) for i in range(2)]
    n2 = [norm2_g[i].reshape(1, d) for i in range(2)]
    ffn = (ffn_w_gate.astype(BF16), ffn_w_up.astype(BF16), ffn_w_down.astype(BF16))

    w_in_t = ab_w_in[0].T.astype(BF16)
    w_out = ab_w_out[0].astype(BF16)
    w1t = jnp.pad(hy_f_w1[0].T, ((0, 0), (0, LANES - HY_EMB)))
    col = lambda vec: vec.reshape(-1, 1)
    mlp = (w1t, col(hy_f_b1[0]), col(hy_f_freq[0]), hy_f_w2[0].T, col(hy_f_b2[0]),
           hy_f_w3[0].T.reshape(2, 2, HY_CH, HY_HID))

    zf, ph = _inproj(x, n1[0], mod_lat[0], w_in_t)
    y_f = _fourier_latent(zf)
    hf = _hfilter_latent(l, *mlp)
    y_h = _hconv_latent(ph, hf, hy_conv_w[0], hy_conv_b[0], hy_bias[0])
    h_lat = _mix_ffn(x, (y_f, y_h), w_out, mod_lat[0], n2[0], *ffn, layer=0)

    zfc, phc = _inproj(ctx, n1[0], mod_ctx[0], w_in_t)
    y_fc = _fourier_ctx(zfc)
    hfc = _hfilter_ctx(lc, *mlp)
    y_hc = _hconv_ctx(phc, hfc, hy_conv_w[0], hy_conv_b[0], hy_bias[0])
    h_ctx = _mix_ffn(ctx, (y_fc, y_hc), w_out, mod_ctx[0], n2[0], *ffn, layer=0)

    w_qkv = na_w_qkv[0].astype(BF16)
    q, k, v = _qkv(h_lat, n1[1], mod_lat[1], w_qkv)
    _, kc, vc = _qkv(h_ctx, n1[1], mod_ctx[1], w_qkv)
    attn = _na(q, k, v, kc, vc, _bias_blocks(na_rpb[0]))
    return _mix_ffn(h_lat, (attn,), na_w_out[0].astype(BF16), mod_lat[1], n2[1], *ffn, layer=1,
                    final_g=final_g.reshape(1, d))
```
